```python
import math
import jax, jax.numpy as jnp
from jax import lax
import numpy as np

D_MODEL = 4096
BATCH = 32
SEQ = 256
DEPTH = 2
DEC_BATCH = 4
DEC_SEQ = 1024
PAST_LEN = 256

GRID_W = 64
HEAD_DIM = 128
N_Q_HEADS = (D_MODEL // 2) // HEAD_DIM
N_KV_HEADS = N_Q_HEADS // 4
GQA_GROUP = N_Q_HEADS // N_KV_HEADS
ATT_W = N_Q_HEADS * HEAD_DIM
KV_W = N_KV_HEADS * HEAD_DIM
WINDOW = 128
BLOCK = 128
ROPE_BASE = 10000.0
ROT_PER_AXIS = HEAD_DIM // 2
ROT_FREQS = ROT_PER_AXIS // 2
D_H = D_MODEL // 4
HY_ORDER = 2
SHORT_CONV = 3
POS_BANDS = 16
POS_DIM = 1 + 2 * POS_BANDS
FILT_HID = 64
DECAY_MIN = math.log(1e-2) / 1.5
DECAY_MAX = math.log(1e-2) / 0.3
D_G = D_MODEL // 4
CHUNK = 128
N_GM_GROUPS = D_G // CHUNK
GM_GROUP_W = D_G // N_GM_GROUPS
N_BRANCH = 3
D_FF = 4 * D_MODEL
IN_COLS = 3 * D_H + ATT_W + 2 * KV_W + 2 * D_G
EPS = 1e-6
NEG = -1e30
ATT_SCALE = HEAD_DIM ** -0.5

kernel_name = 'hybrid_hyena_swa_gmlp_diffusion_step'

F32 = jnp.float32


def rms_norm(x, g):
    xf = x.astype(F32)
    y = xf * lax.rsqrt(jnp.mean(xf * xf, axis=-1, keepdims=True) + EPS)
    return (y * g.astype(F32)).astype(x.dtype)


def layer_norm(x, g, b):
    xf = x.astype(F32)
    mu = jnp.mean(xf, axis=-1, keepdims=True)
    var = jnp.mean(jnp.square(xf - mu), axis=-1, keepdims=True)
    y = (xf - mu) * lax.rsqrt(var + EPS) * g.astype(F32) + b.astype(F32)
    return y.astype(x.dtype)


def rope_tables(L):
    rows = L // GRID_W
    row = jnp.repeat(jnp.arange(rows), GRID_W)
    col = jnp.tile(jnp.arange(GRID_W), rows)
    inv = ROPE_BASE ** (-jnp.arange(ROT_FREQS, dtype=F32) / ROT_FREQS)
    pos = jnp.stack([row, col], axis=-1).astype(F32)
    ang = pos[:, :, None] * inv
    return jnp.cos(ang), jnp.sin(ang)


def apply_rope(x, cos, sin):
    B, L, H, _ = x.shape
    xr = x.astype(F32).reshape(B, L, H, 2, 2, ROT_FREQS)
    x1, x2 = xr[..., 0, :], xr[..., 1, :]
    c = cos[None, :, None]
    s = sin[None, :, None]
    out = jnp.stack([x1 * c - x2 * s, x2 * c + x1 * s], axis=-2)
    return out.reshape(x.shape).astype(x.dtype)


def short_conv(u, w, b):
    up = jnp.pad(u, ((0, 0), (1, 1), (0, 0)))
    return up[:, :-2] * w[0] + up[:, 1:-1] * w[1] + up[:, 2:] * w[2] + b


def hyena_filter_fft(L, p):
    t = jnp.linspace(0.0, 1.0, L, dtype=F32)
    w = 2.0 * math.pi * jnp.arange(L, dtype=F32) / L
    bands = jnp.linspace(1e-4, POS_BANDS - 1, POS_BANDS, dtype=F32)
    z = jnp.concatenate([t[:, None], jnp.cos(w[:, None] * bands), -jnp.sin(w[:, None] * bands)], axis=-1)
    a = jnp.sin(p['flt_f1'].astype(F32) * (z @ p['flt_w1'].astype(F32) + p['flt_b1'].astype(F32)))
    a = jnp.sin(p['flt_f2'].astype(F32) * (a @ p['flt_w2'].astype(F32) + p['flt_b2'].astype(F32)))
    hf = (a @ p['flt_w3'].astype(F32)).reshape(L, HY_ORDER, 2, D_H)
    deltas = jnp.abs(jnp.linspace(DECAY_MIN, DECAY_MAX, D_H, dtype=F32))
    hf = hf * jnp.exp(-t[:, None] * deltas)[:, None, None, :]
    fwd = hf[:, :, 0]
    bwd = hf[:, :, 1]
    two_sided = jnp.concatenate([fwd, jnp.zeros((1, HY_ORDER, D_H), F32), bwd[:0:-1]], axis=0)
    return jnp.fft.rfft(two_sided, axis=0)


def hyena(u, p):
    B, L, _ = u.shape
    u = short_conv(u, p['conv_w'], p['conv_b'])
    v, x1, x2 = jnp.split(u.astype(F32), 3, axis=-1)
    hf = hyena_filter_fft(L, p)
    skip = p['hy_skip'].astype(F32)
    z = v
    for o, gate in enumerate((x1, x2)):
        y = jnp.fft.irfft(jnp.fft.rfft(z, n=2 * L, axis=1) * hf[:, o], n=2 * L, axis=1)[:, :L]
        z = gate * (y + skip[o] * z)
    return z.astype(u.dtype)


def chunk_gmlp(g, p):
    B, L, _ = g.shape
    u, v = jnp.split(g, 2, axis=-1)
    v = layer_norm(v, p['gm_ln_g'], p['gm_ln_b'])
    vb = v.reshape(B, L // CHUNK, CHUNK, N_GM_GROUPS, GM_GROUP_W)
    mixed = jnp.einsum('gpq,bnqgc->bnpgc', p['gm_ws'], vb) + p['gm_bs'].T[None, None, :, :, None]
    return u * mixed.reshape(B, L, D_G)


def context_attention(q, k, v, sink):
    B, C, _, _ = q.shape
    qb = q.reshape(B, C // BLOCK, BLOCK, N_KV_HEADS, GQA_GROUP, HEAD_DIM).transpose(1, 0, 2, 3, 4, 5)
    sink_l = sink.reshape(N_KV_HEADS, GQA_GROUP).astype(F32)[None, :, :, None, None]

    def one_block(qblk):
        s = jnp.einsum('bqkgd,bckd->bkgqc', qblk, k).astype(F32) * ATT_SCALE
        s = jnp.concatenate([s, jnp.broadcast_to(sink_l, s.shape[:-1] + (1,))], axis=-1)
        pr = jax.nn.softmax(s, axis=-1)[..., :C]
        return jnp.einsum('bkgqc,bckd->bqkgd', pr.astype(v.dtype), v)

    o = lax.map(one_block, qb)
    return o.transpose(1, 0, 2, 3, 4, 5).reshape(B, C, ATT_W)


def window_attention(q, k, v, k_ctx, v_ctx, sink):
    B, L, _, _ = q.shape
    nb = L // BLOCK
    C = k_ctx.shape[1]
    qb = q.reshape(B, nb, BLOCK, N_KV_HEADS, GQA_GROUP, HEAD_DIM)
    pad = ((0, 0), (BLOCK, BLOCK), (0, 0), (0, 0))
    kp = jnp.pad(k, pad).reshape(B, nb + 2, BLOCK, N_KV_HEADS, HEAD_DIM)
    vp = jnp.pad(v, pad).reshape(B, nb + 2, BLOCK, N_KV_HEADS, HEAD_DIM)
    kw = jnp.concatenate([kp[:, :-2], kp[:, 1:-1], kp[:, 2:]], axis=2)
    vw = jnp.concatenate([vp[:, :-2], vp[:, 1:-1], vp[:, 2:]], axis=2)
    blk = jnp.arange(nb)[:, None, None]
    qpos = blk * BLOCK + jnp.arange(BLOCK)[None, :, None]
    kpos = blk * BLOCK - BLOCK + jnp.arange(3 * BLOCK)[None, None, :]
    mask = (jnp.abs(kpos - qpos) <= WINDOW) & (kpos >= 0) & (kpos < L)
    s_band = jnp.einsum('bnqkgd,bnjkd->bnkgqj', qb, kw).astype(F32) * ATT_SCALE
    s_band = jnp.where(mask[None, :, None, None], s_band, NEG)
    s_ctx = jnp.einsum('bnqkgd,bckd->bnkgqc', qb, k_ctx).astype(F32) * ATT_SCALE
    sink_l = sink.reshape(N_KV_HEADS, GQA_GROUP).astype(F32)[None, None, :, :, None, None]
    s = jnp.concatenate([s_ctx, s_band, jnp.broadcast_to(sink_l, s_band.shape[:-1] + (1,))], axis=-1)
    pr = jax.nn.softmax(s, axis=-1).astype(v.dtype)
    o = (jnp.einsum('bnkgqc,bckd->bnqkgd', pr[..., :C], v_ctx)
         + jnp.einsum('bnkgqj,bnjkd->bnqkgd', pr[..., C:C + 3 * BLOCK], vw))
    return o.reshape(B, L, ATT_W)


def trunk_layer(x, mod, p, ctx_k, ctx_v, latent):
    sh1, sc1, g1, sh2, sc2, g2 = jnp.split(mod, 6, axis=-1)
    B, L, _ = x.shape
    h = rms_norm(x, p['norm1_g']) * (1.0 + sc1) + sh1
    proj = h @ p['w_in']
    o1 = 3 * D_H
    o2 = o1 + ATT_W
    o3 = o2 + KV_W
    o4 = o3 + KV_W
    hy_in, q, k, v, gm_in = jnp.split(proj, [o1, o2, o3, o4], axis=-1)
    q = rms_norm(q.reshape(B, L, N_Q_HEADS, HEAD_DIM), p['q_norm_g'])
    k = rms_norm(k.reshape(B, L, N_KV_HEADS, HEAD_DIM), p['k_norm_g'])
    v = v.reshape(B, L, N_KV_HEADS, HEAD_DIM)
    if latent:
        cos, sin = rope_tables(L)
        q = apply_rope(q, cos, sin)
        k = apply_rope(k, cos, sin)
        att = window_attention(q, k, v, ctx_k, ctx_v, p['sink'])
    else:
        att = context_attention(q, k, v, p['sink'])
    hy = hyena(hy_in, p)
    gm = chunk_gmlp(gm_in, p)
    gates = jax.nn.sigmoid(h @ p['w_gate'] + p['b_gate'])
    ga, gb, gc = jnp.split(gates, N_BRANCH, axis=-1)
    merged = ga * (hy @ p['w_p_hy']) + gb * (att @ p['w_p_at']) + gc * (gm @ p['w_p_gm'])
    x = x + g1 * (merged @ p['w_out'])
    h2 = rms_norm(x, p['norm2_g']) * (1.0 + sc2) + sh2
    x = x + g2 * (jnp.square(jax.nn.relu(h2 @ p['w_up'])) @ p['w_down'])
    return x, k, v


def setup_inputs(seed: int = 0) -> dict:
    key = jax.random.key(seed)
    ks = iter(jax.random.split(key, 48))

    def nrm(shape, std):
        return std * jax.random.normal(next(ks), shape, F32)

    def gain(shape):
        return 1.0 + nrm(shape, 0.02)

    return {
        'x_prompt': nrm((BATCH, SEQ, D_MODEL), 1.0),
        'x_sample': nrm((DEC_BATCH, DEC_SEQ, D_MODEL), 1.0),
        'cache_k': nrm((DEC_BATCH, DEPTH, PAST_LEN, N_KV_HEADS, HEAD_DIM), 1.0),
        'cache_v': nrm((DEC_BATCH, DEPTH, PAST_LEN, N_KV_HEADS, HEAD_DIM), 1.0),
        'c': nrm((DEC_BATCH, D_MODEL), 1.0),
        'c_ctx': nrm((D_MODEL,), 1.0),
        'w_mod': nrm((DEPTH, D_MODEL, 6 * D_MODEL), 0.5 * D_MODEL ** -0.5),
        'b_mod': nrm((DEPTH, 6 * D_MODEL), 0.02),
        'norm1_g': gain((DEPTH, D_MODEL)),
        'norm2_g': gain((DEPTH, D_MODEL)),
        'w_in': nrm((DEPTH, D_MODEL, IN_COLS), D_MODEL ** -0.5),
        'conv_w': nrm((DEPTH, SHORT_CONV, 3 * D_H), SHORT_CONV ** -0.5),
        'conv_b': nrm((DEPTH, 3 * D_H), 0.02),
        'flt_w1': nrm((DEPTH, POS_DIM, FILT_HID), POS_DIM ** -0.5),
        'flt_b1': nrm((DEPTH, FILT_HID), 0.02),
        'flt_f1': gain((DEPTH, FILT_HID)),
        'flt_w2': nrm((DEPTH, FILT_HID, FILT_HID), FILT_HID ** -0.5),
        'flt_b2': nrm((DEPTH, FILT_HID), 0.02),
        'flt_f2': gain((DEPTH, FILT_HID)),
        'flt_w3': nrm((DEPTH, FILT_HID, HY_ORDER * 2 * D_H), 0.02),
        'hy_skip': nrm((DEPTH, HY_ORDER, D_H), 0.5),
        'q_norm_g': gain((DEPTH, HEAD_DIM)),
        'k_norm_g': gain((DEPTH, HEAD_DIM)),
        'sink': nrm((DEPTH, N_Q_HEADS), 0.5),
        'gm_ln_g': gain((DEPTH, D_G)),
        'gm_ln_b': nrm((DEPTH, D_G), 0.02),
        'gm_ws': nrm((DEPTH, N_GM_GROUPS, CHUNK, CHUNK), CHUNK ** -0.5),
        'gm_bs': gain((DEPTH, N_GM_GROUPS, CHUNK)),
        'w_p_hy': nrm((DEPTH, D_H, D_MODEL), D_H ** -0.5),
        'w_p_at': nrm((DEPTH, ATT_W, D_MODEL), ATT_W ** -0.5),
        'w_p_gm': nrm((DEPTH, D_G, D_MODEL), D_G ** -0.5),
        'w_gate': nrm((DEPTH, D_MODEL, N_BRANCH * D_MODEL), D_MODEL ** -0.5),
        'b_gate': nrm((DEPTH, N_BRANCH * D_MODEL), 0.02),
        'w_out': nrm((DEPTH, D_MODEL, D_MODEL), D_MODEL ** -0.5),
        'w_up': nrm((DEPTH, D_MODEL, D_FF), D_MODEL ** -0.5),
        'w_down': nrm((DEPTH, D_FF, D_MODEL), D_FF ** -0.5),
    }


def reference(x_prompt, x_sample, cache_k, cache_v, c, c_ctx, w_mod, b_mod, norm1_g, norm2_g,
              w_in, conv_w, conv_b, flt_w1, flt_b1, flt_f1, flt_w2, flt_b2, flt_f2, flt_w3,
              hy_skip, q_norm_g, k_norm_g, sink, gm_ln_g, gm_ln_b, gm_ws, gm_bs,
              w_p_hy, w_p_at, w_p_gm, w_gate, b_gate, w_out, w_up, w_down):
    y_p = x_prompt
    y_s = x_sample
    ks_new = []
    vs_new = []
    for l in range(DEPTH):
        p = {
            'norm1_g': norm1_g[l], 'norm2_g': norm2_g[l], 'w_in': w_in[l],
            'conv_w': conv_w[l], 'conv_b': conv_b[l],
            'flt_w1': flt_w1[l], 'flt_b1': flt_b1[l], 'flt_f1': flt_f1[l],
            'flt_w2': flt_w2[l], 'flt_b2': flt_b2[l], 'flt_f2': flt_f2[l], 'flt_w3': flt_w3[l],
            'hy_skip': hy_skip[l], 'q_norm_g': q_norm_g[l], 'k_norm_g': k_norm_g[l], 'sink': sink[l],
            'gm_ln_g': gm_ln_g[l], 'gm_ln_b': gm_ln_b[l], 'gm_ws': gm_ws[l], 'gm_bs': gm_bs[l],
            'w_p_hy': w_p_hy[l], 'w_p_at': w_p_at[l], 'w_p_gm': w_p_gm[l],
            'w_gate': w_gate[l], 'b_gate': b_gate[l], 'w_out': w_out[l],
            'w_up': w_up[l], 'w_down': w_down[l],
        }
        mod_ctx = jax.nn.silu(c_ctx) @ w_mod[l] + b_mod[l]
        y_p, k_l, v_l = trunk_layer(y_p, mod_ctx, p, None, None, False)
        ks_new.append(k_l)
        vs_new.append(v_l)
        mod_lat = (jax.nn.silu(c) @ w_mod[l] + b_mod[l])[:, None, :]
        y_s, _, _ = trunk_layer(y_s, mod_lat, p, cache_k[:, l], cache_v[:, l], True)
    new_k = jnp.stack(ks_new, axis=1)
    new_v = jnp.stack(vs_new, axis=1)
    return (y_p, y_s, new_k, new_v)
```

```python
import functools
import math

import jax
import jax.numpy as jnp
from jax import lax
from jax.experimental import pallas as pl
from jax.experimental.pallas import tpu as pltpu

F32 = jnp.float32
BF16 = jnp.bfloat16

D_MODEL = 4096
HEAD_DIM = 128
N_Q_HEADS = 16
N_KV_HEADS = 4
GQA_GROUP = 4
ATT_W = N_Q_HEADS * HEAD_DIM
KV_W = N_KV_HEADS * HEAD_DIM
GRID_W = 64
WINDOW = 128
BLOCK = 128
ROPE_BASE = 10000.0
ROT_FREQS = 32
D_H = 1024
HY_ORDER = 2
POS_BANDS = 16
POS_DIM = 1 + 2 * POS_BANDS
FILT_HID = 64
DECAY_MIN = math.log(1e-2) / 1.5
DECAY_MAX = math.log(1e-2) / 0.3
D_G = 1024
CHUNK = 128
N_GM_GROUPS = 8
D_FF = 4 * D_MODEL
IN_COLS = 3 * D_H + ATT_W + 2 * KV_W + 2 * D_G
EPS = 1e-6
NEG = -1e30
ATT_SCALE = HEAD_DIM ** -0.5

COL_HY = 0
COL_Q = 3 * D_H
COL_K = COL_Q + ATT_W
COL_V = COL_K + KV_W
COL_GM = COL_V + KV_W

V7X_LANES = 128
V7X_VMEM_BYTES = 64 * 1024 * 1024
VMEM_LIMIT = 56 * 1024 * 1024

TM = 1024
TN = 512
TK = 4096
NORM_ROWS = 256
HY_TC = 256
GM_ROWS = 512
MOD_TN = 512
PAD_ROWS = 8


def _params(sem):
    return pltpu.CompilerParams(dimension_semantics=sem, vmem_limit_bytes=VMEM_LIMIT)


def _split(a):
    hi = a.astype(BF16)
    lo = (a - hi.astype(F32)).astype(BF16)
    return hi, lo


def _dot(a, b):
    return jnp.dot(a, b, preferred_element_type=F32)


def _dot3(a_hi, a_lo, b_hi, b_lo):
    return _dot(a_hi, b_hi) + _dot(a_lo, b_hi) + _dot(a_hi, b_lo)


def _mod_kernel(c_ref, w_ref, b_ref, o_ref):
    c = c_ref[...]
    x = c * jax.nn.sigmoid(c)
    x_hi, x_lo = _split(x)
    w_hi, w_lo = _split(w_ref[...])
    o_ref[...] = _dot3(x_hi, x_lo, w_hi, w_lo) + b_ref[...]


def _modulation(c_rows, w_mod, b_mod):
    depth, d, n = w_mod.shape
    return pl.pallas_call(
        _mod_kernel,
        grid=(depth, n // MOD_TN),
        in_specs=[
            pl.BlockSpec((PAD_ROWS, d), lambda l, j: (0, 0)),
            pl.BlockSpec((None, d, MOD_TN), lambda l, j: (l, 0, j)),
            pl.BlockSpec((None, 1, MOD_TN), lambda l, j: (l, 0, j)),
        ],
        out_specs=pl.BlockSpec((None, PAD_ROWS, MOD_TN), lambda l, j: (l, 0, j)),
        out_shape=jax.ShapeDtypeStruct((depth, PAD_ROWS, n), F32),
        compiler_params=_params(("arbitrary", "arbitrary")),
        name="modulation",
    )(c_rows, w_mod, b_mod.reshape(depth, 1, n))


def _normmod_kernel(x_ref, g_ref, sc_ref, sh_ref, o_ref):
    x = x_ref[...]
    y = x * lax.rsqrt(jnp.mean(x * x, axis=-1, keepdims=True) + EPS) * g_ref[...]
    o_ref[...] = (y * (1.0 + sc_ref[...]) + sh_ref[...]).astype(o_ref.dtype)


def _normmod(x, g, mods, seg0, rows_per_seg, sh_chunk, sc_chunk):
    m, d = x.shape
    tiles_per_seg = rows_per_seg // NORM_ROWS
    seg = lambda i: seg0 + i // tiles_per_seg
    return pl.pallas_call(
        _normmod_kernel,
        grid=(m // NORM_ROWS,),
        in_specs=[
            pl.BlockSpec((NORM_ROWS, d), lambda i: (i, 0)),
            pl.BlockSpec((1, d), lambda i: (0, 0)),
            pl.BlockSpec((None, 1, d), lambda i: (seg(i), 0, sc_chunk)),
            pl.BlockSpec((None, 1, d), lambda i: (seg(i), 0, sh_chunk)),
        ],
        out_specs=pl.BlockSpec((NORM_ROWS, d), lambda i: (i, 0)),
        out_shape=jax.ShapeDtypeStruct((m, d), BF16),
        compiler_params=_params(("arbitrary",)),
        name="normmod",
    )(x, g.reshape(1, d), mods, mods)


def _mm_kernel(x_ref, w_ref, o_ref):
    o_ref[...] = _dot(x_ref[...], w_ref[...]).astype(o_ref.dtype)


def _mm_sigmoid_kernel(x_ref, w_ref, b_ref, o_ref):
    o_ref[...] = jax.nn.sigmoid(_dot(x_ref[...], w_ref[...]) + b_ref[...]).astype(o_ref.dtype)


def _mm_relu2_kernel(x_ref, w_ref, o_ref):
    a = jnp.maximum(_dot(x_ref[...], w_ref[...]), 0.0)
    o_ref[...] = (a * a).astype(o_ref.dtype)


def _matmul(x, w, kernel, out_dtype, name, bias=None):
    m, k = x.shape
    n = w.shape[1]
    in_specs = [
        pl.BlockSpec((TM, k), lambda i, j: (i, 0)),
        pl.BlockSpec((k, TN), lambda i, j: (0, j)),
    ]
    args = [x, w]
    if bias is not None:
        in_specs.append(pl.BlockSpec((1, TN), lambda i, j: (0, j)))
        args.append(bias.reshape(1, n))
    return pl.pallas_call(
        kernel,
        grid=(m // TM, n // TN),
        in_specs=in_specs,
        out_specs=pl.BlockSpec((TM, TN), lambda i, j: (i, j)),
        out_shape=jax.ShapeDtypeStruct((m, n), out_dtype),
        compiler_params=_params(("arbitrary", "arbitrary")),
        name=name,
    )(*args)


def _mm_residual_kernel(a_ref, w_ref, x_ref, g_ref, o_ref, acc_ref, *, nk):
    kk = pl.program_id(2)
    part = _dot(a_ref[...], w_ref[...])

    @pl.when(kk == 0)
    def _():
        acc_ref[...] = part

    @pl.when(kk > 0)
    def _():
        acc_ref[...] += part

    @pl.when(kk == nk - 1)
    def _():
        o_ref[...] = x_ref[...] + g_ref[...] * acc_ref[...]


def _matmul_residual(a, w, x, mods, seg0, rows_per_seg, g_chunk, name):
    m, k = a.shape
    n = w.shape[1]
    tk = min(k, TK)
    nk = k // tk
    tiles_per_seg = rows_per_seg // TM
    seg = lambda i: seg0 + i // tiles_per_seg
    g_blocks = n // TN
    return pl.pallas_call(
        functools.partial(_mm_residual_kernel, nk=nk),
        grid=(m // TM, n // TN, nk),
        in_specs=[
            pl.BlockSpec((TM, tk), lambda i, j, kk: (i, kk)),
            pl.BlockSpec((tk, TN), lambda i, j, kk: (kk, j)),
            pl.BlockSpec((TM, TN), lambda i, j, kk: (i, j)),
            pl.BlockSpec((None, 1, TN), lambda i, j, kk: (seg(i), 0, g_chunk * g_blocks + j)),
        ],
        out_specs=pl.BlockSpec((TM, TN), lambda i, j, kk: (i, j)),
        out_shape=jax.ShapeDtypeStruct((m, n), F32),
        scratch_shapes=[pltpu.VMEM((TM, TN), F32)],
        compiler_params=_params(("arbitrary", "arbitrary", "arbitrary")),
        name=name,
    )(a, w, x, mods)


def _merge_kernel(hy_ref, at_ref, gm_ref, whv_ref, wat_ref, wgm_ref, ga_ref, gb_ref, gc_ref, o_ref):
    merged = (ga_ref[...] * _dot(hy_ref[...], whv_ref[...])
              + gb_ref[...] * _dot(at_ref[...], wat_ref[...])
              + gc_ref[...] * _dot(gm_ref[...], wgm_ref[...]))
    o_ref[...] = merged.astype(o_ref.dtype)


def _merge(hy, at, gm, w_hy, w_at, w_gm, gates):
    m = hy.shape[0]
    n = w_hy.shape[1]
    nb = n // TN
    row = lambda width: pl.BlockSpec((TM, width), lambda i, j: (i, 0))
    col = lambda depth: pl.BlockSpec((depth, TN), lambda i, j: (0, j))
    gate = lambda b: pl.BlockSpec((TM, TN), lambda i, j: (i, b * nb + j))
    return pl.pallas_call(
        _merge_kernel,
        grid=(m // TM, nb),
        in_specs=[row(hy.shape[1]), row(at.shape[1]), row(gm.shape[1]),
                  col(w_hy.shape[0]), col(w_at.shape[0]), col(w_gm.shape[0]),
                  gate(0), gate(1), gate(2)],
        out_specs=pl.BlockSpec((TM, TN), lambda i, j: (i, j)),
        out_shape=jax.ShapeDtypeStruct((m, n), BF16),
        compiler_params=_params(("arbitrary", "arbitrary")),
        name="merge",
    )(hy, at, gm, w_hy, w_at, w_gm, gates, gates, gates)


def _dft_tables(L):
    k = jnp.arange(L, dtype=jnp.int32)
    ks = (k[:, None] * k[None, :]) % (2 * L)
    ang = ks.astype(F32) * (math.pi / L)
    return _split(jnp.cos(ang)) + _split(jnp.sin(ang))


def _filter_features(L):
    t = jnp.linspace(0.0, 1.0, L, dtype=F32)
    w = 2.0 * math.pi * jnp.arange(L, dtype=F32) / L
    bands = jnp.linspace(1e-4, POS_BANDS - 1, POS_BANDS, dtype=F32)
    z = jnp.concatenate([t[:, None], jnp.cos(w[:, None] * bands), -jnp.sin(w[:, None] * bands)], axis=-1)
    return jnp.pad(z, ((0, 0), (0, V7X_LANES - POS_DIM))), t[:, None]


def _filter_kernel(z_ref, t_ref, w1_ref, b1_ref, f1_ref, w2_ref, b2_ref, f2_ref,
                   w3f_ref, w3b_ref, dl_ref, chi_ref, clo_ref, shi_ref, slo_ref, o_ref, *, L):
    z_hi, z_lo = _split(z_ref[...])
    a = jnp.sin(f1_ref[...] * (_dot3(z_hi, z_lo, *_split(w1_ref[...])) + b1_ref[...]))
    a = jnp.sin(f2_ref[...] * (_dot3(*_split(a), *_split(w2_ref[...])) + b2_ref[...]))
    a_hi, a_lo = _split(a)
    decay = jnp.exp(-t_ref[...] * dl_ref[...])
    row = lax.broadcasted_iota(jnp.int32, (L, 1), 0)
    alt = jnp.where(row % 2 == 0, 1.0, -1.0).astype(F32)
    for o in range(HY_ORDER):
        fwd = _dot3(a_hi, a_lo, *_split(w3f_ref[o])) * decay
        bwd = _dot3(a_hi, a_lo, *_split(w3b_ref[o])) * decay
        s = fwd + jnp.where(row == 0, 0.0, bwd)
        d = bwd - fwd
        hr = _dot3(chi_ref[...], clo_ref[...], *_split(s)) * (1.0 / L)
        hi = _dot3(shi_ref[...], slo_ref[...], *_split(d)) * (1.0 / L)
        nyq = jnp.sum(alt * s, axis=0, keepdims=True) * (0.5 / L)
        o_ref[o, 0] = jnp.where(row == 0, 0.5 * hr, hr)
        o_ref[o, 1] = hi
        o_ref[o, 2] = jnp.where(row == 0, nyq, hr)


def _filter_spectra(L, p, tables):
    z, t = _filter_features(L)
    w1 = jnp.pad(p['flt_w1'], ((0, V7X_LANES - POS_DIM), (0, 0)))
    w3 = p['flt_w3'].reshape(FILT_HID, HY_ORDER, 2, D_H)
    w3f = jnp.transpose(w3[:, :, 0], (1, 0, 2))
    w3b = jnp.transpose(w3[:, :, 1], (1, 0, 2))
    deltas = jnp.abs(jnp.linspace(DECAY_MIN, DECAY_MAX, D_H, dtype=F32)).reshape(1, D_H)
    full = lambda shape: pl.BlockSpec(shape, lambda c: (0,) * len(shape))
    return pl.pallas_call(
        functools.partial(_filter_kernel, L=L),
        grid=(D_H // HY_TC,),
        in_specs=[
            full((L, V7X_LANES)), full((L, 1)),
            full((V7X_LANES, FILT_HID)), full((1, FILT_HID)), full((1, FILT_HID)),
            full((FILT_HID, FILT_HID)), full((1, FILT_HID)), full((1, FILT_HID)),
            pl.BlockSpec((HY_ORDER, FILT_HID, HY_TC), lambda c: (0, 0, c)),
            pl.BlockSpec((HY_ORDER, FILT_HID, HY_TC), lambda c: (0, 0, c)),
            pl.BlockSpec((1, HY_TC), lambda c: (0, c)),
            full((L, L)), full((L, L)), full((L, L)), full((L, L)),
        ],
        out_specs=pl.BlockSpec((HY_ORDER, 3, L, HY_TC), lambda c: (0, 0, 0, c)),
        out_shape=jax.ShapeDtypeStruct((HY_ORDER, 3, L, D_H), F32),
        compiler_params=_params(("arbitrary",)),
        name=f"hyena_filter_{L}",
    )(z, t, w1, p['flt_b1'].reshape(1, -1), p['flt_f1'].reshape(1, -1),
      p['flt_w2'], p['flt_b2'].reshape(1, -1), p['flt_f2'].reshape(1, -1),
      w3f, w3b, deltas, *tables)


def _hyena_kernel(v_ref, x1_ref, x2_ref, wv_ref, wx1_ref, wx2_ref, bv_ref, bx1_ref, bx2_ref,
                  tab_ref, skip_ref, chi_ref, clo_ref, shi_ref, slo_ref, o_ref, *, L):
    row = lax.broadcasted_iota(jnp.int32, (L, 1), 0)
    alt = jnp.where(row % 2 == 0, 1.0, -1.0).astype(F32)

    def short_conv(u_ref, w_ref, b_ref):
        u = u_ref[...]
        prev = jnp.where(row == 0, 0.0, pltpu.roll(u, 1, 0))
        nxt = jnp.where(row == L - 1, 0.0, pltpu.roll(u, L - 1, 0))
        return prev * w_ref[0:1, :] + u * w_ref[1:2, :] + nxt * w_ref[2:3, :] + b_ref[...]

    z = short_conv(v_ref, wv_ref, bv_ref)
    gates = (short_conv(x1_ref, wx1_ref, bx1_ref), short_conv(x2_ref, wx2_ref, bx2_ref))
    for o in range(HY_ORDER):
        z_hi, z_lo = _split(z)
        re = _dot3(chi_ref[...], clo_ref[...], z_hi, z_lo)
        im = -_dot3(shi_ref[...], slo_ref[...], z_hi, z_lo)
        nyq = jnp.sum(alt * z, axis=0, keepdims=True)
        im = jnp.where(row == 0, nyq, im)
        p, q, r = tab_ref[o, 0], tab_ref[o, 1], tab_ref[o, 2]
        ya = re * p - im * q
        yb = re * q + im * r
        y = (_dot3(chi_ref[...], clo_ref[...], *_split(ya))
             - _dot3(shi_ref[...], slo_ref[...], *_split(yb))
             + alt * yb[0:1, :])
        z = gates[o] * (y + skip_ref[o:o + 1, :] * z)
    o_ref[...] = z.astype(o_ref.dtype)


def _hyena(proj, L, spectra, tables, p):
    m = proj.shape[0]
    nc = D_H // HY_TC
    u = lambda part: pl.BlockSpec((L, HY_TC), lambda c, b: (b, part * nc + c))
    cw = lambda part: pl.BlockSpec((3, HY_TC), lambda c, b: (0, part * nc + c))
    cb = lambda part: pl.BlockSpec((1, HY_TC), lambda c, b: (0, part * nc + c))
    full = pl.BlockSpec((L, L), lambda c, b: (0, 0))
    return pl.pallas_call(
        functools.partial(_hyena_kernel, L=L),
        grid=(nc, m // L),
        in_specs=[u(0), u(1), u(2), cw(0), cw(1), cw(2), cb(0), cb(1), cb(2),
                  pl.BlockSpec((HY_ORDER, 3, L, HY_TC), lambda c, b: (0, 0, 0, c)),
                  pl.BlockSpec((HY_ORDER, HY_TC), lambda c, b: (0, c)),
                  full, full, full, full],
        out_specs=pl.BlockSpec((L, HY_TC), lambda c, b: (b, c)),
        out_shape=jax.ShapeDtypeStruct((m, D_H), BF16),
        compiler_params=_params(("arbitrary", "arbitrary")),
        name=f"hyena_{L}",
    )(proj, proj, proj, p['conv_w'], p['conv_w'], p['conv_w'],
      p['conv_b'].reshape(1, -1), p['conv_b'].reshape(1, -1), p['conv_b'].reshape(1, -1),
      spectra, p['hy_skip'], *tables)


def _gmlp_kernel(u_ref, v_ref, lg_ref, lb_ref, ws_ref, bs_ref, o_ref):
    v = v_ref[...]
    mu = jnp.mean(v, axis=-1, keepdims=True)
    vc = v - mu
    var = jnp.mean(vc * vc, axis=-1, keepdims=True)
    vn = (vc * lax.rsqrt(var + EPS) * lg_ref[...] + lb_ref[...]).astype(BF16)
    for n in range(GM_ROWS // CHUNK):
        rows = slice(n * CHUNK, (n + 1) * CHUNK)
        for g in range(N_GM_GROUPS):
            cols = slice(g * CHUNK, (g + 1) * CHUNK)
            mixed = _dot(ws_ref[g], vn[rows, cols]) + bs_ref[:, g:g + 1]
            o_ref[rows, cols] = (u_ref[rows, cols] * mixed).astype(o_ref.dtype)


def _gmlp(proj, p):
    m = proj.shape[0]
    cu = COL_GM // D_G
    return pl.pallas_call(
        _gmlp_kernel,
        grid=(m // GM_ROWS,),
        in_specs=[
            pl.BlockSpec((GM_ROWS, D_G), lambda i: (i, cu)),
            pl.BlockSpec((GM_ROWS, D_G), lambda i: (i, cu + 1)),
            pl.BlockSpec((1, D_G), lambda i: (0, 0)),
            pl.BlockSpec((1, D_G), lambda i: (0, 0)),
            pl.BlockSpec((N_GM_GROUPS, CHUNK, CHUNK), lambda i: (0, 0, 0)),
            pl.BlockSpec((CHUNK, N_GM_GROUPS), lambda i: (0, 0)),
        ],
        out_specs=pl.BlockSpec((GM_ROWS, D_G), lambda i: (i, 0)),
        out_shape=jax.ShapeDtypeStruct((m, D_G), BF16),
        compiler_params=_params(("arbitrary",)),
        name="gmlp",
    )(proj, proj, p['gm_ln_g'].reshape(1, -1), p['gm_ln_b'].reshape(1, -1),
      p['gm_ws'].astype(BF16), p['gm_bs'].T)


def _rms(x, g):
    return x * lax.rsqrt(jnp.mean(x * x, axis=-1, keepdims=True) + EPS) * g


def _rope(x, cos, sin_signed, lane):
    partner = jnp.where((lane % 64) < ROT_FREQS,
                        pltpu.roll(x, HEAD_DIM - ROT_FREQS, 1), pltpu.roll(x, ROT_FREQS, 1))
    return x * cos + partner * sin_signed


def _softmax_pv(q_all, k_all, v_all, mask, sink_col):
    s = lax.dot_general(q_all, k_all, (((1,), (1,)), ((), ())), preferred_element_type=F32) * ATT_SCALE
    if mask is not None:
        s = jnp.where(mask, s, NEG)
    m = jnp.maximum(jnp.max(s, axis=-1, keepdims=True), sink_col)
    e = jnp.exp(s - m)
    den = jnp.sum(e, axis=-1, keepdims=True) + jnp.exp(sink_col - m)
    return _dot((e / den).astype(BF16), v_all)


def _sink_column(sink_ref, rows):
    return jnp.concatenate(
        [jnp.broadcast_to(sink_ref[g:g + 1, 0:1], (rows, 1)) for g in range(GQA_GROUP)], axis=0)


def _ctx_att_kernel(q_ref, k_ref, v_ref, qg_ref, kg_ref, sink_ref, o_ref, ko_ref, vo_ref, *, L):
    q = q_ref[...]
    q_all = jnp.concatenate(
        [_rms(q[:, g * HEAD_DIM:(g + 1) * HEAD_DIM], qg_ref[...]) for g in range(GQA_GROUP)], axis=0)
    k = _rms(k_ref[...], kg_ref[...])
    v = v_ref[...]
    ko_ref[...] = k
    vo_ref[...] = v
    out = _softmax_pv(q_all.astype(BF16), k.astype(BF16), v.astype(BF16), None, _sink_column(sink_ref, L))
    for g in range(GQA_GROUP):
        o_ref[:, g * HEAD_DIM:(g + 1) * HEAD_DIM] = out[g * L:(g + 1) * L].astype(o_ref.dtype)


def _sink_rows(sink):
    s = sink.reshape(N_KV_HEADS, GQA_GROUP, 1)
    s = jnp.pad(s, ((0, 0), (0, PAD_ROWS - GQA_GROUP), (0, 0)))
    return jnp.broadcast_to(s, (N_KV_HEADS, PAD_ROWS, V7X_LANES))


def _context_attention(proj, L, p):
    m = proj.shape[0]
    b = m // L
    qw = GQA_GROUP * HEAD_DIM
    kv_shape = jax.ShapeDtypeStruct((b, L, KV_W), F32)
    return pl.pallas_call(
        functools.partial(_ctx_att_kernel, L=L),
        grid=(b, N_KV_HEADS),
        in_specs=[
            pl.BlockSpec((L, qw), lambda i, h: (i, COL_Q // qw + h)),
            pl.BlockSpec((L, HEAD_DIM), lambda i, h: (i, COL_K // HEAD_DIM + h)),
            pl.BlockSpec((L, HEAD_DIM), lambda i, h: (i, COL_V // HEAD_DIM + h)),
            pl.BlockSpec((1, HEAD_DIM), lambda i, h: (0, 0)),
            pl.BlockSpec((1, HEAD_DIM), lambda i, h: (0, 0)),
            pl.BlockSpec((None, PAD_ROWS, V7X_LANES), lambda i, h: (h, 0, 0)),
        ],
        out_specs=[
            pl.BlockSpec((L, qw), lambda i, h: (i, h)),
            pl.BlockSpec((None, L, HEAD_DIM), lambda i, h: (i, 0, h)),
            pl.BlockSpec((None, L, HEAD_DIM), lambda i, h: (i, 0, h)),
        ],
        out_shape=[jax.ShapeDtypeStruct((m, ATT_W), BF16), kv_shape, kv_shape],
        compiler_params=_params(("arbitrary", "arbitrary")),
        name="context_attention",
    )(proj, proj, proj, p['q_norm_g'].reshape(1, -1), p['k_norm_g'].reshape(1, -1), _sink_rows(p['sink']))


def _rope_tables(L):
    rows = L // GRID_W
    row = jnp.repeat(jnp.arange(rows), GRID_W)
    col = jnp.tile(jnp.arange(GRID_W), rows)
    inv = ROPE_BASE ** (-jnp.arange(ROT_FREQS, dtype=F32) / ROT_FREQS)
    pos = jnp.stack([row, col], axis=-1).astype(F32)
    ang = pos[:, :, None] * inv
    cos, sin = jnp.cos(ang), jnp.sin(ang)
    cos_t = jnp.stack([cos, cos], axis=2).reshape(L, HEAD_DIM)
    sin_t = jnp.stack([-sin, sin], axis=2).reshape(L, HEAD_DIM)
    return cos_t, sin_t


def _lat_att_kernel(q_ref, km_ref, k0_ref, kp_ref, vm_ref, v0_ref, vp_ref, kc_ref, vc_ref,
                    cos_ref, sin_ref, qg_ref, kg_ref, sink_ref, o_ref, *, nb):
    n = pl.program_id(2)
    lane = lax.broadcasted_iota(jnp.int32, (1, HEAD_DIM), 1)

    def tables(blk):
        start = pl.multiple_of(blk * BLOCK, BLOCK)
        return cos_ref[pl.ds(start, BLOCK), :], sin_ref[pl.ds(start, BLOCK), :]

    def key(k_ref, blk):
        return _rope(_rms(k_ref[...], kg_ref[...]), *tables(blk), lane)

    c0, s0 = tables(n)
    q = q_ref[...]
    q_all = jnp.concatenate(
        [_rope(_rms(q[:, g * HEAD_DIM:(g + 1) * HEAD_DIM], qg_ref[...]), c0, s0, lane)
         for g in range(GQA_GROUP)], axis=0).astype(BF16)
    k_all = jnp.concatenate(
        [kc_ref[...], key(km_ref, jnp.maximum(n - 1, 0)), key(k0_ref, n), key(kp_ref, jnp.minimum(n + 1, nb - 1))],
        axis=0).astype(BF16)
    v_all = jnp.concatenate([vc_ref[...], vm_ref[...], v0_ref[...], vp_ref[...]], axis=0).astype(BF16)

    n_ctx = kc_ref.shape[0]
    shape = (GQA_GROUP * BLOCK, n_ctx + 3 * BLOCK)
    i = lax.broadcasted_iota(jnp.int32, shape, 0) % BLOCK
    c = lax.broadcasted_iota(jnp.int32, shape, 1)
    far = 1 << 20
    prev_bad = (c >= n_ctx) & (c < n_ctx + BLOCK) & ((c - n_ctx - i) < jnp.where(n > 0, 0, far))
    next_bad = (c >= n_ctx + 2 * BLOCK) & ((c - n_ctx - 2 * BLOCK - i) > jnp.where(n < nb - 1, 0, -far))
    mask = jnp.logical_not(prev_bad | next_bad)

    out = _softmax_pv(q_all, k_all, v_all, mask, _sink_column(sink_ref, BLOCK))
    for g in range(GQA_GROUP):
        o_ref[:, g * HEAD_DIM:(g + 1) * HEAD_DIM] = out[g * BLOCK:(g + 1) * BLOCK].astype(o_ref.dtype)


def _window_attention(proj, L, cache_k, cache_v, layer, p):
    m = proj.shape[0]
    b = m // L
    nb = L // BLOCK
    n_ctx = cache_k.shape[2]
    qw = GQA_GROUP * HEAD_DIM
    cos_t, sin_t = _rope_tables(L)
    prev = lambda n: jnp.maximum(n - 1, 0)
    nxt = lambda n: jnp.minimum(n + 1, nb - 1)
    band = lambda col0, f: pl.BlockSpec(
        (BLOCK, HEAD_DIM), lambda i, h, n: (i * nb + f(n), col0 // HEAD_DIM + h))
    same = lambda n: n
    cache = pl.BlockSpec((None, None, n_ctx, HEAD_DIM), lambda i, h, n: (i, layer, 0, h))
    table = pl.BlockSpec((L, HEAD_DIM), lambda i, h, n: (0, 0))
    gain = pl.BlockSpec((1, HEAD_DIM), lambda i, h, n: (0, 0))
    return pl.pallas_call(
        functools.partial(_lat_att_kernel, nb=nb),
        grid=(b, N_KV_HEADS, nb),
        in_specs=[
            pl.BlockSpec((BLOCK, qw), lambda i, h, n: (i * nb + n, COL_Q // qw + h)),
            band(COL_K, prev), band(COL_K, same), band(COL_K, nxt),
            band(COL_V, prev), band(COL_V, same), band(COL_V, nxt),
            cache, cache, table, table, gain, gain,
            pl.BlockSpec((None, PAD_ROWS, V7X_LANES), lambda i, h, n: (h, 0, 0)),
        ],
        out_specs=pl.BlockSpec((BLOCK, qw), lambda i, h, n: (i * nb + n, h)),
        out_shape=jax.ShapeDtypeStruct((m, ATT_W), BF16),
        compiler_params=_params(("arbitrary", "arbitrary", "arbitrary")),
        name="window_attention",
    )(proj, proj, proj, proj, proj, proj, proj, cache_k, cache_v, cos_t, sin_t,
      p['q_norm_g'].reshape(1, -1), p['k_norm_g'].reshape(1, -1), _sink_rows(p['sink']))


def _trunk_layer(x, L, mods, seg0, rows_per_seg, p, w, tables, cache, layer):
    h = _normmod(x, p['norm1_g'], mods, seg0, rows_per_seg, sh_chunk=0, sc_chunk=1)
    proj = _matmul(h, w['w_in'], _mm_kernel, F32, "in_proj")
    gates = _matmul(h, w['w_gate'], _mm_sigmoid_kernel, F32, "gate_proj", bias=p['b_gate'])
    spectra = _filter_spectra(L, p, tables)
    hy = _hyena(proj, L, spectra, tables, p)
    gm = _gmlp(proj, p)
    if cache is None:
        att, k_new, v_new = _context_attention(proj, L, p)
    else:
        att = _window_attention(proj, L, cache[0], cache[1], layer, p)
        k_new = v_new = None
    merged = _merge(hy, att, gm, w['w_p_hy'], w['w_p_at'], w['w_p_gm'], gates)
    x = _matmul_residual(merged, w['w_out'], x, mods, seg0, rows_per_seg, 2, "out_proj")
    h2 = _normmod(x, p['norm2_g'], mods, seg0, rows_per_seg, sh_chunk=3, sc_chunk=4)
    act = _matmul(h2, w['w_up'], _mm_relu2_kernel, BF16, "mlp_up")
    x = _matmul_residual(act, w['w_down'], x, mods, seg0, rows_per_seg, 5, "mlp_down")
    return x, k_new, v_new


_LAYER_PARAMS = ('norm1_g', 'norm2_g', 'conv_w', 'conv_b', 'flt_w1', 'flt_b1', 'flt_f1', 'flt_w2',
                 'flt_b2', 'flt_f2', 'flt_w3', 'hy_skip', 'q_norm_g', 'k_norm_g', 'sink',
                 'gm_ln_g', 'gm_ln_b', 'gm_ws', 'gm_bs', 'b_gate')
_LAYER_WEIGHTS = ('w_in', 'w_gate', 'w_p_hy', 'w_p_at', 'w_p_gm', 'w_out', 'w_up', 'w_down')


def kernel(x_prompt, x_sample, cache_k, cache_v, c, c_ctx, w_mod, b_mod, norm1_g, norm2_g,
           w_in, conv_w, conv_b, flt_w1, flt_b1, flt_f1, flt_w2, flt_b2, flt_f2, flt_w3,
           hy_skip, q_norm_g, k_norm_g, sink, gm_ln_g, gm_ln_b, gm_ws, gm_bs,
           w_p_hy, w_p_at, w_p_gm, w_gate, b_gate, w_out, w_up, w_down):
    args = dict(locals())
    batch, seq, d = x_prompt.shape
    dec_batch, dec_seq, _ = x_sample.shape
    depth = w_mod.shape[0]
    assert 1 + dec_batch <= PAD_ROWS

    c_rows = jnp.concatenate(
        [c_ctx[None, :], c, jnp.zeros((PAD_ROWS - 1 - dec_batch, d), F32)], axis=0)
    mods = _modulation(c_rows, w_mod, b_mod).reshape(depth, PAD_ROWS, 1, 6 * d)

    tables = {L: _dft_tables(L) for L in (seq, dec_seq)}
    cache = (cache_k.reshape(dec_batch, depth, -1, KV_W), cache_v.reshape(dec_batch, depth, -1, KV_W))

    y_p = x_prompt.reshape(batch * seq, d)
    y_s = x_sample.reshape(dec_batch * dec_seq, d)
    ks, vs = [], []
    for l in range(depth):
        p = {name: args[name][l] for name in _LAYER_PARAMS}
        w = {name: args[name][l].astype(BF16) for name in _LAYER_WEIGHTS}
        y_p, k_l, v_l = _trunk_layer(y_p, seq, mods[l], 0, batch * seq, p, w, tables[seq], None, l)
        ks.append(k_l.reshape(batch, seq, N_KV_HEADS, HEAD_DIM))
        vs.append(v_l.reshape(batch, seq, N_KV_HEADS, HEAD_DIM))
        y_s, _, _ = _trunk_layer(y_s, dec_seq, mods[l], 1, dec_seq, p, w, tables[dec_seq], cache, l)
    return (y_p.reshape(batch, seq, d), y_s.reshape(dec_batch, dec_seq, d),
            jnp.stack(ks, axis=1), jnp.stack(vs, axis=1))
```

```python
import functools
import math

import jax
import jax.numpy as jnp
from jax import lax
from jax.experimental import pallas as pl
from jax.experimental.pallas import tpu as pltpu

F32 = jnp.float32
BF16 = jnp.bfloat16

D_MODEL = 4096
HEAD_DIM = 128
N_Q_HEADS = 16
N_KV_HEADS = 4
GQA_GROUP = 4
ATT_W = N_Q_HEADS * HEAD_DIM
KV_W = N_KV_HEADS * HEAD_DIM
GRID_W = 64
WINDOW = 128
BLOCK = 128
ROPE_BASE = 10000.0
ROT_FREQS = 32
D_H = 1024
HY_ORDER = 2
POS_BANDS = 16
POS_DIM = 1 + 2 * POS_BANDS
FILT_HID = 64
DECAY_MIN = math.log(1e-2) / 1.5
DECAY_MAX = math.log(1e-2) / 0.3
D_G = 1024
CHUNK = 128
N_GM_GROUPS = 8
D_FF = 4 * D_MODEL
IN_COLS = 3 * D_H + ATT_W + 2 * KV_W + 2 * D_G
EPS = 1e-6
NEG = -1e30
ATT_SCALE = HEAD_DIM ** -0.5

COL_HY = 0
COL_Q = 3 * D_H
COL_K = COL_Q + ATT_W
COL_V = COL_K + KV_W
COL_GM = COL_V + KV_W

V7X_LANES = 128
V7X_VMEM_BYTES = 64 * 1024 * 1024
VMEM_LIMIT = 56 * 1024 * 1024

TM = 1024
TN = 1024
TN_KSPLIT = 512
TN_MERGE = 512
TK = 4096
NORM_ROWS = 512
HY_TC = 256
GM_ROWS = 512
MOD_TN = 512
PAD_ROWS = 8


def _params(sem):
    return pltpu.CompilerParams(dimension_semantics=sem, vmem_limit_bytes=VMEM_LIMIT)


def _split(a):
    hi = a.astype(BF16)
    lo = (a - hi.astype(F32)).astype(BF16)
    return hi, lo


def _dot(a, b):
    return jnp.dot(a, b, preferred_element_type=F32)


def _dot3(a_hi, a_lo, b_hi, b_lo):
    return _dot(a_hi, b_hi) + _dot(a_lo, b_hi) + _dot(a_hi, b_lo)


def _mod_kernel(c_ref, w_ref, b_ref, o_ref):
    c = c_ref[...]
    x = c * jax.nn.sigmoid(c)
    x_hi, x_lo = _split(x)
    w_hi, w_lo = _split(w_ref[...])
    o_ref[...] = _dot3(x_hi, x_lo, w_hi, w_lo) + b_ref[...]


def _modulation(c_rows, w_mod, b_mod):
    depth, d, n = w_mod.shape
    return pl.pallas_call(
        _mod_kernel,
        grid=(depth, n // MOD_TN),
        in_specs=[
            pl.BlockSpec((PAD_ROWS, d), lambda l, j: (0, 0)),
            pl.BlockSpec((None, d, MOD_TN), lambda l, j: (l, 0, j)),
            pl.BlockSpec((None, 1, MOD_TN), lambda l, j: (l, 0, j)),
        ],
        out_specs=pl.BlockSpec((None, PAD_ROWS, MOD_TN), lambda l, j: (l, 0, j)),
        out_shape=jax.ShapeDtypeStruct((depth, PAD_ROWS, n), F32),
        compiler_params=_params(("arbitrary", "arbitrary")),
        name="modulation",
    )(c_rows, w_mod, b_mod.reshape(depth, 1, n))


def _normmod_kernel(x_ref, g_ref, sc_ref, sh_ref, o_ref):
    x = x_ref[...]
    y = x * lax.rsqrt(jnp.mean(x * x, axis=-1, keepdims=True) + EPS) * g_ref[...]
    o_ref[...] = (y * (1.0 + sc_ref[...]) + sh_ref[...]).astype(o_ref.dtype)


def _normmod(x, g, mods, seg0, rows_per_seg, sh_chunk, sc_chunk):
    m, d = x.shape
    tiles_per_seg = rows_per_seg // NORM_ROWS
    seg = lambda i: seg0 + i // tiles_per_seg
    return pl.pallas_call(
        _normmod_kernel,
        grid=(m // NORM_ROWS,),
        in_specs=[
            pl.BlockSpec((NORM_ROWS, d), lambda i: (i, 0)),
            pl.BlockSpec((1, d), lambda i: (0, 0)),
            pl.BlockSpec((None, 1, d), lambda i: (seg(i), 0, sc_chunk)),
            pl.BlockSpec((None, 1, d), lambda i: (seg(i), 0, sh_chunk)),
        ],
        out_specs=pl.BlockSpec((NORM_ROWS, d), lambda i: (i, 0)),
        out_shape=jax.ShapeDtypeStruct((m, d), BF16),
        compiler_params=_params(("arbitrary",)),
        name="normmod",
    )(x, g.reshape(1, d), mods, mods)


def _mm_kernel(x_ref, w_ref, o_ref):
    o_ref[...] = _dot(x_ref[...], w_ref[...]).astype(o_ref.dtype)


def _mm_sigmoid_kernel(x_ref, w_ref, b_ref, o_ref):
    o_ref[...] = jax.nn.sigmoid(_dot(x_ref[...], w_ref[...]) + b_ref[...]).astype(o_ref.dtype)


def _mm_relu2_kernel(x_ref, w_ref, o_ref):
    a = jnp.maximum(_dot(x_ref[...], w_ref[...]), 0.0)
    o_ref[...] = (a * a).astype(o_ref.dtype)


def _matmul(x, w, layer, kernel, out_dtype, name, bias=None):
    m, k = x.shape
    n = w.shape[2]
    in_specs = [
        pl.BlockSpec((TM, k), lambda i, j: (i, 0)),
        pl.BlockSpec((None, k, TN), lambda i, j: (layer, 0, j)),
    ]
    args = [x, w]
    if bias is not None:
        in_specs.append(pl.BlockSpec((1, TN), lambda i, j: (0, j)))
        args.append(bias.reshape(1, n))
    return pl.pallas_call(
        kernel,
        grid=(m // TM, n // TN),
        in_specs=in_specs,
        out_specs=pl.BlockSpec((TM, TN), lambda i, j: (i, j)),
        out_shape=jax.ShapeDtypeStruct((m, n), out_dtype),
        compiler_params=_params(("arbitrary", "arbitrary")),
        name=name,
    )(*args)


def _mm_residual_kernel(a_ref, w_ref, x_ref, g_ref, o_ref):
    o_ref[...] = x_ref[...] + g_ref[...] * _dot(a_ref[...], w_ref[...])


def _mm_residual_ksplit_kernel(a_ref, w_ref, x_ref, g_ref, o_ref, acc_ref, *, nk):
    kk = pl.program_id(1)
    j = pl.program_id(2)
    part = _dot(a_ref[...], w_ref[...])

    @pl.when(kk == 0)
    def _():
        acc_ref[j] = part

    @pl.when((kk > 0) & (kk < nk - 1))
    def _():
        acc_ref[j] += part

    @pl.when(kk == nk - 1)
    def _():
        o_ref[...] = x_ref[...] + g_ref[...] * (acc_ref[j] + part)


def _matmul_residual(a, w, layer, x, mods, seg0, rows_per_seg, g_chunk, name):
    m, k = a.shape
    n = w.shape[2]
    tiles_per_seg = rows_per_seg // TM
    seg = lambda i: seg0 + i // tiles_per_seg
    if k <= TK:
        nb = n // TN
        return pl.pallas_call(
            _mm_residual_kernel,
            grid=(m // TM, nb),
            in_specs=[
                pl.BlockSpec((TM, k), lambda i, j: (i, 0)),
                pl.BlockSpec((None, k, TN), lambda i, j: (layer, 0, j)),
                pl.BlockSpec((TM, TN), lambda i, j: (i, j)),
                pl.BlockSpec((None, 1, TN), lambda i, j: (seg(i), 0, g_chunk * nb + j)),
            ],
            out_specs=pl.BlockSpec((TM, TN), lambda i, j: (i, j)),
            out_shape=jax.ShapeDtypeStruct((m, n), F32),
            compiler_params=_params(("arbitrary", "arbitrary")),
            name=name,
        )(a, w, x, mods)
    nk = k // TK
    tn = TN_KSPLIT
    nb = n // tn
    out_col = lambda kk, j: jnp.where(kk == nk - 1, j, 0)
    return pl.pallas_call(
        functools.partial(_mm_residual_ksplit_kernel, nk=nk),
        grid=(m // TM, nk, nb),
        in_specs=[
            pl.BlockSpec((TM, TK), lambda i, kk, j: (i, kk)),
            pl.BlockSpec((None, TK, tn), lambda i, kk, j: (layer, kk, j)),
            pl.BlockSpec((TM, tn), lambda i, kk, j: (i, out_col(kk, j))),
            pl.BlockSpec((None, 1, tn), lambda i, kk, j: (seg(i), 0, g_chunk * nb + out_col(kk, j))),
        ],
        out_specs=pl.BlockSpec((TM, tn), lambda i, kk, j: (i, out_col(kk, j))),
        out_shape=jax.ShapeDtypeStruct((m, n), F32),
        scratch_shapes=[pltpu.VMEM((nb, TM, tn), F32)],
        compiler_params=_params(("arbitrary", "arbitrary", "arbitrary")),
        name=name,
    )(a, w, x, mods)


def _merge_kernel(hy_ref, at_ref, gm_ref, whv_ref, wat_ref, wgm_ref, ga_ref, gb_ref, gc_ref, o_ref):
    merged = (ga_ref[...] * _dot(hy_ref[...], whv_ref[...])
              + gb_ref[...] * _dot(at_ref[...], wat_ref[...])
              + gc_ref[...] * _dot(gm_ref[...], wgm_ref[...]))
    o_ref[...] = merged.astype(o_ref.dtype)


def _merge(hy, at, gm, w_hy, w_at, w_gm, layer, gates):
    m = hy.shape[0]
    n = w_hy.shape[2]
    tn = TN_MERGE
    nb = n // tn
    row = lambda width: pl.BlockSpec((TM, width), lambda i, j: (i, 0))
    col = lambda depth: pl.BlockSpec((None, depth, tn), lambda i, j: (layer, 0, j))
    gate = lambda b: pl.BlockSpec((TM, tn), lambda i, j: (i, b * nb + j))
    return pl.pallas_call(
        _merge_kernel,
        grid=(m // TM, nb),
        in_specs=[row(hy.shape[1]), row(at.shape[1]), row(gm.shape[1]),
                  col(w_hy.shape[1]), col(w_at.shape[1]), col(w_gm.shape[1]),
                  gate(0), gate(1), gate(2)],
        out_specs=pl.BlockSpec((TM, tn), lambda i, j: (i, j)),
        out_shape=jax.ShapeDtypeStruct((m, n), BF16),
        compiler_params=_params(("arbitrary", "arbitrary")),
        name="merge",
    )(hy, at, gm, w_hy, w_at, w_gm, gates, gates, gates)


def _dft_tables(L):
    k = jnp.arange(L, dtype=jnp.int32)
    ks = (k[:, None] * k[None, :]) % (2 * L)
    ang = ks.astype(F32) * (math.pi / L)
    return _split(jnp.cos(ang)) + _split(jnp.sin(ang))


def _filter_features(L):
    t = jnp.linspace(0.0, 1.0, L, dtype=F32)
    w = 2.0 * math.pi * jnp.arange(L, dtype=F32) / L
    bands = jnp.linspace(1e-4, POS_BANDS - 1, POS_BANDS, dtype=F32)
    z = jnp.concatenate([t[:, None], jnp.cos(w[:, None] * bands), -jnp.sin(w[:, None] * bands)], axis=-1)
    return jnp.pad(z, ((0, 0), (0, V7X_LANES - POS_DIM))), t[:, None]


def _filter_kernel(z_ref, t_ref, w1_ref, b1_ref, f1_ref, w2_ref, b2_ref, f2_ref,
                   w3f_ref, w3b_ref, dl_ref, chi_ref, clo_ref, shi_ref, slo_ref, o_ref, *, L):
    z_hi, z_lo = _split(z_ref[...])
    a = jnp.sin(f1_ref[...] * (_dot3(z_hi, z_lo, *_split(w1_ref[...])) + b1_ref[...]))
    a = jnp.sin(f2_ref[...] * (_dot3(*_split(a), *_split(w2_ref[...])) + b2_ref[...]))
    a_hi, a_lo = _split(a)
    decay = jnp.exp(-t_ref[...] * dl_ref[...])
    row = lax.broadcasted_iota(jnp.int32, (L, 1), 0)
    alt = jnp.where(row % 2 == 0, 1.0, -1.0).astype(F32)
    for o in range(HY_ORDER):
        fwd = _dot3(a_hi, a_lo, *_split(w3f_ref[o])) * decay
        bwd = _dot3(a_hi, a_lo, *_split(w3b_ref[o])) * decay
        s = fwd + jnp.where(row == 0, 0.0, bwd)
        d = bwd - fwd
        hr = _dot3(chi_ref[...], clo_ref[...], *_split(s)) * (1.0 / L)
        hi = _dot3(shi_ref[...], slo_ref[...], *_split(d)) * (1.0 / L)
        nyq = jnp.sum(alt * s, axis=0, keepdims=True) * (0.5 / L)
        o_ref[o, 0] = jnp.where(row == 0, 0.5 * hr, hr)
        o_ref[o, 1] = hi
        o_ref[o, 2] = jnp.where(row == 0, nyq, hr)


def _filter_spectra(L, p, tables):
    z, t = _filter_features(L)
    w1 = jnp.pad(p['flt_w1'], ((0, V7X_LANES - POS_DIM), (0, 0)))
    w3 = p['flt_w3'].reshape(FILT_HID, HY_ORDER, 2, D_H)
    w3f = jnp.transpose(w3[:, :, 0], (1, 0, 2))
    w3b = jnp.transpose(w3[:, :, 1], (1, 0, 2))
    deltas = jnp.abs(jnp.linspace(DECAY_MIN, DECAY_MAX, D_H, dtype=F32)).reshape(1, D_H)
    full = lambda shape: pl.BlockSpec(shape, lambda c: (0,) * len(shape))
    return pl.pallas_call(
        functools.partial(_filter_kernel, L=L),
        grid=(D_H // HY_TC,),
        in_specs=[
            full((L, V7X_LANES)), full((L, 1)),
            full((V7X_LANES, FILT_HID)), full((1, FILT_HID)), full((1, FILT_HID)),
            full((FILT_HID, FILT_HID)), full((1, FILT_HID)), full((1, FILT_HID)),
            pl.BlockSpec((HY_ORDER, FILT_HID, HY_TC), lambda c: (0, 0, c)),
            pl.BlockSpec((HY_ORDER, FILT_HID, HY_TC), lambda c: (0, 0, c)),
            pl.BlockSpec((1, HY_TC), lambda c: (0, c)),
            full((L, L)), full((L, L)), full((L, L)), full((L, L)),
        ],
        out_specs=pl.BlockSpec((HY_ORDER, 3, L, HY_TC), lambda c: (0, 0, 0, c)),
        out_shape=jax.ShapeDtypeStruct((HY_ORDER, 3, L, D_H), F32),
        compiler_params=_params(("arbitrary",)),
        name=f"hyena_filter_{L}",
    )(z, t, w1, p['flt_b1'].reshape(1, -1), p['flt_f1'].reshape(1, -1),
      p['flt_w2'], p['flt_b2'].reshape(1, -1), p['flt_f2'].reshape(1, -1),
      w3f, w3b, deltas, *tables)


def _hyena_kernel(v_ref, x1_ref, x2_ref, wv_ref, wx1_ref, wx2_ref, bv_ref, bx1_ref, bx2_ref,
                  tab_ref, skip_ref, chi_ref, clo_ref, shi_ref, slo_ref, o_ref, *, L):
    row = lax.broadcasted_iota(jnp.int32, (L, 1), 0)
    alt = jnp.where(row % 2 == 0, 1.0, -1.0).astype(F32)

    def short_conv(u_ref, w_ref, b_ref):
        u = u_ref[...]
        prev = jnp.where(row == 0, 0.0, pltpu.roll(u, 1, 0))
        nxt = jnp.where(row == L - 1, 0.0, pltpu.roll(u, L - 1, 0))
        return prev * w_ref[0:1, :] + u * w_ref[1:2, :] + nxt * w_ref[2:3, :] + b_ref[...]

    z = short_conv(v_ref, wv_ref, bv_ref)
    gates = (short_conv(x1_ref, wx1_ref, bx1_ref), short_conv(x2_ref, wx2_ref, bx2_ref))
    for o in range(HY_ORDER):
        z_hi, z_lo = _split(z)
        re = _dot3(chi_ref[...], clo_ref[...], z_hi, z_lo)
        im = -_dot3(shi_ref[...], slo_ref[...], z_hi, z_lo)
        nyq = jnp.sum(alt * z, axis=0, keepdims=True)
        im = jnp.where(row == 0, nyq, im)
        p, q, r = tab_ref[o, 0], tab_ref[o, 1], tab_ref[o, 2]
        ya = re * p - im * q
        yb = re * q + im * r
        y = (_dot3(chi_ref[...], clo_ref[...], *_split(ya))
             - _dot3(shi_ref[...], slo_ref[...], *_split(yb))
             + alt * yb[0:1, :])
        z = gates[o] * (y + skip_ref[o:o + 1, :] * z)
    o_ref[...] = z.astype(o_ref.dtype)


def _hyena(proj, L, spectra, tables, p):
    m = proj.shape[0]
    nc = D_H // HY_TC
    u = lambda part: pl.BlockSpec((L, HY_TC), lambda c, b: (b, part * nc + c))
    cw = lambda part: pl.BlockSpec((3, HY_TC), lambda c, b: (0, part * nc + c))
    cb = lambda part: pl.BlockSpec((1, HY_TC), lambda c, b: (0, part * nc + c))
    full = pl.BlockSpec((L, L), lambda c, b: (0, 0))
    return pl.pallas_call(
        functools.partial(_hyena_kernel, L=L),
        grid=(nc, m // L),
        in_specs=[u(0), u(1), u(2), cw(0), cw(1), cw(2), cb(0), cb(1), cb(2),
                  pl.BlockSpec((HY_ORDER, 3, L, HY_TC), lambda c, b: (0, 0, 0, c)),
                  pl.BlockSpec((HY_ORDER, HY_TC), lambda c, b: (0, c)),
                  full, full, full, full],
        out_specs=pl.BlockSpec((L, HY_TC), lambda c, b: (b, c)),
        out_shape=jax.ShapeDtypeStruct((m, D_H), BF16),
        compiler_params=_params(("arbitrary", "arbitrary")),
        name=f"hyena_{L}",
    )(proj, proj, proj, p['conv_w'], p['conv_w'], p['conv_w'],
      p['conv_b'].reshape(1, -1), p['conv_b'].reshape(1, -1), p['conv_b'].reshape(1, -1),
      spectra, p['hy_skip'], *tables)


def _gmlp_kernel(u_ref, v_ref, lg_ref, lb_ref, ws_ref, bs_ref, o_ref):
    v = v_ref[...]
    mu = jnp.mean(v, axis=-1, keepdims=True)
    vc = v - mu
    var = jnp.mean(vc * vc, axis=-1, keepdims=True)
    vn = (vc * lax.rsqrt(var + EPS) * lg_ref[...] + lb_ref[...]).astype(BF16)
    for n in range(GM_ROWS // CHUNK):
        rows = slice(n * CHUNK, (n + 1) * CHUNK)
        for g in range(N_GM_GROUPS):
            cols = slice(g * CHUNK, (g + 1) * CHUNK)
            mixed = _dot(ws_ref[g], vn[rows, cols]) + bs_ref[:, g:g + 1]
            o_ref[rows, cols] = (u_ref[rows, cols] * mixed).astype(o_ref.dtype)


def _gmlp(proj, p):
    m = proj.shape[0]
    cu = COL_GM // D_G
    return pl.pallas_call(
        _gmlp_kernel,
        grid=(m // GM_ROWS,),
        in_specs=[
            pl.BlockSpec((GM_ROWS, D_G), lambda i: (i, cu)),
            pl.BlockSpec((GM_ROWS, D_G), lambda i: (i, cu + 1)),
            pl.BlockSpec((1, D_G), lambda i: (0, 0)),
            pl.BlockSpec((1, D_G), lambda i: (0, 0)),
            pl.BlockSpec((N_GM_GROUPS, CHUNK, CHUNK), lambda i: (0, 0, 0)),
            pl.BlockSpec((CHUNK, N_GM_GROUPS), lambda i: (0, 0)),
        ],
        out_specs=pl.BlockSpec((GM_ROWS, D_G), lambda i: (i, 0)),
        out_shape=jax.ShapeDtypeStruct((m, D_G), BF16),
        compiler_params=_params(("arbitrary",)),
        name="gmlp",
    )(proj, proj, p['gm_ln_g'].reshape(1, -1), p['gm_ln_b'].reshape(1, -1),
      p['gm_ws'].astype(BF16), p['gm_bs'].T)


def _rms(x, g):
    return x * lax.rsqrt(jnp.mean(x * x, axis=-1, keepdims=True) + EPS) * g


def _rope(x, cos, sin_signed, lane):
    partner = jnp.where((lane % 64) < ROT_FREQS,
                        pltpu.roll(x, HEAD_DIM - ROT_FREQS, 1), pltpu.roll(x, ROT_FREQS, 1))
    return x * cos + partner * sin_signed


def _softmax_pv(q_all, k_all, v_all, mask, sink_col):
    s = lax.dot_general(q_all, k_all, (((1,), (1,)), ((), ())), preferred_element_type=F32) * ATT_SCALE
    if mask is not None:
        s = jnp.where(mask, s, NEG)
    m = jnp.maximum(jnp.max(s, axis=-1, keepdims=True), sink_col)
    e = jnp.exp(s - m)
    den = jnp.sum(e, axis=-1, keepdims=True) + jnp.exp(sink_col - m)
    return _dot((e / den).astype(BF16), v_all)


def _sink_column(sink_ref, rows):
    return jnp.concatenate(
        [jnp.broadcast_to(sink_ref[g:g + 1, 0:1], (rows, 1)) for g in range(GQA_GROUP)], axis=0)


def _ctx_att_kernel(q_ref, k_ref, v_ref, qg_ref, kg_ref, sink_ref, o_ref, ko_ref, vo_ref, *, L):
    q = q_ref[...]
    q_all = jnp.concatenate(
        [_rms(q[:, g * HEAD_DIM:(g + 1) * HEAD_DIM], qg_ref[...]) for g in range(GQA_GROUP)], axis=0)
    k = _rms(k_ref[...], kg_ref[...])
    v = v_ref[...]
    ko_ref[...] = k
    vo_ref[...] = v
    out = _softmax_pv(q_all.astype(BF16), k.astype(BF16), v.astype(BF16), None, _sink_column(sink_ref, L))
    for g in range(GQA_GROUP):
        o_ref[:, g * HEAD_DIM:(g + 1) * HEAD_DIM] = out[g * L:(g + 1) * L].astype(o_ref.dtype)


def _sink_rows(sink):
    s = sink.reshape(N_KV_HEADS, GQA_GROUP, 1)
    s = jnp.pad(s, ((0, 0), (0, PAD_ROWS - GQA_GROUP), (0, 0)))
    return jnp.broadcast_to(s, (N_KV_HEADS, PAD_ROWS, V7X_LANES))


def _context_attention(proj, L, p):
    m = proj.shape[0]
    b = m // L
    qw = GQA_GROUP * HEAD_DIM
    kv_shape = jax.ShapeDtypeStruct((b, L, KV_W), F32)
    return pl.pallas_call(
        functools.partial(_ctx_att_kernel, L=L),
        grid=(b, N_KV_HEADS),
        in_specs=[
            pl.BlockSpec((L, qw), lambda i, h: (i, COL_Q // qw + h)),
            pl.BlockSpec((L, HEAD_DIM), lambda i, h: (i, COL_K // HEAD_DIM + h)),
            pl.BlockSpec((L, HEAD_DIM), lambda i, h: (i, COL_V // HEAD_DIM + h)),
            pl.BlockSpec((1, HEAD_DIM), lambda i, h: (0, 0)),
            pl.BlockSpec((1, HEAD_DIM), lambda i, h: (0, 0)),
            pl.BlockSpec((None, PAD_ROWS, V7X_LANES), lambda i, h: (h, 0, 0)),
        ],
        out_specs=[
            pl.BlockSpec((L, qw), lambda i, h: (i, h)),
            pl.BlockSpec((None, L, HEAD_DIM), lambda i, h: (i, 0, h)),
            pl.BlockSpec((None, L, HEAD_DIM), lambda i, h: (i, 0, h)),
        ],
        out_shape=[jax.ShapeDtypeStruct((m, ATT_W), BF16), kv_shape, kv_shape],
        compiler_params=_params(("arbitrary", "arbitrary")),
        name="context_attention",
    )(proj, proj, proj, p['q_norm_g'].reshape(1, -1), p['k_norm_g'].reshape(1, -1), _sink_rows(p['sink']))


def _rope_tables(L):
    rows = L // GRID_W
    row = jnp.repeat(jnp.arange(rows), GRID_W)
    col = jnp.tile(jnp.arange(GRID_W), rows)
    inv = ROPE_BASE ** (-jnp.arange(ROT_FREQS, dtype=F32) / ROT_FREQS)
    pos = jnp.stack([row, col], axis=-1).astype(F32)
    ang = pos[:, :, None] * inv
    cos, sin = jnp.cos(ang), jnp.sin(ang)
    cos_t = jnp.stack([cos, cos], axis=2).reshape(L, HEAD_DIM)
    sin_t = jnp.stack([-sin, sin], axis=2).reshape(L, HEAD_DIM)
    return cos_t, sin_t


def _lat_att_kernel(q_ref, km_ref, k0_ref, kp_ref, vm_ref, v0_ref, vp_ref, kc_ref, vc_ref,
                    cos_ref, sin_ref, qg_ref, kg_ref, sink_ref, o_ref, *, nb):
    n = pl.program_id(2)
    lane = lax.broadcasted_iota(jnp.int32, (1, HEAD_DIM), 1)

    def tables(blk):
        start = pl.multiple_of(blk * BLOCK, BLOCK)
        return cos_ref[pl.ds(start, BLOCK), :], sin_ref[pl.ds(start, BLOCK), :]

    def key(k_ref, blk):
        return _rope(_rms(k_ref[...], kg_ref[...]), *tables(blk), lane)

    c0, s0 = tables(n)
    q = q_ref[...]
    q_all = jnp.concatenate(
        [_rope(_rms(q[:, g * HEAD_DIM:(g + 1) * HEAD_DIM], qg_ref[...]), c0, s0, lane)
         for g in range(GQA_GROUP)], axis=0).astype(BF16)
    k_all = jnp.concatenate(
        [kc_ref[...], key(km_ref, jnp.maximum(n - 1, 0)), key(k0_ref, n), key(kp_ref, jnp.minimum(n + 1, nb - 1))],
        axis=0).astype(BF16)
    v_all = jnp.concatenate([vc_ref[...], vm_ref[...], v0_ref[...], vp_ref[...]], axis=0).astype(BF16)

    n_ctx = kc_ref.shape[0]
    shape = (GQA_GROUP * BLOCK, n_ctx + 3 * BLOCK)
    i = lax.broadcasted_iota(jnp.int32, shape, 0) % BLOCK
    c = lax.broadcasted_iota(jnp.int32, shape, 1)
    far = 1 << 20
    prev_bad = (c >= n_ctx) & (c < n_ctx + BLOCK) & ((c - n_ctx - i) < jnp.where(n > 0, 0, far))
    next_bad = (c >= n_ctx + 2 * BLOCK) & ((c - n_ctx - 2 * BLOCK - i) > jnp.where(n < nb - 1, 0, -far))
    mask = jnp.logical_not(prev_bad | next_bad)

    out = _softmax_pv(q_all, k_all, v_all, mask, _sink_column(sink_ref, BLOCK))
    for g in range(GQA_GROUP):
        o_ref[:, g * HEAD_DIM:(g + 1) * HEAD_DIM] = out[g * BLOCK:(g + 1) * BLOCK].astype(o_ref.dtype)


def _window_attention(proj, L, cache_k, cache_v, layer, p):
    m = proj.shape[0]
    b = m // L
    nb = L // BLOCK
    n_ctx = cache_k.shape[2]
    qw = GQA_GROUP * HEAD_DIM
    cos_t, sin_t = _rope_tables(L)
    prev = lambda n: jnp.maximum(n - 1, 0)
    nxt = lambda n: jnp.minimum(n + 1, nb - 1)
    band = lambda col0, f: pl.BlockSpec(
        (BLOCK, HEAD_DIM), lambda i, h, n: (i * nb + f(n), col0 // HEAD_DIM + h))
    same = lambda n: n
    cache = pl.BlockSpec((None, None, n_ctx, HEAD_DIM), lambda i, h, n: (i, layer, 0, h))
    table = pl.BlockSpec((L, HEAD_DIM), lambda i, h, n: (0, 0))
    gain = pl.BlockSpec((1, HEAD_DIM), lambda i, h, n: (0, 0))
    return pl.pallas_call(
        functools.partial(_lat_att_kernel, nb=nb),
        grid=(b, N_KV_HEADS, nb),
        in_specs=[
            pl.BlockSpec((BLOCK, qw), lambda i, h, n: (i * nb + n, COL_Q // qw + h)),
            band(COL_K, prev), band(COL_K, same), band(COL_K, nxt),
            band(COL_V, prev), band(COL_V, same), band(COL_V, nxt),
            cache, cache, table, table, gain, gain,
            pl.BlockSpec((None, PAD_ROWS, V7X_LANES), lambda i, h, n: (h, 0, 0)),
        ],
        out_specs=pl.BlockSpec((BLOCK, qw), lambda i, h, n: (i * nb + n, h)),
        out_shape=jax.ShapeDtypeStruct((m, ATT_W), BF16),
        compiler_params=_params(("arbitrary", "arbitrary", "arbitrary")),
        name="window_attention",
    )(proj, proj, proj, proj, proj, proj, proj, cache_k, cache_v, cos_t, sin_t,
      p['q_norm_g'].reshape(1, -1), p['k_norm_g'].reshape(1, -1), _sink_rows(p['sink']))


def _trunk_layer(x, L, mods, seg0, rows_per_seg, p, w, tables, cache, layer):
    h = _normmod(x, p['norm1_g'], mods, seg0, rows_per_seg, sh_chunk=0, sc_chunk=1)
    proj = _matmul(h, w['w_in'], layer, _mm_kernel, F32, "in_proj")
    gates = _matmul(h, w['w_gate'], layer, _mm_sigmoid_kernel, BF16, "gate_proj", bias=p['b_gate'])
    spectra = _filter_spectra(L, p, tables)
    hy = _hyena(proj, L, spectra, tables, p)
    gm = _gmlp(proj, p)
    if cache is None:
        att, k_new, v_new = _context_attention(proj, L, p)
    else:
        att = _window_attention(proj, L, cache[0], cache[1], layer, p)
        k_new = v_new = None
    merged = _merge(hy, att, gm, w['w_p_hy'], w['w_p_at'], w['w_p_gm'], layer, gates)
    x = _matmul_residual(merged, w['w_out'], layer, x, mods, seg0, rows_per_seg, 2, "out_proj")
    h2 = _normmod(x, p['norm2_g'], mods, seg0, rows_per_seg, sh_chunk=3, sc_chunk=4)
    act = _matmul(h2, w['w_up'], layer, _mm_relu2_kernel, BF16, "mlp_up")
    x = _matmul_residual(act, w['w_down'], layer, x, mods, seg0, rows_per_seg, 5, "mlp_down")
    return x, k_new, v_new


_LAYER_PARAMS = ('norm1_g', 'norm2_g', 'conv_w', 'conv_b', 'flt_w1', 'flt_b1', 'flt_f1', 'flt_w2',
                 'flt_b2', 'flt_f2', 'flt_w3', 'hy_skip', 'q_norm_g', 'k_norm_g', 'sink',
                 'gm_ln_g', 'gm_ln_b', 'gm_ws', 'gm_bs', 'b_gate')
_LAYER_WEIGHTS = ('w_in', 'w_gate', 'w_p_hy', 'w_p_at', 'w_p_gm', 'w_out', 'w_up', 'w_down')


def kernel(x_prompt, x_sample, cache_k, cache_v, c, c_ctx, w_mod, b_mod, norm1_g, norm2_g,
           w_in, conv_w, conv_b, flt_w1, flt_b1, flt_f1, flt_w2, flt_b2, flt_f2, flt_w3,
           hy_skip, q_norm_g, k_norm_g, sink, gm_ln_g, gm_ln_b, gm_ws, gm_bs,
           w_p_hy, w_p_at, w_p_gm, w_gate, b_gate, w_out, w_up, w_down):
    args = dict(locals())
    batch, seq, d = x_prompt.shape
    dec_batch, dec_seq, _ = x_sample.shape
    depth = w_mod.shape[0]
    assert 1 + dec_batch <= PAD_ROWS

    c_rows = jnp.concatenate(
        [c_ctx[None, :], c, jnp.zeros((PAD_ROWS - 1 - dec_batch, d), F32)], axis=0)
    mods = _modulation(c_rows, w_mod, b_mod).reshape(depth, PAD_ROWS, 1, 6 * d)

    tables = {L: _dft_tables(L) for L in (seq, dec_seq)}
    cache = (cache_k.reshape(dec_batch, depth, -1, KV_W), cache_v.reshape(dec_batch, depth, -1, KV_W))

    y_p = x_prompt.reshape(batch * seq, d)
    y_s = x_sample.reshape(dec_batch * dec_seq, d)
    ks, vs = [], []
    w = {name: args[name].astype(BF16) for name in _LAYER_WEIGHTS}
    for l in range(depth):
        p = {name: args[name][l] for name in _LAYER_PARAMS}
        y_p, k_l, v_l = _trunk_layer(y_p, seq, mods[l], 0, batch * seq, p, w, tables[seq], None, l)
        ks.append(k_l.reshape(batch, seq, N_KV_HEADS, HEAD_DIM))
        vs.append(v_l.reshape(batch, seq, N_KV_HEADS, HEAD_DIM))
        y_s, _, _ = _trunk_layer(y_s, dec_seq, mods[l], 1, dec_seq, p, w, tables[dec_seq], cache, l)
    return (y_p.reshape(batch, seq, d), y_s.reshape(dec_batch, dec_seq, d),
            jnp.stack(ks, axis=1), jnp.stack(vs, axis=1))
```

```python
import functools
import math

import jax
import jax.numpy as jnp
from jax import lax
from jax.experimental import pallas as pl
from jax.experimental.pallas import tpu as pltpu

F32 = jnp.float32
BF16 = jnp.bfloat16

D_MODEL = 4096
HEAD_DIM = 128
N_Q_HEADS = 16
N_KV_HEADS = 4
GQA_GROUP = 4
ATT_W = N_Q_HEADS * HEAD_DIM
KV_W = N_KV_HEADS * HEAD_DIM
GRID_W = 64
WINDOW = 128
BLOCK = 128
ROPE_BASE = 10000.0
ROT_FREQS = 32
D_H = 1024
HY_ORDER = 2
POS_BANDS = 16
POS_DIM = 1 + 2 * POS_BANDS
FILT_HID = 64
DECAY_MIN = math.log(1e-2) / 1.5
DECAY_MAX = math.log(1e-2) / 0.3
D_G = 1024
CHUNK = 128
N_GM_GROUPS = 8
D_FF = 4 * D_MODEL
IN_COLS = 3 * D_H + ATT_W + 2 * KV_W + 2 * D_G
EPS = 1e-6
NEG = -1e30
ATT_SCALE = HEAD_DIM ** -0.5

COL_HY = 0
COL_Q = 3 * D_H
COL_K = COL_Q + ATT_W
COL_V = COL_K + KV_W
COL_GM = COL_V + KV_W

V7X_LANES = 128
V7X_VMEM_BYTES = 64 * 1024 * 1024
VMEM_LIMIT = 56 * 1024 * 1024

TM = 1024
TN = 1024
TN_KSPLIT = 512
TN_MERGE = 512
TK = 4096
NORM_ROWS = 512
HY_TC = 256
HY_TILE_ELEMS = 256 * 1024
GM_ROWS = 512
MOD_TN = 512
PAD_ROWS = 8


def _params(sem):
    return pltpu.CompilerParams(dimension_semantics=sem, vmem_limit_bytes=VMEM_LIMIT)


def _split(a):
    hi = a.astype(BF16)
    lo = (a - hi.astype(F32)).astype(BF16)
    return hi, lo


def _dot(a, b):
    return jnp.dot(a, b, preferred_element_type=F32)


def _dot3(a_hi, a_lo, b_hi, b_lo):
    return _dot(a_hi, b_hi) + _dot(a_lo, b_hi) + _dot(a_hi, b_lo)


def _mod_kernel(c_ref, w_ref, b_ref, o_ref):
    c = c_ref[...]
    x = c * jax.nn.sigmoid(c)
    x_hi, x_lo = _split(x)
    w_hi, w_lo = _split(w_ref[...])
    o_ref[...] = _dot3(x_hi, x_lo, w_hi, w_lo) + b_ref[...]


def _modulation(c_rows, w_mod, b_mod):
    depth, d, n = w_mod.shape
    return pl.pallas_call(
        _mod_kernel,
        grid=(depth, n // MOD_TN),
        in_specs=[
            pl.BlockSpec((PAD_ROWS, d), lambda l, j: (0, 0)),
            pl.BlockSpec((None, d, MOD_TN), lambda l, j: (l, 0, j)),
            pl.BlockSpec((None, 1, MOD_TN), lambda l, j: (l, 0, j)),
        ],
        out_specs=pl.BlockSpec((None, PAD_ROWS, MOD_TN), lambda l, j: (l, 0, j)),
        out_shape=jax.ShapeDtypeStruct((depth, PAD_ROWS, n), F32),
        compiler_params=_params(("arbitrary", "arbitrary")),
        name="modulation",
    )(c_rows, w_mod, b_mod.reshape(depth, 1, n))


def _normmod_kernel(x_ref, g_ref, sc_ref, sh_ref, o_ref):
    x = x_ref[...]
    y = x * lax.rsqrt(jnp.mean(x * x, axis=-1, keepdims=True) + EPS) * g_ref[...]
    o_ref[...] = (y * (1.0 + sc_ref[...]) + sh_ref[...]).astype(o_ref.dtype)


def _normmod(x, g, mods, seg0, rows_per_seg, sh_chunk, sc_chunk):
    m, d = x.shape
    tiles_per_seg = rows_per_seg // NORM_ROWS
    seg = lambda i: seg0 + i // tiles_per_seg
    return pl.pallas_call(
        _normmod_kernel,
        grid=(m // NORM_ROWS,),
        in_specs=[
            pl.BlockSpec((NORM_ROWS, d), lambda i: (i, 0)),
            pl.BlockSpec((1, d), lambda i: (0, 0)),
            pl.BlockSpec((None, 1, d), lambda i: (seg(i), 0, sc_chunk)),
            pl.BlockSpec((None, 1, d), lambda i: (seg(i), 0, sh_chunk)),
        ],
        out_specs=pl.BlockSpec((NORM_ROWS, d), lambda i: (i, 0)),
        out_shape=jax.ShapeDtypeStruct((m, d), BF16),
        compiler_params=_params(("arbitrary",)),
        name="normmod",
    )(x, g.reshape(1, d), mods, mods)


def _mm_kernel(x_ref, w_ref, o_ref):
    o_ref[...] = _dot(x_ref[...], w_ref[...]).astype(o_ref.dtype)


def _mm_sigmoid_kernel(x_ref, w_ref, b_ref, o_ref):
    o_ref[...] = jax.nn.sigmoid(_dot(x_ref[...], w_ref[...]) + b_ref[...]).astype(o_ref.dtype)


def _mm_relu2_kernel(x_ref, w_ref, o_ref):
    a = jnp.maximum(_dot(x_ref[...], w_ref[...]), 0.0)
    o_ref[...] = (a * a).astype(o_ref.dtype)


def _matmul(x, w, layer, kernel, out_dtype, name, bias=None):
    m, k = x.shape
    n = w.shape[2]
    in_specs = [
        pl.BlockSpec((TM, k), lambda i, j: (i, 0)),
        pl.BlockSpec((None, k, TN), lambda i, j: (layer, 0, j)),
    ]
    args = [x, w]
    if bias is not None:
        in_specs.append(pl.BlockSpec((1, TN), lambda i, j: (0, j)))
        args.append(bias.reshape(1, n))
    return pl.pallas_call(
        kernel,
        grid=(m // TM, n // TN),
        in_specs=in_specs,
        out_specs=pl.BlockSpec((TM, TN), lambda i, j: (i, j)),
        out_shape=jax.ShapeDtypeStruct((m, n), out_dtype),
        compiler_params=_params(("arbitrary", "arbitrary")),
        name=name,
    )(*args)


def _mm_residual_kernel(a_ref, w_ref, x_ref, g_ref, o_ref):
    o_ref[...] = x_ref[...] + g_ref[...] * _dot(a_ref[...], w_ref[...])


def _mm_residual_ksplit_kernel(a_ref, w_ref, x_ref, g_ref, o_ref, acc_ref, *, nk):
    kk = pl.program_id(1)
    j = pl.program_id(2)
    part = _dot(a_ref[...], w_ref[...])

    @pl.when(kk == 0)
    def _():
        acc_ref[j] = part

    @pl.when((kk > 0) & (kk < nk - 1))
    def _():
        acc_ref[j] += part

    @pl.when(kk == nk - 1)
    def _():
        o_ref[...] = x_ref[...] + g_ref[...] * (acc_ref[j] + part)


def _matmul_residual(a, w, layer, x, mods, seg0, rows_per_seg, g_chunk, name):
    m, k = a.shape
    n = w.shape[2]
    tiles_per_seg = rows_per_seg // TM
    seg = lambda i: seg0 + i // tiles_per_seg
    if k <= TK:
        nb = n // TN
        return pl.pallas_call(
            _mm_residual_kernel,
            grid=(m // TM, nb),
            in_specs=[
                pl.BlockSpec((TM, k), lambda i, j: (i, 0)),
                pl.BlockSpec((None, k, TN), lambda i, j: (layer, 0, j)),
                pl.BlockSpec((TM, TN), lambda i, j: (i, j)),
                pl.BlockSpec((None, 1, TN), lambda i, j: (seg(i), 0, g_chunk * nb + j)),
            ],
            out_specs=pl.BlockSpec((TM, TN), lambda i, j: (i, j)),
            out_shape=jax.ShapeDtypeStruct((m, n), F32),
            compiler_params=_params(("arbitrary", "arbitrary")),
            name=name,
        )(a, w, x, mods)
    nk = k // TK
    tn = TN_KSPLIT
    nb = n // tn
    out_col = lambda kk, j: jnp.where(kk == nk - 1, j, 0)
    return pl.pallas_call(
        functools.partial(_mm_residual_ksplit_kernel, nk=nk),
        grid=(m // TM, nk, nb),
        in_specs=[
            pl.BlockSpec((TM, TK), lambda i, kk, j: (i, kk)),
            pl.BlockSpec((None, TK, tn), lambda i, kk, j: (layer, kk, j)),
            pl.BlockSpec((TM, tn), lambda i, kk, j: (i, out_col(kk, j))),
            pl.BlockSpec((None, 1, tn), lambda i, kk, j: (seg(i), 0, g_chunk * nb + out_col(kk, j))),
        ],
        out_specs=pl.BlockSpec((TM, tn), lambda i, kk, j: (i, out_col(kk, j))),
        out_shape=jax.ShapeDtypeStruct((m, n), F32),
        scratch_shapes=[pltpu.VMEM((nb, TM, tn), F32)],
        compiler_params=_params(("arbitrary", "arbitrary", "arbitrary")),
        name=name,
    )(a, w, x, mods)


def _merge_kernel(hy_ref, at_ref, gm_ref, whv_ref, wat_ref, wgm_ref, ga_ref, gb_ref, gc_ref, o_ref):
    merged = (ga_ref[...] * _dot(hy_ref[...], whv_ref[...])
              + gb_ref[...] * _dot(at_ref[...], wat_ref[...])
              + gc_ref[...] * _dot(gm_ref[...], wgm_ref[...]))
    o_ref[...] = merged.astype(o_ref.dtype)


def _merge(hy, at, gm, w_hy, w_at, w_gm, layer, gates):
    m = hy.shape[0]
    n = w_hy.shape[2]
    tn = TN_MERGE
    nb = n // tn
    row = lambda width: pl.BlockSpec((TM, width), lambda i, j: (i, 0))
    col = lambda depth: pl.BlockSpec((None, depth, tn), lambda i, j: (layer, 0, j))
    gate = lambda b: pl.BlockSpec((TM, tn), lambda i, j: (i, b * nb + j))
    return pl.pallas_call(
        _merge_kernel,
        grid=(m // TM, nb),
        in_specs=[row(hy.shape[1]), row(at.shape[1]), row(gm.shape[1]),
                  col(w_hy.shape[1]), col(w_at.shape[1]), col(w_gm.shape[1]),
                  gate(0), gate(1), gate(2)],
        out_specs=pl.BlockSpec((TM, tn), lambda i, j: (i, j)),
        out_shape=jax.ShapeDtypeStruct((m, n), BF16),
        compiler_params=_params(("arbitrary", "arbitrary")),
        name="merge",
    )(hy, at, gm, w_hy, w_at, w_gm, gates, gates, gates)


def _dft_tables(L):
    k = jnp.arange(L, dtype=jnp.int32)
    ks = (k[:, None] * k[None, :]) % (2 * L)
    ang = ks.astype(F32) * (math.pi / L)
    cos, sin = jnp.cos(ang), jnp.sin(ang)
    return {
        'split': _split(cos) + _split(sin),
        'fwd': jnp.concatenate([cos, -sin], axis=0).astype(BF16),
        'inv': jnp.concatenate([cos, -sin], axis=1).astype(BF16),
    }


def _filter_features(L):
    t = jnp.linspace(0.0, 1.0, L, dtype=F32)
    w = 2.0 * math.pi * jnp.arange(L, dtype=F32) / L
    bands = jnp.linspace(1e-4, POS_BANDS - 1, POS_BANDS, dtype=F32)
    z = jnp.concatenate([t[:, None], jnp.cos(w[:, None] * bands), -jnp.sin(w[:, None] * bands)], axis=-1)
    return jnp.pad(z, ((0, 0), (0, V7X_LANES - POS_DIM))), t[:, None]


def _filter_kernel(z_ref, t_ref, w1_ref, b1_ref, f1_ref, w2_ref, b2_ref, f2_ref,
                   w3f_ref, w3b_ref, dl_ref, chi_ref, clo_ref, shi_ref, slo_ref, o_ref, *, L):
    z_hi, z_lo = _split(z_ref[...])
    a = jnp.sin(f1_ref[...] * (_dot3(z_hi, z_lo, *_split(w1_ref[...])) + b1_ref[...]))
    a = jnp.sin(f2_ref[...] * (_dot3(*_split(a), *_split(w2_ref[...])) + b2_ref[...]))
    a_hi, a_lo = _split(a)
    decay = jnp.exp(-t_ref[...] * dl_ref[...])
    row = lax.broadcasted_iota(jnp.int32, (L, 1), 0)
    alt = jnp.where(row % 2 == 0, 1.0, -1.0).astype(F32)
    for o in range(HY_ORDER):
        fwd = _dot3(a_hi, a_lo, *_split(w3f_ref[o])) * decay
        bwd = _dot3(a_hi, a_lo, *_split(w3b_ref[o])) * decay
        s = fwd + jnp.where(row == 0, 0.0, bwd)
        d = bwd - fwd
        hr = _dot3(chi_ref[...], clo_ref[...], *_split(s)) * (1.0 / L)
        hi = _dot3(shi_ref[...], slo_ref[...], *_split(d)) * (1.0 / L)
        nyq = jnp.sum(alt * s, axis=0, keepdims=True) * (0.5 / L)
        o_ref[o, 0] = jnp.where(row == 0, 0.5 * hr, hr)
        o_ref[o, 1] = hi
        o_ref[o, 2] = jnp.where(row == 0, nyq, hr)


def _filter_spectra(L, p, tables):
    z, t = _filter_features(L)
    w1 = jnp.pad(p['flt_w1'], ((0, V7X_LANES - POS_DIM), (0, 0)))
    w3 = p['flt_w3'].reshape(FILT_HID, HY_ORDER, 2, D_H)
    w3f = jnp.transpose(w3[:, :, 0], (1, 0, 2))
    w3b = jnp.transpose(w3[:, :, 1], (1, 0, 2))
    deltas = jnp.abs(jnp.linspace(DECAY_MIN, DECAY_MAX, D_H, dtype=F32)).reshape(1, D_H)
    full = lambda shape: pl.BlockSpec(shape, lambda c: (0,) * len(shape))
    return pl.pallas_call(
        functools.partial(_filter_kernel, L=L),
        grid=(D_H // HY_TC,),
        in_specs=[
            full((L, V7X_LANES)), full((L, 1)),
            full((V7X_LANES, FILT_HID)), full((1, FILT_HID)), full((1, FILT_HID)),
            full((FILT_HID, FILT_HID)), full((1, FILT_HID)), full((1, FILT_HID)),
            pl.BlockSpec((HY_ORDER, FILT_HID, HY_TC), lambda c: (0, 0, c)),
            pl.BlockSpec((HY_ORDER, FILT_HID, HY_TC), lambda c: (0, 0, c)),
            pl.BlockSpec((1, HY_TC), lambda c: (0, c)),
            full((L, L)), full((L, L)), full((L, L)), full((L, L)),
        ],
        out_specs=pl.BlockSpec((HY_ORDER, 3, L, HY_TC), lambda c: (0, 0, 0, c)),
        out_shape=jax.ShapeDtypeStruct((HY_ORDER, 3, L, D_H), F32),
        compiler_params=_params(("arbitrary",)),
        name=f"hyena_filter_{L}",
    )(z, t, w1, p['flt_b1'].reshape(1, -1), p['flt_f1'].reshape(1, -1),
      p['flt_w2'], p['flt_b2'].reshape(1, -1), p['flt_f2'].reshape(1, -1),
      w3f, w3b, deltas, *tables['split'])


def _hyena_kernel(v_ref, x1_ref, x2_ref, wv_ref, wx1_ref, wx2_ref, bv_ref, bx1_ref, bx2_ref,
                  tab_ref, skip_ref, fwd_ref, inv_ref, o_ref, *, L):
    row = lax.broadcasted_iota(jnp.int32, (L, 1), 0)
    alt = jnp.where(row % 2 == 0, 1.0, -1.0).astype(F32)

    def short_conv(u_ref, w_ref, b_ref):
        u = u_ref[...]
        prev = jnp.where(row == 0, 0.0, pltpu.roll(u, 1, 0))
        nxt = jnp.where(row == L - 1, 0.0, pltpu.roll(u, L - 1, 0))
        return prev * w_ref[0:1, :] + u * w_ref[1:2, :] + nxt * w_ref[2:3, :] + b_ref[...]

    z = short_conv(v_ref, wv_ref, bv_ref)
    gates = (short_conv(x1_ref, wx1_ref, bx1_ref), short_conv(x2_ref, wx2_ref, bx2_ref))
    for o in range(HY_ORDER):
        spec = _dot(fwd_ref[...], z.astype(BF16))
        re = spec[:L]
        nyq = jnp.sum(alt * z, axis=0, keepdims=True)
        im = jnp.where(row == 0, nyq, spec[L:])
        p, q, r = tab_ref[o, 0], tab_ref[o, 1], tab_ref[o, 2]
        ya = re * p - im * q
        yb = re * q + im * r
        y = _dot(inv_ref[...], jnp.concatenate([ya, yb], axis=0).astype(BF16)) + alt * yb[0:1, :]
        z = gates[o] * (y + skip_ref[o:o + 1, :] * z)
    o_ref[...] = z.astype(o_ref.dtype)


def _hyena(proj, L, spectra, tables, p):
    m = proj.shape[0]
    tc = min(D_H, HY_TILE_ELEMS // L)
    nc = D_H // tc
    u = lambda part: pl.BlockSpec((L, tc), lambda c, b: (b, part * nc + c))
    cw = lambda part: pl.BlockSpec((3, tc), lambda c, b: (0, part * nc + c))
    cb = lambda part: pl.BlockSpec((1, tc), lambda c, b: (0, part * nc + c))
    return pl.pallas_call(
        functools.partial(_hyena_kernel, L=L),
        grid=(nc, m // L),
        in_specs=[u(0), u(1), u(2), cw(0), cw(1), cw(2), cb(0), cb(1), cb(2),
                  pl.BlockSpec((HY_ORDER, 3, L, tc), lambda c, b: (0, 0, 0, c)),
                  pl.BlockSpec((HY_ORDER, tc), lambda c, b: (0, c)),
                  pl.BlockSpec((2 * L, L), lambda c, b: (0, 0)),
                  pl.BlockSpec((L, 2 * L), lambda c, b: (0, 0))],
        out_specs=pl.BlockSpec((L, tc), lambda c, b: (b, c)),
        out_shape=jax.ShapeDtypeStruct((m, D_H), BF16),
        compiler_params=_params(("arbitrary", "arbitrary")),
        name=f"hyena_{L}",
    )(proj, proj, proj, p['conv_w'], p['conv_w'], p['conv_w'],
      p['conv_b'].reshape(1, -1), p['conv_b'].reshape(1, -1), p['conv_b'].reshape(1, -1),
      spectra, p['hy_skip'], tables['fwd'], tables['inv'])


def _gmlp_kernel(u_ref, v_ref, lg_ref, lb_ref, ws_ref, bs_ref, o_ref):
    v = v_ref[...]
    mu = jnp.mean(v, axis=-1, keepdims=True)
    vc = v - mu
    var = jnp.mean(vc * vc, axis=-1, keepdims=True)
    vn = (vc * lax.rsqrt(var + EPS) * lg_ref[...] + lb_ref[...]).astype(BF16)
    for n in range(GM_ROWS // CHUNK):
        rows = slice(n * CHUNK, (n + 1) * CHUNK)
        for g in range(N_GM_GROUPS):
            cols = slice(g * CHUNK, (g + 1) * CHUNK)
            mixed = _dot(ws_ref[g], vn[rows, cols]) + bs_ref[:, g:g + 1]
            o_ref[rows, cols] = (u_ref[rows, cols] * mixed).astype(o_ref.dtype)


def _gmlp(proj, p):
    m = proj.shape[0]
    cu = COL_GM // D_G
    return pl.pallas_call(
        _gmlp_kernel,
        grid=(m // GM_ROWS,),
        in_specs=[
            pl.BlockSpec((GM_ROWS, D_G), lambda i: (i, cu)),
            pl.BlockSpec((GM_ROWS, D_G), lambda i: (i, cu + 1)),
            pl.BlockSpec((1, D_G), lambda i: (0, 0)),
            pl.BlockSpec((1, D_G), lambda i: (0, 0)),
            pl.BlockSpec((N_GM_GROUPS, CHUNK, CHUNK), lambda i: (0, 0, 0)),
            pl.BlockSpec((CHUNK, N_GM_GROUPS), lambda i: (0, 0)),
        ],
        out_specs=pl.BlockSpec((GM_ROWS, D_G), lambda i: (i, 0)),
        out_shape=jax.ShapeDtypeStruct((m, D_G), BF16),
        compiler_params=_params(("arbitrary",)),
        name="gmlp",
    )(proj, proj, p['gm_ln_g'].reshape(1, -1), p['gm_ln_b'].reshape(1, -1),
      p['gm_ws'].astype(BF16), p['gm_bs'].T)


def _rms(x, g):
    return x * lax.rsqrt(jnp.mean(x * x, axis=-1, keepdims=True) + EPS) * g


def _rope(x, cos, sin_signed, lane):
    partner = jnp.where((lane % 64) < ROT_FREQS,
                        pltpu.roll(x, HEAD_DIM - ROT_FREQS, 1), pltpu.roll(x, ROT_FREQS, 1))
    return x * cos + partner * sin_signed


def _softmax_pv(q_all, k_all, v_all, mask, sink_col):
    s = lax.dot_general(q_all, k_all, (((1,), (1,)), ((), ())), preferred_element_type=F32) * ATT_SCALE
    if mask is not None:
        s = jnp.where(mask, s, NEG)
    m = jnp.maximum(jnp.max(s, axis=-1, keepdims=True), sink_col)
    e = jnp.exp(s - m)
    den = jnp.sum(e, axis=-1, keepdims=True) + jnp.exp(sink_col - m)
    return _dot((e / den).astype(BF16), v_all)


def _sink_column(sink_ref, rows):
    return jnp.concatenate(
        [jnp.broadcast_to(sink_ref[g:g + 1, 0:1], (rows, 1)) for g in range(GQA_GROUP)], axis=0)


def _ctx_att_kernel(qa_ref, qb_ref, k_ref, v_ref, qg_ref, kg_ref, sink_ref, o_ref, ko_ref, vo_ref, *, L):
    qw = GQA_GROUP * HEAD_DIM
    q_halves = (qa_ref, qb_ref)
    for h in range(N_KV_HEADS):
        q_ref = q_halves[h // 2]
        q0 = (h % 2) * qw
        q_all = jnp.concatenate(
            [_rms(q_ref[:, q0 + g * HEAD_DIM:q0 + (g + 1) * HEAD_DIM], qg_ref[...]) for g in range(GQA_GROUP)],
            axis=0)
        cols = slice(h * HEAD_DIM, (h + 1) * HEAD_DIM)
        k = _rms(k_ref[:, cols], kg_ref[...])
        v = v_ref[:, cols]
        ko_ref[:, cols] = k
        vo_ref[:, cols] = v
        out = _softmax_pv(q_all.astype(BF16), k.astype(BF16), v.astype(BF16), None,
                          _sink_column(sink_ref.at[h], L))
        for g in range(GQA_GROUP):
            o_ref[:, h * qw + g * HEAD_DIM:h * qw + (g + 1) * HEAD_DIM] = out[g * L:(g + 1) * L].astype(o_ref.dtype)


def _sink_rows(sink):
    s = sink.reshape(N_KV_HEADS, GQA_GROUP, 1)
    s = jnp.pad(s, ((0, 0), (0, PAD_ROWS - GQA_GROUP), (0, 0)))
    return jnp.broadcast_to(s, (N_KV_HEADS, PAD_ROWS, V7X_LANES))


def _context_attention(proj, L, p):
    m = proj.shape[0]
    b = m // L
    half = ATT_W // 2
    kv_shape = jax.ShapeDtypeStruct((b, L, KV_W), F32)
    return pl.pallas_call(
        functools.partial(_ctx_att_kernel, L=L),
        grid=(b,),
        in_specs=[
            pl.BlockSpec((L, half), lambda i: (i, COL_Q // half)),
            pl.BlockSpec((L, half), lambda i: (i, COL_Q // half + 1)),
            pl.BlockSpec((L, KV_W), lambda i: (i, COL_K // KV_W)),
            pl.BlockSpec((L, KV_W), lambda i: (i, COL_V // KV_W)),
            pl.BlockSpec((1, HEAD_DIM), lambda i: (0, 0)),
            pl.BlockSpec((1, HEAD_DIM), lambda i: (0, 0)),
            pl.BlockSpec((N_KV_HEADS, PAD_ROWS, V7X_LANES), lambda i: (0, 0, 0)),
        ],
        out_specs=[
            pl.BlockSpec((L, ATT_W), lambda i: (i, 0)),
            pl.BlockSpec((None, L, KV_W), lambda i: (i, 0, 0)),
            pl.BlockSpec((None, L, KV_W), lambda i: (i, 0, 0)),
        ],
        out_shape=[jax.ShapeDtypeStruct((m, ATT_W), BF16), kv_shape, kv_shape],
        compiler_params=_params(("arbitrary",)),
        name="context_attention",
    )(proj, proj, proj, proj, p['q_norm_g'].reshape(1, -1), p['k_norm_g'].reshape(1, -1),
      _sink_rows(p['sink']))


def _rope_tables(L):
    rows = L // GRID_W
    row = jnp.repeat(jnp.arange(rows), GRID_W)
    col = jnp.tile(jnp.arange(GRID_W), rows)
    inv = ROPE_BASE ** (-jnp.arange(ROT_FREQS, dtype=F32) / ROT_FREQS)
    pos = jnp.stack([row, col], axis=-1).astype(F32)
    ang = pos[:, :, None] * inv
    cos, sin = jnp.cos(ang), jnp.sin(ang)
    cos_t = jnp.stack([cos, cos], axis=2).reshape(L, HEAD_DIM)
    sin_t = jnp.stack([-sin, sin], axis=2).reshape(L, HEAD_DIM)
    return cos_t, sin_t


def _lat_att_kernel(q_ref, k_ref, v_ref, kc_ref, vc_ref, cos_ref, sin_ref, qg_ref, kg_ref, sink_ref,
                    o_ref, *, nb):
    lane = lax.broadcasted_iota(jnp.int32, (1, HEAD_DIM), 1)
    k_seq = _rope(_rms(k_ref[...], kg_ref[...]), cos_ref[...], sin_ref[...], lane).astype(BF16)
    v_seq = v_ref[...].astype(BF16)
    k_ctx = kc_ref[...].astype(BF16)
    v_ctx = vc_ref[...].astype(BF16)
    n_ctx = k_ctx.shape[0]
    sink_col = _sink_column(sink_ref, BLOCK)

    def band_mask(has_prev, has_next):
        n_keys = n_ctx + (1 + has_prev + has_next) * BLOCK
        shape = (GQA_GROUP * BLOCK, n_keys)
        i = lax.broadcasted_iota(jnp.int32, shape, 0) % BLOCK
        c = lax.broadcasted_iota(jnp.int32, shape, 1)
        ok = None
        if has_prev:
            ok = (c < n_ctx) | (c >= n_ctx + BLOCK) | (c - n_ctx >= i)
        if has_next:
            start = n_keys - BLOCK
            nxt = (c < start) | (c - start <= i)
            ok = nxt if ok is None else ok & nxt
        return ok

    masks = {}
    for n in range(nb):
        lo, hi = max(n - 1, 0), min(n + 1, nb - 1)
        shape_key = (lo < n, hi > n)
        if shape_key not in masks:
            masks[shape_key] = band_mask(*shape_key)
        rows = slice(n * BLOCK, (n + 1) * BLOCK)
        cos, sin = cos_ref[rows, :], sin_ref[rows, :]
        q_all = jnp.concatenate(
            [_rope(_rms(q_ref[rows, g * HEAD_DIM:(g + 1) * HEAD_DIM], qg_ref[...]), cos, sin, lane)
             for g in range(GQA_GROUP)], axis=0).astype(BF16)
        band = slice(lo * BLOCK, (hi + 1) * BLOCK)
        k_all = jnp.concatenate([k_ctx, k_seq[band]], axis=0)
        v_all = jnp.concatenate([v_ctx, v_seq[band]], axis=0)
        out = _softmax_pv(q_all, k_all, v_all, masks[shape_key], sink_col)
        for g in range(GQA_GROUP):
            o_ref[rows, g * HEAD_DIM:(g + 1) * HEAD_DIM] = out[g * BLOCK:(g + 1) * BLOCK].astype(o_ref.dtype)


def _window_attention(proj, L, cache_k, cache_v, layer, p):
    m = proj.shape[0]
    b = m // L
    nb = L // BLOCK
    n_ctx = cache_k.shape[2]
    qw = GQA_GROUP * HEAD_DIM
    cos_t, sin_t = _rope_tables(L)
    seq = lambda col0: pl.BlockSpec((L, HEAD_DIM), lambda i, h: (i, col0 // HEAD_DIM + h))
    cache = pl.BlockSpec((None, None, n_ctx, HEAD_DIM), lambda i, h: (i, layer, 0, h))
    table = pl.BlockSpec((L, HEAD_DIM), lambda i, h: (0, 0))
    gain = pl.BlockSpec((1, HEAD_DIM), lambda i, h: (0, 0))
    return pl.pallas_call(
        functools.partial(_lat_att_kernel, nb=nb),
        grid=(b, N_KV_HEADS),
        in_specs=[
            pl.BlockSpec((L, qw), lambda i, h: (i, COL_Q // qw + h)),
            seq(COL_K), seq(COL_V), cache, cache, table, table, gain, gain,
            pl.BlockSpec((None, PAD_ROWS, V7X_LANES), lambda i, h: (h, 0, 0)),
        ],
        out_specs=pl.BlockSpec((L, qw), lambda i, h: (i, h)),
        out_shape=jax.ShapeDtypeStruct((m, ATT_W), BF16),
        compiler_params=_params(("arbitrary", "arbitrary")),
        name="window_attention",
    )(proj, proj, proj, cache_k, cache_v, cos_t, sin_t,
      p['q_norm_g'].reshape(1, -1), p['k_norm_g'].reshape(1, -1), _sink_rows(p['sink']))


def _trunk_layer(x, L, mods, seg0, rows_per_seg, p, w, tables, cache, layer):
    h = _normmod(x, p['norm1_g'], mods, seg0, rows_per_seg, sh_chunk=0, sc_chunk=1)
    proj = _matmul(h, w['w_in'], layer, _mm_kernel, F32, "in_proj")
    gates = _matmul(h, w['w_gate'], layer, _mm_sigmoid_kernel, BF16, "gate_proj", bias=p['b_gate'])
    spectra = _filter_spectra(L, p, tables)
    hy = _hyena(proj, L, spectra, tables, p)
    gm = _gmlp(proj, p)
    if cache is None:
        att, k_new, v_new = _context_attention(proj, L, p)
    else:
        att = _window_attention(proj, L, cache[0], cache[1], layer, p)
        k_new = v_new = None
    merged = _merge(hy, att, gm, w['w_p_hy'], w['w_p_at'], w['w_p_gm'], layer, gates)
    x = _matmul_residual(merged, w['w_out'], layer, x, mods, seg0, rows_per_seg, 2, "out_proj")
    h2 = _normmod(x, p['norm2_g'], mods, seg0, rows_per_seg, sh_chunk=3, sc_chunk=4)
    act = _matmul(h2, w['w_up'], layer, _mm_relu2_kernel, BF16, "mlp_up")
    x = _matmul_residual(act, w['w_down'], layer, x, mods, seg0, rows_per_seg, 5, "mlp_down")
    return x, k_new, v_new


_LAYER_PARAMS = ('norm1_g', 'norm2_g', 'conv_w', 'conv_b', 'flt_w1', 'flt_b1', 'flt_f1', 'flt_w2',
                 'flt_b2', 'flt_f2', 'flt_w3', 'hy_skip', 'q_norm_g', 'k_norm_g', 'sink',
                 'gm_ln_g', 'gm_ln_b', 'gm_ws', 'gm_bs', 'b_gate')
_LAYER_WEIGHTS = ('w_in', 'w_gate', 'w_p_hy', 'w_p_at', 'w_p_gm', 'w_out', 'w_up', 'w_down')


def kernel(x_prompt, x_sample, cache_k, cache_v, c, c_ctx, w_mod, b_mod, norm1_g, norm2_g,
           w_in, conv_w, conv_b, flt_w1, flt_b1, flt_f1, flt_w2, flt_b2, flt_f2, flt_w3,
           hy_skip, q_norm_g, k_norm_g, sink, gm_ln_g, gm_ln_b, gm_ws, gm_bs,
           w_p_hy, w_p_at, w_p_gm, w_gate, b_gate, w_out, w_up, w_down):
    args = dict(locals())
    batch, seq, d = x_prompt.shape
    dec_batch, dec_seq, _ = x_sample.shape
    depth = w_mod.shape[0]
    assert 1 + dec_batch <= PAD_ROWS

    c_rows = jnp.concatenate(
        [c_ctx[None, :], c, jnp.zeros((PAD_ROWS - 1 - dec_batch, d), F32)], axis=0)
    mods = _modulation(c_rows, w_mod, b_mod).reshape(depth, PAD_ROWS, 1, 6 * d)

    tables = {L: _dft_tables(L) for L in (seq, dec_seq)}
    cache = (cache_k.reshape(dec_batch, depth, -1, KV_W), cache_v.reshape(dec_batch, depth, -1, KV_W))

    y_p = x_prompt.reshape(batch * seq, d)
    y_s = x_sample.reshape(dec_batch * dec_seq, d)
    ks, vs = [], []
    w = {name: args[name].astype(BF16) for name in _LAYER_WEIGHTS}
    for l in range(depth):
        p = {name: args[name][l] for name in _LAYER_PARAMS}
        y_p, k_l, v_l = _trunk_layer(y_p, seq, mods[l], 0, batch * seq, p, w, tables[seq], None, l)
        ks.append(k_l.reshape(batch, seq, N_KV_HEADS, HEAD_DIM))
        vs.append(v_l.reshape(batch, seq, N_KV_HEADS, HEAD_DIM))
        y_s, _, _ = _trunk_layer(y_s, dec_seq, mods[l], 1, dec_seq, p, w, tables[dec_seq], cache, l)
    return (y_p.reshape(batch, seq, d), y_s.reshape(dec_batch, dec_seq, d),
            jnp.stack(ks, axis=1), jnp.stack(vs, axis=1))
```

```python
import functools
import math

import jax
import jax.numpy as jnp
from jax import lax
from jax.experimental import pallas as pl
from jax.experimental.pallas import tpu as pltpu

F32 = jnp.float32
BF16 = jnp.bfloat16

D_MODEL = 4096
HEAD_DIM = 128
N_Q_HEADS = 16
N_KV_HEADS = 4
GQA_GROUP = 4
ATT_W = N_Q_HEADS * HEAD_DIM
KV_W = N_KV_HEADS * HEAD_DIM
GRID_W = 64
WINDOW = 128
BLOCK = 128
ROPE_BASE = 10000.0
ROT_FREQS = 32
D_H = 1024
HY_ORDER = 2
POS_BANDS = 16
POS_DIM = 1 + 2 * POS_BANDS
FILT_HID = 64
DECAY_MIN = math.log(1e-2) / 1.5
DECAY_MAX = math.log(1e-2) / 0.3
D_G = 1024
CHUNK = 128
N_GM_GROUPS = 8
D_FF = 4 * D_MODEL
IN_COLS = 3 * D_H + ATT_W + 2 * KV_W + 2 * D_G
EPS = 1e-6
NEG = -1e30
ATT_SCALE = HEAD_DIM ** -0.5

COL_HY = 0
COL_Q = 3 * D_H
COL_K = COL_Q + ATT_W
COL_V = COL_K + KV_W
COL_GM = COL_V + KV_W

V7X_LANES = 128
V7X_VMEM_BYTES = 64 * 1024 * 1024
VMEM_LIMIT = 60 * 1024 * 1024

TM = 1024
TN = 1024
TN_KSPLIT = 512
TN_MERGE = 512
TK = 4096
NORM_ROWS = 512
HY_TC = 256
HY_TILE_ELEMS = 256 * 1024
GM_ROWS = 512
MOD_TN = 512
PAD_ROWS = 8
BF16_SUBLANES = 16


def _params(sem):
    return pltpu.CompilerParams(dimension_semantics=sem, vmem_limit_bytes=VMEM_LIMIT)


def _split(a):
    hi = a.astype(BF16)
    lo = (a - hi.astype(F32)).astype(BF16)
    return hi, lo


def _dot(a, b):
    return jnp.dot(a, b, preferred_element_type=F32)


def _dot3(a_hi, a_lo, b_hi, b_lo):
    return _dot(a_hi, b_hi) + _dot(a_lo, b_hi) + _dot(a_hi, b_lo)


def _mod_kernel(c_ref, w_ref, b_ref, o_ref):
    c = c_ref[...]
    x = c * jax.nn.sigmoid(c)
    x_hi, x_lo = _split(x)
    w_hi, w_lo = _split(w_ref[...])
    o_ref[...] = _dot3(x_hi, x_lo, w_hi, w_lo) + b_ref[...]


def _modulation(c_rows, w_mod, b_mod):
    depth, d, n = w_mod.shape
    return pl.pallas_call(
        _mod_kernel,
        grid=(depth, n // MOD_TN),
        in_specs=[
            pl.BlockSpec((PAD_ROWS, d), lambda l, j: (0, 0)),
            pl.BlockSpec((None, d, MOD_TN), lambda l, j: (l, 0, j)),
            pl.BlockSpec((None, 1, MOD_TN), lambda l, j: (l, 0, j)),
        ],
        out_specs=pl.BlockSpec((None, PAD_ROWS, MOD_TN), lambda l, j: (l, 0, j)),
        out_shape=jax.ShapeDtypeStruct((depth, PAD_ROWS, n), F32),
        compiler_params=_params(("arbitrary", "arbitrary")),
        name="modulation",
    )(c_rows, w_mod, b_mod.reshape(depth, 1, n))


def _normmod_kernel(x_ref, g_ref, sc_ref, sh_ref, o_ref):
    x = x_ref[...]
    y = x * lax.rsqrt(jnp.mean(x * x, axis=-1, keepdims=True) + EPS) * g_ref[...]
    o_ref[...] = (y * (1.0 + sc_ref[...]) + sh_ref[...]).astype(o_ref.dtype)


def _normmod(x, g, mods, seg0, rows_per_seg, sh_chunk, sc_chunk):
    m, d = x.shape
    tiles_per_seg = rows_per_seg // NORM_ROWS
    seg = lambda i: seg0 + i // tiles_per_seg
    return pl.pallas_call(
        _normmod_kernel,
        grid=(m // NORM_ROWS,),
        in_specs=[
            pl.BlockSpec((NORM_ROWS, d), lambda i: (i, 0)),
            pl.BlockSpec((1, d), lambda i: (0, 0)),
            pl.BlockSpec((None, 1, d), lambda i: (seg(i), 0, sc_chunk)),
            pl.BlockSpec((None, 1, d), lambda i: (seg(i), 0, sh_chunk)),
        ],
        out_specs=pl.BlockSpec((NORM_ROWS, d), lambda i: (i, 0)),
        out_shape=jax.ShapeDtypeStruct((m, d), BF16),
        compiler_params=_params(("arbitrary",)),
        name="normmod",
    )(x, g.reshape(1, d), mods, mods)


def _mm_kernel(x_ref, w_ref, o_ref):
    o_ref[...] = _dot(x_ref[...], w_ref[...]).astype(o_ref.dtype)


def _mm_sigmoid_kernel(x_ref, w_ref, b_ref, o_ref):
    o_ref[...] = jax.nn.sigmoid(_dot(x_ref[...], w_ref[...]) + b_ref[...]).astype(o_ref.dtype)


def _mm_relu2_kernel(x_ref, w_ref, o_ref):
    a = jnp.maximum(_dot(x_ref[...], w_ref[...]), 0.0)
    o_ref[...] = (a * a).astype(o_ref.dtype)


def _call(body, grid, in_specs, out_specs, out_shape, args, name, scratch_shapes=(), cast=None):
    sem = ("arbitrary",) * len(grid)
    if cast is None:
        out = pl.pallas_call(body, grid=grid, in_specs=in_specs, out_specs=out_specs, out_shape=out_shape,
                             scratch_shapes=scratch_shapes, compiler_params=_params(sem), name=name)(*args)
        return tuple(out)
    src, layer = cast
    _, k2, n2 = src.shape
    steps = math.prod(grid)
    slabs = min(1 << (steps.bit_length() - 1), k2 // BF16_SUBLANES)
    rows = k2 // slabs

    def slab(*ids):
        step = ids[0]
        for extent, idx in zip(grid[1:], ids[1:]):
            step = step * extent + idx
        return jnp.minimum(step, slabs - 1)

    n_in, n_out = len(in_specs), len(out_specs)

    def kernel(*refs):
        src_ref, dst_ref = refs[n_in], refs[n_in + 1 + n_out]
        body(*refs[:n_in], *refs[n_in + 1:n_in + 1 + n_out], *refs[n_in + 2 + n_out:])
        dst_ref[...] = src_ref[...].astype(BF16)

    out = pl.pallas_call(
        kernel, grid=grid,
        in_specs=[*in_specs, pl.BlockSpec((None, rows, n2), lambda *ids: (layer, slab(*ids), 0))],
        out_specs=[*out_specs, pl.BlockSpec((None, rows, n2), lambda *ids: (0, slab(*ids), 0))],
        out_shape=[*out_shape, jax.ShapeDtypeStruct((1, k2, n2), BF16)],
        scratch_shapes=scratch_shapes, compiler_params=_params(sem), name=name)(*args, src)
    return tuple(out)


def _matmul(x, w, kernel, out_dtype, name, bias=None, cast=None):
    w_arr, layer = w
    m, k = x.shape
    n = w_arr.shape[2]
    in_specs = [
        pl.BlockSpec((TM, k), lambda i, j: (i, 0)),
        pl.BlockSpec((None, k, TN), lambda i, j: (layer, 0, j)),
    ]
    args = [x, w_arr]
    if bias is not None:
        in_specs.append(pl.BlockSpec((1, TN), lambda i, j: (0, j)))
        args.append(bias.reshape(1, n))
    return _call(kernel, (m // TM, n // TN), in_specs,
                 [pl.BlockSpec((TM, TN), lambda i, j: (i, j))],
                 [jax.ShapeDtypeStruct((m, n), out_dtype)], args, name, cast=cast)


def _mm_residual_kernel(a_ref, w_ref, x_ref, g_ref, o_ref):
    o_ref[...] = x_ref[...] + g_ref[...] * _dot(a_ref[...], w_ref[...])


def _mm_residual_ksplit_kernel(a_ref, w_ref, x_ref, g_ref, o_ref, acc_ref, *, nk):
    kk = pl.program_id(1)
    j = pl.program_id(2)
    part = _dot(a_ref[...], w_ref[...])

    @pl.when(kk == 0)
    def _():
        acc_ref[j] = part

    @pl.when((kk > 0) & (kk < nk - 1))
    def _():
        acc_ref[j] += part

    @pl.when(kk == nk - 1)
    def _():
        o_ref[...] = x_ref[...] + g_ref[...] * (acc_ref[j] + part)


def _matmul_residual(a, w, x, mods, seg0, rows_per_seg, g_chunk, name, cast=None):
    w_arr, layer = w
    m, k = a.shape
    n = w_arr.shape[2]
    tiles_per_seg = rows_per_seg // TM
    seg = lambda i: seg0 + i // tiles_per_seg
    if k <= TK:
        nb = n // TN
        return _call(
            _mm_residual_kernel, (m // TM, nb),
            [
                pl.BlockSpec((TM, k), lambda i, j: (i, 0)),
                pl.BlockSpec((None, k, TN), lambda i, j: (layer, 0, j)),
                pl.BlockSpec((TM, TN), lambda i, j: (i, j)),
                pl.BlockSpec((None, 1, TN), lambda i, j: (seg(i), 0, g_chunk * nb + j)),
            ],
            [pl.BlockSpec((TM, TN), lambda i, j: (i, j))],
            [jax.ShapeDtypeStruct((m, n), F32)], (a, w_arr, x, mods), name, cast=cast)
    nk = k // TK
    tn = TN_KSPLIT
    nb = n // tn
    out_col = lambda kk, j: jnp.where(kk == nk - 1, j, 0)
    return _call(
        functools.partial(_mm_residual_ksplit_kernel, nk=nk), (m // TM, nk, nb),
        [
            pl.BlockSpec((TM, TK), lambda i, kk, j: (i, kk)),
            pl.BlockSpec((None, TK, tn), lambda i, kk, j: (layer, kk, j)),
            pl.BlockSpec((TM, tn), lambda i, kk, j: (i, out_col(kk, j))),
            pl.BlockSpec((None, 1, tn), lambda i, kk, j: (seg(i), 0, g_chunk * nb + out_col(kk, j))),
        ],
        [pl.BlockSpec((TM, tn), lambda i, kk, j: (i, out_col(kk, j)))],
        [jax.ShapeDtypeStruct((m, n), F32)], (a, w_arr, x, mods), name,
        scratch_shapes=[pltpu.VMEM((nb, TM, tn), F32)], cast=cast)


def _merge_kernel(hy_ref, at_ref, gm_ref, whv_ref, wat_ref, wgm_ref, ga_ref, gb_ref, gc_ref, o_ref):
    merged = (ga_ref[...] * _dot(hy_ref[...], whv_ref[...])
              + gb_ref[...] * _dot(at_ref[...], wat_ref[...])
              + gc_ref[...] * _dot(gm_ref[...], wgm_ref[...]))
    o_ref[...] = merged.astype(o_ref.dtype)


def _merge(hy, at, gm, w_hy, w_at, w_gm, gates):
    m = hy.shape[0]
    n = w_hy[0].shape[2]
    tn = TN_MERGE
    nb = n // tn
    row = lambda width: pl.BlockSpec((TM, width), lambda i, j: (i, 0))
    col = lambda w: pl.BlockSpec((None, w[0].shape[1], tn), lambda i, j: (w[1], 0, j))
    gate = lambda b: pl.BlockSpec((TM, tn), lambda i, j: (i, b * nb + j))
    return _call(
        _merge_kernel, (m // TM, nb),
        [row(hy.shape[1]), row(at.shape[1]), row(gm.shape[1]), col(w_hy), col(w_at), col(w_gm),
         gate(0), gate(1), gate(2)],
        [pl.BlockSpec((TM, tn), lambda i, j: (i, j))],
        [jax.ShapeDtypeStruct((m, n), BF16)],
        (hy, at, gm, w_hy[0], w_at[0], w_gm[0], gates, gates, gates), "merge")[0]


def _dft_tables(L):
    k = jnp.arange(L, dtype=jnp.int32)
    ks = (k[:, None] * k[None, :]) % (2 * L)
    ang = ks.astype(F32) * (math.pi / L)
    cos, sin = jnp.cos(ang), jnp.sin(ang)
    return {
        'split': _split(cos) + _split(sin),
        'fwd': jnp.concatenate([cos, -sin], axis=0).astype(BF16),
        'inv': jnp.concatenate([cos, -sin], axis=1).astype(BF16),
    }


def _filter_features(L):
    t = jnp.linspace(0.0, 1.0, L, dtype=F32)
    w = 2.0 * math.pi * jnp.arange(L, dtype=F32) / L
    bands = jnp.linspace(1e-4, POS_BANDS - 1, POS_BANDS, dtype=F32)
    z = jnp.concatenate([t[:, None], jnp.cos(w[:, None] * bands), -jnp.sin(w[:, None] * bands)], axis=-1)
    return jnp.pad(z, ((0, 0), (0, V7X_LANES - POS_DIM))), t[:, None]


def _filter_kernel(z_ref, t_ref, w1_ref, b1_ref, f1_ref, w2_ref, b2_ref, f2_ref,
                   w3f_ref, w3b_ref, dl_ref, chi_ref, clo_ref, shi_ref, slo_ref, o_ref, *, L):
    z_hi, z_lo = _split(z_ref[...])
    a = jnp.sin(f1_ref[...] * (_dot3(z_hi, z_lo, *_split(w1_ref[...])) + b1_ref[...]))
    a = jnp.sin(f2_ref[...] * (_dot3(*_split(a), *_split(w2_ref[...])) + b2_ref[...]))
    a_hi, a_lo = _split(a)
    decay = jnp.exp(-t_ref[...] * dl_ref[...])
    row = lax.broadcasted_iota(jnp.int32, (L, 1), 0)
    alt = jnp.where(row % 2 == 0, 1.0, -1.0).astype(F32)
    for o in range(HY_ORDER):
        fwd = _dot3(a_hi, a_lo, *_split(w3f_ref[o])) * decay
        bwd = _dot3(a_hi, a_lo, *_split(w3b_ref[o])) * decay
        s = fwd + jnp.where(row == 0, 0.0, bwd)
        d = bwd - fwd
        hr = _dot3(chi_ref[...], clo_ref[...], *_split(s)) * (1.0 / L)
        hi = _dot3(shi_ref[...], slo_ref[...], *_split(d)) * (1.0 / L)
        nyq = jnp.sum(alt * s, axis=0, keepdims=True) * (0.5 / L)
        o_ref[o, 0] = jnp.where(row == 0, 0.5 * hr, hr)
        o_ref[o, 1] = hi
        o_ref[o, 2] = jnp.where(row == 0, nyq, hr)


def _filter_spectra(L, p, tables):
    z, t = _filter_features(L)
    w1 = jnp.pad(p['flt_w1'], ((0, V7X_LANES - POS_DIM), (0, 0)))
    w3 = p['flt_w3'].reshape(FILT_HID, HY_ORDER, 2, D_H)
    w3f = jnp.transpose(w3[:, :, 0], (1, 0, 2))
    w3b = jnp.transpose(w3[:, :, 1], (1, 0, 2))
    deltas = jnp.abs(jnp.linspace(DECAY_MIN, DECAY_MAX, D_H, dtype=F32)).reshape(1, D_H)
    full = lambda shape: pl.BlockSpec(shape, lambda c: (0,) * len(shape))
    return pl.pallas_call(
        functools.partial(_filter_kernel, L=L),
        grid=(D_H // HY_TC,),
        in_specs=[
            full((L, V7X_LANES)), full((L, 1)),
            full((V7X_LANES, FILT_HID)), full((1, FILT_HID)), full((1, FILT_HID)),
            full((FILT_HID, FILT_HID)), full((1, FILT_HID)), full((1, FILT_HID)),
            pl.BlockSpec((HY_ORDER, FILT_HID, HY_TC), lambda c: (0, 0, c)),
            pl.BlockSpec((HY_ORDER, FILT_HID, HY_TC), lambda c: (0, 0, c)),
            pl.BlockSpec((1, HY_TC), lambda c: (0, c)),
            full((L, L)), full((L, L)), full((L, L)), full((L, L)),
        ],
        out_specs=pl.BlockSpec((HY_ORDER, 3, L, HY_TC), lambda c: (0, 0, 0, c)),
        out_shape=jax.ShapeDtypeStruct((HY_ORDER, 3, L, D_H), F32),
        compiler_params=_params(("arbitrary",)),
        name=f"hyena_filter_{L}",
    )(z, t, w1, p['flt_b1'].reshape(1, -1), p['flt_f1'].reshape(1, -1),
      p['flt_w2'], p['flt_b2'].reshape(1, -1), p['flt_f2'].reshape(1, -1),
      w3f, w3b, deltas, *tables['split'])


def _hyena_kernel(v_ref, x1_ref, x2_ref, wv_ref, wx1_ref, wx2_ref, bv_ref, bx1_ref, bx2_ref,
                  tab_ref, skip_ref, fwd_ref, inv_ref, o_ref, *, L):
    row = lax.broadcasted_iota(jnp.int32, (L, 1), 0)
    alt = jnp.where(row % 2 == 0, 1.0, -1.0).astype(F32)

    def short_conv(u_ref, w_ref, b_ref):
        u = u_ref[...]
        prev = jnp.where(row == 0, 0.0, pltpu.roll(u, 1, 0))
        nxt = jnp.where(row == L - 1, 0.0, pltpu.roll(u, L - 1, 0))
        return prev * w_ref[0:1, :] + u * w_ref[1:2, :] + nxt * w_ref[2:3, :] + b_ref[...]

    z = short_conv(v_ref, wv_ref, bv_ref)
    gates = (short_conv(x1_ref, wx1_ref, bx1_ref), short_conv(x2_ref, wx2_ref, bx2_ref))
    for o in range(HY_ORDER):
        spec = _dot(fwd_ref[...], z.astype(BF16))
        re = spec[:L]
        nyq = jnp.sum(alt * z, axis=0, keepdims=True)
        im = jnp.where(row == 0, nyq, spec[L:])
        p, q, r = tab_ref[o, 0], tab_ref[o, 1], tab_ref[o, 2]
        ya = re * p - im * q
        yb = re * q + im * r
        y = _dot(inv_ref[...], jnp.concatenate([ya, yb], axis=0).astype(BF16)) + alt * yb[0:1, :]
        z = gates[o] * (y + skip_ref[o:o + 1, :] * z)
    o_ref[...] = z.astype(o_ref.dtype)


def _hyena(proj, L, spectra, tables, p):
    m = proj.shape[0]
    tc = min(D_H, HY_TILE_ELEMS // L)
    nc = D_H // tc
    u = lambda part: pl.BlockSpec((L, tc), lambda c, b: (b, part * nc + c))
    cw = lambda part: pl.BlockSpec((3, tc), lambda c, b: (0, part * nc + c))
    cb = lambda part: pl.BlockSpec((1, tc), lambda c, b: (0, part * nc + c))
    return pl.pallas_call(
        functools.partial(_hyena_kernel, L=L),
        grid=(nc, m // L),
        in_specs=[u(0), u(1), u(2), cw(0), cw(1), cw(2), cb(0), cb(1), cb(2),
                  pl.BlockSpec((HY_ORDER, 3, L, tc), lambda c, b: (0, 0, 0, c)),
                  pl.BlockSpec((HY_ORDER, tc), lambda c, b: (0, c)),
                  pl.BlockSpec((2 * L, L), lambda c, b: (0, 0)),
                  pl.BlockSpec((L, 2 * L), lambda c, b: (0, 0))],
        out_specs=pl.BlockSpec((L, tc), lambda c, b: (b, c)),
        out_shape=jax.ShapeDtypeStruct((m, D_H), BF16),
        compiler_params=_params(("arbitrary", "arbitrary")),
        name=f"hyena_{L}",
    )(proj, proj, proj, p['conv_w'], p['conv_w'], p['conv_w'],
      p['conv_b'].reshape(1, -1), p['conv_b'].reshape(1, -1), p['conv_b'].reshape(1, -1),
      spectra, p['hy_skip'], tables['fwd'], tables['inv'])


def _gmlp_kernel(u_ref, v_ref, lg_ref, lb_ref, ws_ref, bs_ref, o_ref):
    v = v_ref[...]
    mu = jnp.mean(v, axis=-1, keepdims=True)
    vc = v - mu
    var = jnp.mean(vc * vc, axis=-1, keepdims=True)
    vn = (vc * lax.rsqrt(var + EPS) * lg_ref[...] + lb_ref[...]).astype(BF16)
    for n in range(GM_ROWS // CHUNK):
        rows = slice(n * CHUNK, (n + 1) * CHUNK)
        for g in range(N_GM_GROUPS):
            cols = slice(g * CHUNK, (g + 1) * CHUNK)
            mixed = _dot(ws_ref[g], vn[rows, cols]) + bs_ref[:, g:g + 1]
            o_ref[rows, cols] = (u_ref[rows, cols] * mixed).astype(o_ref.dtype)


def _gmlp(proj, p):
    m = proj.shape[0]
    cu = COL_GM // D_G
    return pl.pallas_call(
        _gmlp_kernel,
        grid=(m // GM_ROWS,),
        in_specs=[
            pl.BlockSpec((GM_ROWS, D_G), lambda i: (i, cu)),
            pl.BlockSpec((GM_ROWS, D_G), lambda i: (i, cu + 1)),
            pl.BlockSpec((1, D_G), lambda i: (0, 0)),
            pl.BlockSpec((1, D_G), lambda i: (0, 0)),
            pl.BlockSpec((N_GM_GROUPS, CHUNK, CHUNK), lambda i: (0, 0, 0)),
            pl.BlockSpec((CHUNK, N_GM_GROUPS), lambda i: (0, 0)),
        ],
        out_specs=pl.BlockSpec((GM_ROWS, D_G), lambda i: (i, 0)),
        out_shape=jax.ShapeDtypeStruct((m, D_G), BF16),
        compiler_params=_params(("arbitrary",)),
        name="gmlp",
    )(proj, proj, p['gm_ln_g'].reshape(1, -1), p['gm_ln_b'].reshape(1, -1),
      p['gm_ws'].astype(BF16), p['gm_bs'].T)


def _rms(x, g):
    return x * lax.rsqrt(jnp.mean(x * x, axis=-1, keepdims=True) + EPS) * g


def _rope(x, cos, sin_signed, lane):
    partner = jnp.where((lane % 64) < ROT_FREQS,
                        pltpu.roll(x, HEAD_DIM - ROT_FREQS, 1), pltpu.roll(x, ROT_FREQS, 1))
    return x * cos + partner * sin_signed


def _softmax_pv(q_all, k_all, v_all, mask, sink_col):
    s = lax.dot_general(q_all, k_all, (((1,), (1,)), ((), ())), preferred_element_type=F32) * ATT_SCALE
    if mask is not None:
        s = jnp.where(mask, s, NEG)
    m = jnp.maximum(jnp.max(s, axis=-1, keepdims=True), sink_col)
    e = jnp.exp(s - m)
    den = jnp.sum(e, axis=-1, keepdims=True) + jnp.exp(sink_col - m)
    return _dot((e / den).astype(BF16), v_all)


def _sink_column(sink_ref, rows):
    return jnp.concatenate(
        [jnp.broadcast_to(sink_ref[g:g + 1, 0:1], (rows, 1)) for g in range(GQA_GROUP)], axis=0)


def _ctx_att_kernel(qa_ref, qb_ref, k_ref, v_ref, qg_ref, kg_ref, sink_ref, o_ref, ko_ref, vo_ref, *, L):
    qw = GQA_GROUP * HEAD_DIM
    q_halves = (qa_ref, qb_ref)
    for h in range(N_KV_HEADS):
        q_ref = q_halves[h // 2]
        q0 = (h % 2) * qw
        q_all = jnp.concatenate(
            [_rms(q_ref[:, q0 + g * HEAD_DIM:q0 + (g + 1) * HEAD_DIM], qg_ref[...]) for g in range(GQA_GROUP)],
            axis=0)
        cols = slice(h * HEAD_DIM, (h + 1) * HEAD_DIM)
        k = _rms(k_ref[:, cols], kg_ref[...])
        v = v_ref[:, cols]
        ko_ref[:, cols] = k
        vo_ref[:, cols] = v
        out = _softmax_pv(q_all.astype(BF16), k.astype(BF16), v.astype(BF16), None,
                          _sink_column(sink_ref.at[h], L))
        for g in range(GQA_GROUP):
            o_ref[:, h * qw + g * HEAD_DIM:h * qw + (g + 1) * HEAD_DIM] = out[g * L:(g + 1) * L].astype(o_ref.dtype)


def _sink_rows(sink):
    s = sink.reshape(N_KV_HEADS, GQA_GROUP, 1)
    s = jnp.pad(s, ((0, 0), (0, PAD_ROWS - GQA_GROUP), (0, 0)))
    return jnp.broadcast_to(s, (N_KV_HEADS, PAD_ROWS, V7X_LANES))


def _context_attention(proj, L, p):
    m = proj.shape[0]
    b = m // L
    half = ATT_W // 2
    kv_shape = jax.ShapeDtypeStruct((b, L, KV_W), F32)
    return pl.pallas_call(
        functools.partial(_ctx_att_kernel, L=L),
        grid=(b,),
        in_specs=[
            pl.BlockSpec((L, half), lambda i: (i, COL_Q // half)),
            pl.BlockSpec((L, half), lambda i: (i, COL_Q // half + 1)),
            pl.BlockSpec((L, KV_W), lambda i: (i, COL_K // KV_W)),
            pl.BlockSpec((L, KV_W), lambda i: (i, COL_V // KV_W)),
            pl.BlockSpec((1, HEAD_DIM), lambda i: (0, 0)),
            pl.BlockSpec((1, HEAD_DIM), lambda i: (0, 0)),
            pl.BlockSpec((N_KV_HEADS, PAD_ROWS, V7X_LANES), lambda i: (0, 0, 0)),
        ],
        out_specs=[
            pl.BlockSpec((L, ATT_W), lambda i: (i, 0)),
            pl.BlockSpec((None, L, KV_W), lambda i: (i, 0, 0)),
            pl.BlockSpec((None, L, KV_W), lambda i: (i, 0, 0)),
        ],
        out_shape=[jax.ShapeDtypeStruct((m, ATT_W), BF16), kv_shape, kv_shape],
        compiler_params=_params(("arbitrary",)),
        name="context_attention",
    )(proj, proj, proj, proj, p['q_norm_g'].reshape(1, -1), p['k_norm_g'].reshape(1, -1),
      _sink_rows(p['sink']))


def _rope_tables(L):
    rows = L // GRID_W
    row = jnp.repeat(jnp.arange(rows), GRID_W)
    col = jnp.tile(jnp.arange(GRID_W), rows)
    inv = ROPE_BASE ** (-jnp.arange(ROT_FREQS, dtype=F32) / ROT_FREQS)
    pos = jnp.stack([row, col], axis=-1).astype(F32)
    ang = pos[:, :, None] * inv
    cos, sin = jnp.cos(ang), jnp.sin(ang)
    cos_t = jnp.stack([cos, cos], axis=2).reshape(L, HEAD_DIM)
    sin_t = jnp.stack([-sin, sin], axis=2).reshape(L, HEAD_DIM)
    return cos_t, sin_t


def _lat_att_kernel(q_ref, k_ref, v_ref, kc_ref, vc_ref, cos_ref, sin_ref, qg_ref, kg_ref, sink_ref,
                    o_ref, *, nb):
    lane = lax.broadcasted_iota(jnp.int32, (1, HEAD_DIM), 1)
    k_seq = _rope(_rms(k_ref[...], kg_ref[...]), cos_ref[...], sin_ref[...], lane).astype(BF16)
    v_seq = v_ref[...].astype(BF16)
    k_ctx = kc_ref[...].astype(BF16)
    v_ctx = vc_ref[...].astype(BF16)
    n_ctx = k_ctx.shape[0]
    sink_col = _sink_column(sink_ref, BLOCK)

    def band_mask(has_prev, has_next):
        n_keys = n_ctx + (1 + has_prev + has_next) * BLOCK
        shape = (GQA_GROUP * BLOCK, n_keys)
        i = lax.broadcasted_iota(jnp.int32, shape, 0) % BLOCK
        c = lax.broadcasted_iota(jnp.int32, shape, 1)
        ok = None
        if has_prev:
            ok = (c < n_ctx) | (c >= n_ctx + BLOCK) | (c - n_ctx >= i)
        if has_next:
            start = n_keys - BLOCK
            nxt = (c < start) | (c - start <= i)
            ok = nxt if ok is None else ok & nxt
        return ok

    masks = {}
    for n in range(nb):
        lo, hi = max(n - 1, 0), min(n + 1, nb - 1)
        shape_key = (lo < n, hi > n)
        if shape_key not in masks:
            masks[shape_key] = band_mask(*shape_key)
        rows = slice(n * BLOCK, (n + 1) * BLOCK)
        cos, sin = cos_ref[rows, :], sin_ref[rows, :]
        q_all = jnp.concatenate(
            [_rope(_rms(q_ref[rows, g * HEAD_DIM:(g + 1) * HEAD_DIM], qg_ref[...]), cos, sin, lane)
             for g in range(GQA_GROUP)], axis=0).astype(BF16)
        band = slice(lo * BLOCK, (hi + 1) * BLOCK)
        k_all = jnp.concatenate([k_ctx, k_seq[band]], axis=0)
        v_all = jnp.concatenate([v_ctx, v_seq[band]], axis=0)
        out = _softmax_pv(q_all, k_all, v_all, masks[shape_key], sink_col)
        for g in range(GQA_GROUP):
            o_ref[rows, g * HEAD_DIM:(g + 1) * HEAD_DIM] = out[g * BLOCK:(g + 1) * BLOCK].astype(o_ref.dtype)


def _window_attention(proj, L, cache_k, cache_v, layer, p):
    m = proj.shape[0]
    b = m // L
    nb = L // BLOCK
    n_ctx = cache_k.shape[2]
    qw = GQA_GROUP * HEAD_DIM
    cos_t, sin_t = _rope_tables(L)
    seq = lambda col0: pl.BlockSpec((L, HEAD_DIM), lambda i, h: (i, col0 // HEAD_DIM + h))
    cache = pl.BlockSpec((None, None, n_ctx, HEAD_DIM), lambda i, h: (i, layer, 0, h))
    table = pl.BlockSpec((L, HEAD_DIM), lambda i, h: (0, 0))
    gain = pl.BlockSpec((1, HEAD_DIM), lambda i, h: (0, 0))
    return pl.pallas_call(
        functools.partial(_lat_att_kernel, nb=nb),
        grid=(b, N_KV_HEADS),
        in_specs=[
            pl.BlockSpec((L, qw), lambda i, h: (i, COL_Q // qw + h)),
            seq(COL_K), seq(COL_V), cache, cache, table, table, gain, gain,
            pl.BlockSpec((None, PAD_ROWS, V7X_LANES), lambda i, h: (h, 0, 0)),
        ],
        out_specs=pl.BlockSpec((L, qw), lambda i, h: (i, h)),
        out_shape=jax.ShapeDtypeStruct((m, ATT_W), BF16),
        compiler_params=_params(("arbitrary", "arbitrary")),
        name="window_attention",
    )(proj, proj, proj, cache_k, cache_v, cos_t, sin_t,
      p['q_norm_g'].reshape(1, -1), p['k_norm_g'].reshape(1, -1), _sink_rows(p['sink']))


def _mixer_branches(proj, L, p, tables, cache, layer):
    hy = _hyena(proj, L, _filter_spectra(L, p, tables), tables, p)
    gm = _gmlp(proj, p)
    if cache is None:
        att, k_new, v_new = _context_attention(proj, L, p)
    else:
        att = _window_attention(proj, L, cache[0], cache[1], layer, p)
        k_new = v_new = None
    return (hy, att, gm), k_new, v_new


_LAYER_PARAMS = ('norm1_g', 'norm2_g', 'conv_w', 'conv_b', 'flt_w1', 'flt_b1', 'flt_f1', 'flt_w2',
                 'flt_b2', 'flt_f2', 'flt_w3', 'hy_skip', 'q_norm_g', 'k_norm_g', 'sink',
                 'gm_ln_g', 'gm_ln_b', 'gm_ws', 'gm_bs', 'b_gate')


def kernel(x_prompt, x_sample, cache_k, cache_v, c, c_ctx, w_mod, b_mod, norm1_g, norm2_g,
           w_in, conv_w, conv_b, flt_w1, flt_b1, flt_f1, flt_w2, flt_b2, flt_f2, flt_w3,
           hy_skip, q_norm_g, k_norm_g, sink, gm_ln_g, gm_ln_b, gm_ws, gm_bs,
           w_p_hy, w_p_at, w_p_gm, w_gate, b_gate, w_out, w_up, w_down):
    args = dict(locals())
    batch, seq, d = x_prompt.shape
    dec_batch, dec_seq, _ = x_sample.shape
    depth = w_mod.shape[0]
    assert 1 + dec_batch <= PAD_ROWS

    c_rows = jnp.concatenate(
        [c_ctx[None, :], c, jnp.zeros((PAD_ROWS - 1 - dec_batch, d), F32)], axis=0)
    mods = _modulation(c_rows, w_mod, b_mod).reshape(depth, PAD_ROWS, 1, 6 * d)

    tables = {L: _dft_tables(L) for L in (seq, dec_seq)}
    cache = (cache_k.reshape(dec_batch, depth, -1, KV_W), cache_v.reshape(dec_batch, depth, -1, KV_W))

    y_p = x_prompt.reshape(batch * seq, d)
    y_s = x_sample.reshape(dec_batch * dec_seq, d)
    ks, vs = [], []
    ctx_seg = (0, batch * seq)
    lat_seg = (1, dec_seq)
    w_in_b = (w_in[:1].astype(BF16), 0)
    w_p_hy_b = w_p_hy.astype(BF16)
    w_p_gm_b = w_p_gm.astype(BF16)
    for l in range(depth):
        p = {name: args[name][l] for name in _LAYER_PARAMS}
        m_l = mods[l]
        h_c = _normmod(y_p, p['norm1_g'], m_l, *ctx_seg, sh_chunk=0, sc_chunk=1)
        h_s = _normmod(y_s, p['norm1_g'], m_l, *lat_seg, sh_chunk=0, sc_chunk=1)
        proj_c, w_gate_b = _matmul(h_c, w_in_b, _mm_kernel, F32, "in_proj", cast=(w_gate, l))
        proj_s, w_p_at_b = _matmul(h_s, w_in_b, _mm_kernel, F32, "in_proj", cast=(w_p_at, l))
        gates_c, w_up_b = _matmul(h_c, (w_gate_b, 0), _mm_sigmoid_kernel, BF16, "gate_proj",
                                  bias=p['b_gate'], cast=(w_up, l))
        gates_s, w_out_b = _matmul(h_s, (w_gate_b, 0), _mm_sigmoid_kernel, BF16, "gate_proj",
                                   bias=p['b_gate'], cast=(w_out, l))
        br_c, k_l, v_l = _mixer_branches(proj_c, seq, p, tables[seq], None, l)
        br_s, _, _ = _mixer_branches(proj_s, dec_seq, p, tables[dec_seq], cache, l)
        ks.append(k_l.reshape(batch, seq, N_KV_HEADS, HEAD_DIM))
        vs.append(v_l.reshape(batch, seq, N_KV_HEADS, HEAD_DIM))
        w_branch = ((w_p_hy_b, l), (w_p_at_b, 0), (w_p_gm_b, l))
        merged_c = _merge(*br_c, *w_branch, gates_c)
        merged_s = _merge(*br_s, *w_branch, gates_s)
        y_p, = _matmul_residual(merged_c, (w_out_b, 0), y_p, m_l, *ctx_seg, 2, "out_proj")
        y_s, = _matmul_residual(merged_s, (w_out_b, 0), y_s, m_l, *lat_seg, 2, "out_proj")
        h_c = _normmod(y_p, p['norm2_g'], m_l, *ctx_seg, sh_chunk=3, sc_chunk=4)
        h_s = _normmod(y_s, p['norm2_g'], m_l, *lat_seg, sh_chunk=3, sc_chunk=4)
        act_c, w_down_b = _matmul(h_c, (w_up_b, 0), _mm_relu2_kernel, BF16, "mlp_up", cast=(w_down, l))
        act_s, = _matmul(h_s, (w_up_b, 0), _mm_relu2_kernel, BF16, "mlp_up")
        if l + 1 < depth:
            y_p, w_in_next = _matmul_residual(act_c, (w_down_b, 0), y_p, m_l, *ctx_seg, 5, "mlp_down",
                                              cast=(w_in, l + 1))
            w_in_b = (w_in_next, 0)
        else:
            y_p, = _matmul_residual(act_c, (w_down_b, 0), y_p, m_l, *ctx_seg, 5, "mlp_down")
        y_s, = _matmul_residual(act_s, (w_down_b, 0), y_s, m_l, *lat_seg, 5, "mlp_down")
    return (y_p.reshape(batch, seq, d), y_s.reshape(dec_batch, dec_seq, d),
            jnp.stack(ks, axis=1), jnp.stack(vs, axis=1))
```

```python
import functools
import math

import jax
import jax.numpy as jnp
from jax import lax
from jax.experimental import pallas as pl
from jax.experimental.pallas import tpu as pltpu

F32 = jnp.float32
BF16 = jnp.bfloat16

D_MODEL = 4096
HEAD_DIM = 128
N_Q_HEADS = 16
N_KV_HEADS = 4
GQA_GROUP = 4
ATT_W = N_Q_HEADS * HEAD_DIM
KV_W = N_KV_HEADS * HEAD_DIM
GRID_W = 64
WINDOW = 128
BLOCK = 128
ROPE_BASE = 10000.0
ROT_FREQS = 32
D_H = 1024
HY_ORDER = 2
POS_BANDS = 16
POS_DIM = 1 + 2 * POS_BANDS
FILT_HID = 64
DECAY_MIN = math.log(1e-2) / 1.5
DECAY_MAX = math.log(1e-2) / 0.3
D_G = 1024
CHUNK = 128
N_GM_GROUPS = 8
D_FF = 4 * D_MODEL
IN_COLS = 3 * D_H + ATT_W + 2 * KV_W + 2 * D_G
EPS = 1e-6
NEG = -1e30
ATT_SCALE = HEAD_DIM ** -0.5

COL_HY = 0
COL_Q = 3 * D_H
COL_K = COL_Q + ATT_W
COL_V = COL_K + KV_W
COL_GM = COL_V + KV_W

V7X_LANES = 128
V7X_VMEM_BYTES = 64 * 1024 * 1024
VMEM_LIMIT = 60 * 1024 * 1024

TM = 1024
TN = 1024
TN_KSPLIT = 512
TN_MERGE = 512
TK = 4096
NORM_ROWS = 512
HY_TC = 256
HY_TILE_ELEMS = 256 * 1024
HY_SEQS = 2
GM_ROWS = 512
MOD_TN = 512
PAD_ROWS = 8
BF16_SUBLANES = 16


def _params(sem):
    return pltpu.CompilerParams(dimension_semantics=sem, vmem_limit_bytes=VMEM_LIMIT)


def _split(a):
    hi = a.astype(BF16)
    lo = (a - hi.astype(F32)).astype(BF16)
    return hi, lo


def _dot(a, b):
    return jnp.dot(a, b, preferred_element_type=F32)


def _dot3(a_hi, a_lo, b_hi, b_lo):
    return _dot(a_hi, b_hi) + _dot(a_lo, b_hi) + _dot(a_hi, b_lo)


def _mod_kernel(c_ref, w_ref, b_ref, o_ref):
    c = c_ref[...]
    x = c * jax.nn.sigmoid(c)
    x_hi, x_lo = _split(x)
    w_hi, w_lo = _split(w_ref[...])
    o_ref[...] = _dot3(x_hi, x_lo, w_hi, w_lo) + b_ref[...]


def _modulation(c_rows, w_mod, b_mod):
    depth, d, n = w_mod.shape
    return pl.pallas_call(
        _mod_kernel,
        grid=(depth, n // MOD_TN),
        in_specs=[
            pl.BlockSpec((PAD_ROWS, d), lambda l, j: (0, 0)),
            pl.BlockSpec((None, d, MOD_TN), lambda l, j: (l, 0, j)),
            pl.BlockSpec((None, 1, MOD_TN), lambda l, j: (l, 0, j)),
        ],
        out_specs=pl.BlockSpec((None, PAD_ROWS, MOD_TN), lambda l, j: (l, 0, j)),
        out_shape=jax.ShapeDtypeStruct((depth, PAD_ROWS, n), F32),
        compiler_params=_params(("arbitrary", "arbitrary")),
        name="modulation",
    )(c_rows, w_mod, b_mod.reshape(depth, 1, n))


def _normmod_kernel(x_ref, g_ref, sc_ref, sh_ref, o_ref):
    x = x_ref[...]
    y = x * lax.rsqrt(jnp.mean(x * x, axis=-1, keepdims=True) + EPS) * g_ref[...]
    o_ref[...] = (y * (1.0 + sc_ref[...]) + sh_ref[...]).astype(o_ref.dtype)


def _normmod(x, g, mods, seg0, rows_per_seg, sh_chunk, sc_chunk):
    m, d = x.shape
    tiles_per_seg = rows_per_seg // NORM_ROWS
    seg = lambda i: seg0 + i // tiles_per_seg
    return pl.pallas_call(
        _normmod_kernel,
        grid=(m // NORM_ROWS,),
        in_specs=[
            pl.BlockSpec((NORM_ROWS, d), lambda i: (i, 0)),
            pl.BlockSpec((1, d), lambda i: (0, 0)),
            pl.BlockSpec((None, 1, d), lambda i: (seg(i), 0, sc_chunk)),
            pl.BlockSpec((None, 1, d), lambda i: (seg(i), 0, sh_chunk)),
        ],
        out_specs=pl.BlockSpec((NORM_ROWS, d), lambda i: (i, 0)),
        out_shape=jax.ShapeDtypeStruct((m, d), BF16),
        compiler_params=_params(("arbitrary",)),
        name="normmod",
    )(x, g.reshape(1, d), mods, mods)


def _mm_kernel(x_ref, w_ref, o_ref):
    o_ref[...] = _dot(x_ref[...], w_ref[...]).astype(o_ref.dtype)


def _mm_sigmoid_kernel(x_ref, w_ref, b_ref, o_ref):
    t = _dot(x_ref[...], w_ref[...]) + b_ref[...]
    o_ref[...] = (0.5 * jnp.tanh(0.5 * t) + 0.5).astype(o_ref.dtype)


def _mm_relu2_kernel(x_ref, w_ref, o_ref):
    a = jnp.maximum(_dot(x_ref[...], w_ref[...]), 0.0)
    o_ref[...] = (a * a).astype(o_ref.dtype)


def _call(body, grid, in_specs, out_specs, out_shape, args, name, scratch_shapes=(), cast=None):
    sem = ("arbitrary",) * len(grid)
    if cast is None:
        out = pl.pallas_call(body, grid=grid, in_specs=in_specs, out_specs=out_specs, out_shape=out_shape,
                             scratch_shapes=scratch_shapes, compiler_params=_params(sem), name=name)(*args)
        return tuple(out)
    src, layer = cast
    _, k2, n2 = src.shape
    steps = math.prod(grid)
    slabs = min(1 << (steps.bit_length() - 1), k2 // BF16_SUBLANES)
    rows = k2 // slabs

    def slab(*ids):
        step = ids[0]
        for extent, idx in zip(grid[1:], ids[1:]):
            step = step * extent + idx
        return jnp.minimum(step, slabs - 1)

    n_in, n_out = len(in_specs), len(out_specs)

    def kernel(*refs):
        src_ref, dst_ref = refs[n_in], refs[n_in + 1 + n_out]
        body(*refs[:n_in], *refs[n_in + 1:n_in + 1 + n_out], *refs[n_in + 2 + n_out:])
        dst_ref[...] = src_ref[...].astype(BF16)

    out = pl.pallas_call(
        kernel, grid=grid,
        in_specs=[*in_specs, pl.BlockSpec((None, rows, n2), lambda *ids: (layer, slab(*ids), 0))],
        out_specs=[*out_specs, pl.BlockSpec((None, rows, n2), lambda *ids: (0, slab(*ids), 0))],
        out_shape=[*out_shape, jax.ShapeDtypeStruct((1, k2, n2), BF16)],
        scratch_shapes=scratch_shapes, compiler_params=_params(sem), name=name)(*args, src)
    return tuple(out)


def _matmul(x, w, kernel, out_dtype, name, bias=None, cast=None):
    w_arr, layer = w
    m, k = x.shape
    n = w_arr.shape[2]
    in_specs = [
        pl.BlockSpec((TM, k), lambda i, j: (i, 0)),
        pl.BlockSpec((None, k, TN), lambda i, j: (layer, 0, j)),
    ]
    args = [x, w_arr]
    if bias is not None:
        in_specs.append(pl.BlockSpec((1, TN), lambda i, j: (0, j)))
        args.append(bias.reshape(1, n))
    return _call(kernel, (m // TM, n // TN), in_specs,
                 [pl.BlockSpec((TM, TN), lambda i, j: (i, j))],
                 [jax.ShapeDtypeStruct((m, n), out_dtype)], args, name, cast=cast)


def _mm_residual_kernel(a_ref, w_ref, x_ref, g_ref, o_ref):
    o_ref[...] = x_ref[...] + g_ref[...] * _dot(a_ref[...], w_ref[...])


def _mm_residual_ksplit_kernel(a_ref, w_ref, x_ref, g_ref, o_ref, acc_ref, *, nk):
    kk = pl.program_id(1)
    j = pl.program_id(2)

    @pl.when(kk == 0)
    def _():
        acc_ref[j] = _dot(a_ref[...], w_ref[...])

    @pl.when((kk > 0) & (kk < nk - 1))
    def _():
        acc_ref[j] += _dot(a_ref[...], w_ref[...])

    @pl.when(kk == nk - 1)
    def _():
        o_ref[...] = x_ref[...] + g_ref[...] * (acc_ref[j] + _dot(a_ref[...], w_ref[...]))


def _matmul_residual(a, w, x, mods, seg0, rows_per_seg, g_chunk, name, cast=None):
    w_arr, layer = w
    m, k = a.shape
    n = w_arr.shape[2]
    tiles_per_seg = rows_per_seg // TM
    seg = lambda i: seg0 + i // tiles_per_seg
    if k <= TK:
        nb = n // TN
        return _call(
            _mm_residual_kernel, (m // TM, nb),
            [
                pl.BlockSpec((TM, k), lambda i, j: (i, 0)),
                pl.BlockSpec((None, k, TN), lambda i, j: (layer, 0, j)),
                pl.BlockSpec((TM, TN), lambda i, j: (i, j)),
                pl.BlockSpec((None, 1, TN), lambda i, j: (seg(i), 0, g_chunk * nb + j)),
            ],
            [pl.BlockSpec((TM, TN), lambda i, j: (i, j))],
            [jax.ShapeDtypeStruct((m, n), F32)], (a, w_arr, x, mods), name, cast=cast)
    nk = k // TK
    tn = TN_KSPLIT
    nb = n // tn
    out_col = lambda kk, j: jnp.where(kk == nk - 1, j, 0)
    return _call(
        functools.partial(_mm_residual_ksplit_kernel, nk=nk), (m // TM, nk, nb),
        [
            pl.BlockSpec((TM, TK), lambda i, kk, j: (i, kk)),
            pl.BlockSpec((None, TK, tn), lambda i, kk, j: (layer, kk, j)),
            pl.BlockSpec((TM, tn), lambda i, kk, j: (i, out_col(kk, j))),
            pl.BlockSpec((None, 1, tn), lambda i, kk, j: (seg(i), 0, g_chunk * nb + out_col(kk, j))),
        ],
        [pl.BlockSpec((TM, tn), lambda i, kk, j: (i, out_col(kk, j)))],
        [jax.ShapeDtypeStruct((m, n), F32)], (a, w_arr, x, mods), name,
        scratch_shapes=[pltpu.VMEM((nb, TM, tn), F32)], cast=cast)


def _merge_kernel(hy_ref, at_ref, gm_ref, whv_ref, wat_ref, wgm_ref, ga_ref, gb_ref, gc_ref, o_ref):
    merged = (ga_ref[...] * _dot(hy_ref[...], whv_ref[...])
              + gb_ref[...] * _dot(at_ref[...], wat_ref[...])
              + gc_ref[...] * _dot(gm_ref[...], wgm_ref[...]))
    o_ref[...] = merged.astype(o_ref.dtype)


def _merge(hy, at, gm, w_hy, w_at, w_gm, gates):
    m = hy.shape[0]
    n = w_hy[0].shape[2]
    tn = TN_MERGE
    nb = n // tn
    row = lambda width: pl.BlockSpec((TM, width), lambda i, j: (i, 0))
    col = lambda w: pl.BlockSpec((None, w[0].shape[1], tn), lambda i, j: (w[1], 0, j))
    gate = lambda b: pl.BlockSpec((TM, tn), lambda i, j: (i, b * nb + j))
    return _call(
        _merge_kernel, (m // TM, nb),
        [row(hy.shape[1]), row(at.shape[1]), row(gm.shape[1]), col(w_hy), col(w_at), col(w_gm),
         gate(0), gate(1), gate(2)],
        [pl.BlockSpec((TM, tn), lambda i, j: (i, j))],
        [jax.ShapeDtypeStruct((m, n), BF16)],
        (hy, at, gm, w_hy[0], w_at[0], w_gm[0], gates, gates, gates), "merge")[0]


def _dft_tables(L):
    k = jnp.arange(L, dtype=jnp.int32)
    ks = (k[:, None] * k[None, :]) % (2 * L)
    ang = ks.astype(F32) * (math.pi / L)
    cos, sin = jnp.cos(ang), jnp.sin(ang)
    return {
        'split': _split(cos) + _split(sin),
        'fwd': jnp.concatenate([cos, -sin], axis=0).astype(BF16),
        'inv': jnp.concatenate([cos, -sin], axis=1).astype(BF16),
    }


def _filter_features(L):
    t = jnp.linspace(0.0, 1.0, L, dtype=F32)
    w = 2.0 * math.pi * jnp.arange(L, dtype=F32) / L
    bands = jnp.linspace(1e-4, POS_BANDS - 1, POS_BANDS, dtype=F32)
    z = jnp.concatenate([t[:, None], jnp.cos(w[:, None] * bands), -jnp.sin(w[:, None] * bands)], axis=-1)
    return jnp.pad(z, ((0, 0), (0, V7X_LANES - POS_DIM))), t[:, None]


def _filter_kernel(z_ref, t_ref, w1_ref, b1_ref, f1_ref, w2_ref, b2_ref, f2_ref,
                   w3f_ref, w3b_ref, dl_ref, chi_ref, clo_ref, shi_ref, slo_ref, o_ref, *, L):
    z_hi, z_lo = _split(z_ref[...])
    a = jnp.sin(f1_ref[...] * (_dot3(z_hi, z_lo, *_split(w1_ref[...])) + b1_ref[...]))
    a = jnp.sin(f2_ref[...] * (_dot3(*_split(a), *_split(w2_ref[...])) + b2_ref[...]))
    a_hi, a_lo = _split(a)
    decay = jnp.exp(-t_ref[...] * dl_ref[...])
    row = lax.broadcasted_iota(jnp.int32, (L, 1), 0)
    alt = jnp.where(row % 2 == 0, 1.0, -1.0).astype(F32)
    for o in range(HY_ORDER):
        fwd = _dot3(a_hi, a_lo, *_split(w3f_ref[o])) * decay
        bwd = _dot3(a_hi, a_lo, *_split(w3b_ref[o])) * decay
        s = fwd + jnp.where(row == 0, 0.0, bwd)
        d = bwd - fwd
        hr = _dot3(chi_ref[...], clo_ref[...], *_split(s)) * (1.0 / L)
        hi = _dot3(shi_ref[...], slo_ref[...], *_split(d)) * (1.0 / L)
        nyq = jnp.sum(alt * s, axis=0, keepdims=True) * (0.5 / L)
        o_ref[o, 0] = jnp.where(row == 0, 0.5 * hr, hr)
        o_ref[o, 1] = hi
        o_ref[o, 2] = jnp.where(row == 0, nyq, hr)


def _filter_spectra(L, p, tables):
    z, t = _filter_features(L)
    w1 = jnp.pad(p['flt_w1'], ((0, V7X_LANES - POS_DIM), (0, 0)))
    w3 = p['flt_w3'].reshape(FILT_HID, HY_ORDER, 2, D_H)
    w3f = jnp.transpose(w3[:, :, 0], (1, 0, 2))
    w3b = jnp.transpose(w3[:, :, 1], (1, 0, 2))
    deltas = jnp.abs(jnp.linspace(DECAY_MIN, DECAY_MAX, D_H, dtype=F32)).reshape(1, D_H)
    full = lambda shape: pl.BlockSpec(shape, lambda c: (0,) * len(shape))
    return pl.pallas_call(
        functools.partial(_filter_kernel, L=L),
        grid=(D_H // HY_TC,),
        in_specs=[
            full((L, V7X_LANES)), full((L, 1)),
            full((V7X_LANES, FILT_HID)), full((1, FILT_HID)), full((1, FILT_HID)),
            full((FILT_HID, FILT_HID)), full((1, FILT_HID)), full((1, FILT_HID)),
            pl.BlockSpec((HY_ORDER, FILT_HID, HY_TC), lambda c: (0, 0, c)),
            pl.BlockSpec((HY_ORDER, FILT_HID, HY_TC), lambda c: (0, 0, c)),
            pl.BlockSpec((1, HY_TC), lambda c: (0, c)),
            full((L, L)), full((L, L)), full((L, L)), full((L, L)),
        ],
        out_specs=pl.BlockSpec((HY_ORDER, 3, L, HY_TC), lambda c: (0, 0, 0, c)),
        out_shape=jax.ShapeDtypeStruct((HY_ORDER, 3, L, D_H), F32),
        compiler_params=_params(("arbitrary",)),
        name=f"hyena_filter_{L}",
    )(z, t, w1, p['flt_b1'].reshape(1, -1), p['flt_f1'].reshape(1, -1),
      p['flt_w2'], p['flt_b2'].reshape(1, -1), p['flt_f2'].reshape(1, -1),
      w3f, w3b, deltas, *tables['split'])


def _hyena_kernel(v_ref, x1_ref, x2_ref, wv_ref, wx1_ref, wx2_ref, bv_ref, bx1_ref, bx2_ref,
                  tab_ref, skip_ref, fwd_ref, inv_ref, o_ref, *, L):
    row = lax.broadcasted_iota(jnp.int32, (L, 1), 0)
    alt = jnp.where(row % 2 == 0, 1.0, -1.0).astype(F32)

    def short_conv(u_ref, rows, w_ref, b_ref):
        u = u_ref[rows, :]
        prev = jnp.where(row == 0, 0.0, pltpu.roll(u, 1, 0))
        nxt = jnp.where(row == L - 1, 0.0, pltpu.roll(u, L - 1, 0))
        return prev * w_ref[0:1, :] + u * w_ref[1:2, :] + nxt * w_ref[2:3, :] + b_ref[...]

    seqs = [slice(s * L, (s + 1) * L) for s in range(v_ref.shape[0] // L)]
    zs = [short_conv(v_ref, rows, wv_ref, bv_ref) for rows in seqs]
    gates = [(short_conv(x1_ref, rows, wx1_ref, bx1_ref), short_conv(x2_ref, rows, wx2_ref, bx2_ref))
             for rows in seqs]
    for o in range(HY_ORDER):
        p, q, r = tab_ref[o, 0], tab_ref[o, 1], tab_ref[o, 2]
        specs = [_dot(fwd_ref[...], z.astype(BF16)) for z in zs]
        prods = []
        for z, spec in zip(zs, specs):
            re = spec[:L]
            nyq = jnp.sum(alt * z, axis=0, keepdims=True)
            im = jnp.where(row == 0, nyq, spec[L:])
            prods.append((re * p - im * q, re * q + im * r))
        ys = [_dot(inv_ref[...], jnp.concatenate([ya, yb], axis=0).astype(BF16)) + alt * yb[0:1, :]
              for ya, yb in prods]
        zs = [g[o] * (y + skip_ref[o:o + 1, :] * z) for g, y, z in zip(gates, ys, zs)]
    for rows, z in zip(seqs, zs):
        o_ref[rows, :] = z.astype(o_ref.dtype)


def _hyena(proj, L, spectra, tables, p):
    m = proj.shape[0]
    tc = min(D_H, HY_TILE_ELEMS // L)
    nc = D_H // tc
    rows = (HY_SEQS if tc < D_H else 1) * L
    u = lambda part: pl.BlockSpec((rows, tc), lambda c, b: (b, part * nc + c))
    cw = lambda part: pl.BlockSpec((3, tc), lambda c, b: (0, part * nc + c))
    cb = lambda part: pl.BlockSpec((1, tc), lambda c, b: (0, part * nc + c))
    return pl.pallas_call(
        functools.partial(_hyena_kernel, L=L),
        grid=(nc, m // rows),
        in_specs=[u(0), u(1), u(2), cw(0), cw(1), cw(2), cb(0), cb(1), cb(2),
                  pl.BlockSpec((HY_ORDER, 3, L, tc), lambda c, b: (0, 0, 0, c)),
                  pl.BlockSpec((HY_ORDER, tc), lambda c, b: (0, c)),
                  pl.BlockSpec((2 * L, L), lambda c, b: (0, 0)),
                  pl.BlockSpec((L, 2 * L), lambda c, b: (0, 0))],
        out_specs=pl.BlockSpec((rows, tc), lambda c, b: (b, c)),
        out_shape=jax.ShapeDtypeStruct((m, D_H), BF16),
        compiler_params=_params(("arbitrary", "arbitrary")),
        name=f"hyena_{L}",
    )(proj, proj, proj, p['conv_w'], p['conv_w'], p['conv_w'],
      p['conv_b'].reshape(1, -1), p['conv_b'].reshape(1, -1), p['conv_b'].reshape(1, -1),
      spectra, p['hy_skip'], tables['fwd'], tables['inv'])


def _gmlp_kernel(u_ref, v_ref, lg_ref, lb_ref, ws_ref, bs_ref, o_ref):
    v = v_ref[...]
    mu = jnp.mean(v, axis=-1, keepdims=True)
    vc = v - mu
    var = jnp.mean(vc * vc, axis=-1, keepdims=True)
    vn = (vc * lax.rsqrt(var + EPS) * lg_ref[...] + lb_ref[...]).astype(BF16)
    for n in range(GM_ROWS // CHUNK):
        rows = slice(n * CHUNK, (n + 1) * CHUNK)
        for g in range(N_GM_GROUPS):
            cols = slice(g * CHUNK, (g + 1) * CHUNK)
            mixed = _dot(ws_ref[g], vn[rows, cols]) + bs_ref[:, g:g + 1]
            o_ref[rows, cols] = (u_ref[rows, cols] * mixed).astype(o_ref.dtype)


def _gmlp(proj, p):
    m = proj.shape[0]
    cu = COL_GM // D_G
    return pl.pallas_call(
        _gmlp_kernel,
        grid=(m // GM_ROWS,),
        in_specs=[
            pl.BlockSpec((GM_ROWS, D_G), lambda i: (i, cu)),
            pl.BlockSpec((GM_ROWS, D_G), lambda i: (i, cu + 1)),
            pl.BlockSpec((1, D_G), lambda i: (0, 0)),
            pl.BlockSpec((1, D_G), lambda i: (0, 0)),
            pl.BlockSpec((N_GM_GROUPS, CHUNK, CHUNK), lambda i: (0, 0, 0)),
            pl.BlockSpec((CHUNK, N_GM_GROUPS), lambda i: (0, 0)),
        ],
        out_specs=pl.BlockSpec((GM_ROWS, D_G), lambda i: (i, 0)),
        out_shape=jax.ShapeDtypeStruct((m, D_G), BF16),
        compiler_params=_params(("arbitrary",)),
        name="gmlp",
    )(proj, proj, p['gm_ln_g'].reshape(1, -1), p['gm_ln_b'].reshape(1, -1),
      p['gm_ws'].astype(BF16), p['gm_bs'].T)


def _rms(x, g):
    return x * lax.rsqrt(jnp.mean(x * x, axis=-1, keepdims=True) + EPS) * g


def _rope(x, cos, sin_signed, lane):
    partner = jnp.where((lane % 64) < ROT_FREQS,
                        pltpu.roll(x, HEAD_DIM - ROT_FREQS, 1), pltpu.roll(x, ROT_FREQS, 1))
    return x * cos + partner * sin_signed


def _softmax_pv(q_all, k_all, v_all, mask, sink_col):
    s = lax.dot_general(q_all, k_all, (((1,), (1,)), ((), ())), preferred_element_type=F32) * ATT_SCALE
    if mask is not None:
        s = jnp.where(mask, s, NEG)
    m = jnp.maximum(jnp.max(s, axis=-1, keepdims=True), sink_col)
    e = jnp.exp(s - m)
    den = jnp.sum(e, axis=-1, keepdims=True) + jnp.exp(sink_col - m)
    return _dot(e.astype(BF16), v_all) / den


def _sink_column(sink_ref, rows):
    return jnp.concatenate(
        [jnp.broadcast_to(sink_ref[g:g + 1, 0:1], (rows, 1)) for g in range(GQA_GROUP)], axis=0)


def _ctx_att_kernel(qa_ref, qb_ref, k_ref, v_ref, qg_ref, kg_ref, sink_ref, o_ref, ko_ref, vo_ref, *, L):
    qw = GQA_GROUP * HEAD_DIM
    q_halves = (qa_ref, qb_ref)
    for h in range(N_KV_HEADS):
        q_ref = q_halves[h // 2]
        q0 = (h % 2) * qw
        q_all = jnp.concatenate(
            [_rms(q_ref[:, q0 + g * HEAD_DIM:q0 + (g + 1) * HEAD_DIM], qg_ref[...]) for g in range(GQA_GROUP)],
            axis=0)
        cols = slice(h * HEAD_DIM, (h + 1) * HEAD_DIM)
        k = _rms(k_ref[:, cols], kg_ref[...])
        v = v_ref[:, cols]
        ko_ref[:, cols] = k
        vo_ref[:, cols] = v
        out = _softmax_pv(q_all.astype(BF16), k.astype(BF16), v.astype(BF16), None,
                          _sink_column(sink_ref.at[h], L))
        for g in range(GQA_GROUP):
            o_ref[:, h * qw + g * HEAD_DIM:h * qw + (g + 1) * HEAD_DIM] = out[g * L:(g + 1) * L].astype(o_ref.dtype)


def _sink_rows(sink):
    s = sink.reshape(N_KV_HEADS, GQA_GROUP, 1)
    s = jnp.pad(s, ((0, 0), (0, PAD_ROWS - GQA_GROUP), (0, 0)))
    return jnp.broadcast_to(s, (N_KV_HEADS, PAD_ROWS, V7X_LANES))


def _context_attention(proj, L, p):
    m = proj.shape[0]
    b = m // L
    half = ATT_W // 2
    kv_shape = jax.ShapeDtypeStruct((b, L, KV_W), F32)
    return pl.pallas_call(
        functools.partial(_ctx_att_kernel, L=L),
        grid=(b,),
        in_specs=[
            pl.BlockSpec((L, half), lambda i: (i, COL_Q // half)),
            pl.BlockSpec((L, half), lambda i: (i, COL_Q // half + 1)),
            pl.BlockSpec((L, KV_W), lambda i: (i, COL_K // KV_W)),
            pl.BlockSpec((L, KV_W), lambda i: (i, COL_V // KV_W)),
            pl.BlockSpec((1, HEAD_DIM), lambda i: (0, 0)),
            pl.BlockSpec((1, HEAD_DIM), lambda i: (0, 0)),
            pl.BlockSpec((N_KV_HEADS, PAD_ROWS, V7X_LANES), lambda i: (0, 0, 0)),
        ],
        out_specs=[
            pl.BlockSpec((L, ATT_W), lambda i: (i, 0)),
            pl.BlockSpec((None, L, KV_W), lambda i: (i, 0, 0)),
            pl.BlockSpec((None, L, KV_W), lambda i: (i, 0, 0)),
        ],
        out_shape=[jax.ShapeDtypeStruct((m, ATT_W), BF16), kv_shape, kv_shape],
        compiler_params=_params(("arbitrary",)),
        name="context_attention",
    )(proj, proj, proj, proj, p['q_norm_g'].reshape(1, -1), p['k_norm_g'].reshape(1, -1),
      _sink_rows(p['sink']))


def _rope_tables(L):
    rows = L // GRID_W
    row = jnp.repeat(jnp.arange(rows), GRID_W)
    col = jnp.tile(jnp.arange(GRID_W), rows)
    inv = ROPE_BASE ** (-jnp.arange(ROT_FREQS, dtype=F32) / ROT_FREQS)
    pos = jnp.stack([row, col], axis=-1).astype(F32)
    ang = pos[:, :, None] * inv
    cos, sin = jnp.cos(ang), jnp.sin(ang)
    cos_t = jnp.stack([cos, cos], axis=2).reshape(L, HEAD_DIM)
    sin_t = jnp.stack([-sin, sin], axis=2).reshape(L, HEAD_DIM)
    return cos_t, sin_t


def _lat_att_kernel(q_ref, k_ref, v_ref, kc_ref, vc_ref, cos_ref, sin_ref, qg_ref, kg_ref, sink_ref,
                    o_ref, *, nb):
    lane = lax.broadcasted_iota(jnp.int32, (1, HEAD_DIM), 1)
    k_seq = _rope(_rms(k_ref[...], kg_ref[...]), cos_ref[...], sin_ref[...], lane).astype(BF16)
    v_seq = v_ref[...].astype(BF16)
    k_ctx = kc_ref[...].astype(BF16)
    v_ctx = vc_ref[...].astype(BF16)
    n_ctx = k_ctx.shape[0]
    sink_col = _sink_column(sink_ref, BLOCK)

    def band_mask(has_prev, has_next):
        n_keys = n_ctx + (1 + has_prev + has_next) * BLOCK
        shape = (GQA_GROUP * BLOCK, n_keys)
        i = lax.broadcasted_iota(jnp.int32, shape, 0) % BLOCK
        c = lax.broadcasted_iota(jnp.int32, shape, 1)
        ok = None
        if has_prev:
            ok = (c < n_ctx) | (c >= n_ctx + BLOCK) | (c - n_ctx >= i)
        if has_next:
            start = n_keys - BLOCK
            nxt = (c < start) | (c - start <= i)
            ok = nxt if ok is None else ok & nxt
        return ok

    masks = {}
    for n in range(nb):
        lo, hi = max(n - 1, 0), min(n + 1, nb - 1)
        shape_key = (lo < n, hi > n)
        if shape_key not in masks:
            masks[shape_key] = band_mask(*shape_key)
        rows = slice(n * BLOCK, (n + 1) * BLOCK)
        cos, sin = cos_ref[rows, :], sin_ref[rows, :]
        q_all = jnp.concatenate(
            [_rope(_rms(q_ref[rows, g * HEAD_DIM:(g + 1) * HEAD_DIM], qg_ref[...]), cos, sin, lane)
             for g in range(GQA_GROUP)], axis=0).astype(BF16)
        band = slice(lo * BLOCK, (hi + 1) * BLOCK)
        k_all = jnp.concatenate([k_ctx, k_seq[band]], axis=0)
        v_all = jnp.concatenate([v_ctx, v_seq[band]], axis=0)
        out = _softmax_pv(q_all, k_all, v_all, masks[shape_key], sink_col)
        for g in range(GQA_GROUP):
            o_ref[rows, g * HEAD_DIM:(g + 1) * HEAD_DIM] = out[g * BLOCK:(g + 1) * BLOCK].astype(o_ref.dtype)


def _window_attention(proj, L, cache_k, cache_v, layer, p):
    m = proj.shape[0]
    b = m // L
    nb = L // BLOCK
    n_ctx = cache_k.shape[2]
    qw = GQA_GROUP * HEAD_DIM
    cos_t, sin_t = _rope_tables(L)
    seq = lambda col0: pl.BlockSpec((L, HEAD_DIM), lambda i, h: (i, col0 // HEAD_DIM + h))
    cache = pl.BlockSpec((None, None, n_ctx, HEAD_DIM), lambda i, h: (i, layer, 0, h))
    table = pl.BlockSpec((L, HEAD_DIM), lambda i, h: (0, 0))
    gain = pl.BlockSpec((1, HEAD_DIM), lambda i, h: (0, 0))
    return pl.pallas_call(
        functools.partial(_lat_att_kernel, nb=nb),
        grid=(b, N_KV_HEADS),
        in_specs=[
            pl.BlockSpec((L, qw), lambda i, h: (i, COL_Q // qw + h)),
            seq(COL_K), seq(COL_V), cache, cache, table, table, gain, gain,
            pl.BlockSpec((None, PAD_ROWS, V7X_LANES), lambda i, h: (h, 0, 0)),
        ],
        out_specs=pl.BlockSpec((L, qw), lambda i, h: (i, h)),
        out_shape=jax.ShapeDtypeStruct((m, ATT_W), BF16),
        compiler_params=_params(("arbitrary", "arbitrary")),
        name="window_attention",
    )(proj, proj, proj, cache_k, cache_v, cos_t, sin_t,
      p['q_norm_g'].reshape(1, -1), p['k_norm_g'].reshape(1, -1), _sink_rows(p['sink']))


def _mixer_branches(proj, L, p, tables, cache, layer):
    hy = _hyena(proj, L, _filter_spectra(L, p, tables), tables, p)
    gm = _gmlp(proj, p)
    if cache is None:
        att, k_new, v_new = _context_attention(proj, L, p)
    else:
        att = _window_attention(proj, L, cache[0], cache[1], layer, p)
        k_new = v_new = None
    return (hy, att, gm), k_new, v_new


_LAYER_PARAMS = ('norm1_g', 'norm2_g', 'conv_w', 'conv_b', 'flt_w1', 'flt_b1', 'flt_f1', 'flt_w2',
                 'flt_b2', 'flt_f2', 'flt_w3', 'hy_skip', 'q_norm_g', 'k_norm_g', 'sink',
                 'gm_ln_g', 'gm_ln_b', 'gm_ws', 'gm_bs', 'b_gate')


def kernel(x_prompt, x_sample, cache_k, cache_v, c, c_ctx, w_mod, b_mod, norm1_g, norm2_g,
           w_in, conv_w, conv_b, flt_w1, flt_b1, flt_f1, flt_w2, flt_b2, flt_f2, flt_w3,
           hy_skip, q_norm_g, k_norm_g, sink, gm_ln_g, gm_ln_b, gm_ws, gm_bs,
           w_p_hy, w_p_at, w_p_gm, w_gate, b_gate, w_out, w_up, w_down):
    args = dict(locals())
    batch, seq, d = x_prompt.shape
    dec_batch, dec_seq, _ = x_sample.shape
    depth = w_mod.shape[0]
    assert 1 + dec_batch <= PAD_ROWS

    c_rows = jnp.concatenate(
        [c_ctx[None, :], c, jnp.zeros((PAD_ROWS - 1 - dec_batch, d), F32)], axis=0)
    mods = _modulation(c_rows, w_mod, b_mod).reshape(depth, PAD_ROWS, 1, 6 * d)

    tables = {L: _dft_tables(L) for L in (seq, dec_seq)}
    cache = (cache_k.reshape(dec_batch, depth, -1, KV_W), cache_v.reshape(dec_batch, depth, -1, KV_W))

    y_p = x_prompt.reshape(batch * seq, d)
    y_s = x_sample.reshape(dec_batch * dec_seq, d)
    ks, vs = [], []
    ctx_seg = (0, batch * seq)
    lat_seg = (1, dec_seq)
    w_in_b = (w_in[:1].astype(BF16), 0)
    w_p_hy_b = w_p_hy.astype(BF16)
    w_p_gm_b = w_p_gm.astype(BF16)
    for l in range(depth):
        p = {name: args[name][l] for name in _LAYER_PARAMS}
        m_l = mods[l]
        h_c = _normmod(y_p, p['norm1_g'], m_l, *ctx_seg, sh_chunk=0, sc_chunk=1)
        h_s = _normmod(y_s, p['norm1_g'], m_l, *lat_seg, sh_chunk=0, sc_chunk=1)
        proj_c, w_gate_b = _matmul(h_c, w_in_b, _mm_kernel, F32, "in_proj", cast=(w_gate, l))
        proj_s, w_p_at_b = _matmul(h_s, w_in_b, _mm_kernel, F32, "in_proj", cast=(w_p_at, l))
        gates_c, w_up_b = _matmul(h_c, (w_gate_b, 0), _mm_sigmoid_kernel, BF16, "gate_proj",
                                  bias=p['b_gate'], cast=(w_up, l))
        gates_s, w_out_b = _matmul(h_s, (w_gate_b, 0), _mm_sigmoid_kernel, BF16, "gate_proj",
                                   bias=p['b_gate'], cast=(w_out, l))
        br_c, k_l, v_l = _mixer_branches(proj_c, seq, p, tables[seq], None, l)
        br_s, _, _ = _mixer_branches(proj_s, dec_seq, p, tables[dec_seq], cache, l)
        ks.append(k_l.reshape(batch, seq, N_KV_HEADS, HEAD_DIM))
        vs.append(v_l.reshape(batch, seq, N_KV_HEADS, HEAD_DIM))
        w_branch = ((w_p_hy_b, l), (w_p_at_b, 0), (w_p_gm_b, l))
        merged_c = _merge(*br_c, *w_branch, gates_c)
        merged_s = _merge(*br_s, *w_branch, gates_s)
        y_p, = _matmul_residual(merged_c, (w_out_b, 0), y_p, m_l, *ctx_seg, 2, "out_proj")
        y_s, = _matmul_residual(merged_s, (w_out_b, 0), y_s, m_l, *lat_seg, 2, "out_proj")
        h_c = _normmod(y_p, p['norm2_g'], m_l, *ctx_seg, sh_chunk=3, sc_chunk=4)
        h_s = _normmod(y_s, p['norm2_g'], m_l, *lat_seg, sh_chunk=3, sc_chunk=4)
        act_c, w_down_b = _matmul(h_c, (w_up_b, 0), _mm_relu2_kernel, BF16, "mlp_up", cast=(w_down, l))
        act_s, = _matmul(h_s, (w_up_b, 0), _mm_relu2_kernel, BF16, "mlp_up")
        if l + 1 < depth:
            y_p, w_in_next = _matmul_residual(act_c, (w_down_b, 0), y_p, m_l, *ctx_seg, 5, "mlp_down",
                                              cast=(w_in, l + 1))
            w_in_b = (w_in_next, 0)
        else:
            y_p, = _matmul_residual(act_c, (w_down_b, 0), y_p, m_l, *ctx_seg, 5, "mlp_down")
        y_s, = _matmul_residual(act_s, (w_down_b, 0), y_s, m_l, *lat_seg, 5, "mlp_down")
    return (y_p.reshape(batch, seq, d), y_s.reshape(dec_batch, dec_seq, d),
            jnp.stack(ks, axis=1), jnp.stack(vs, axis=1))
```

```python
import functools
import math

import jax
import jax.numpy as jnp
import numpy as np
from jax import lax
from jax.experimental import pallas as pl
from jax.experimental.pallas import tpu as pltpu

F32 = jnp.float32
BF16 = jnp.bfloat16

D_MODEL = 4096
HEAD_DIM = 128
N_Q_HEADS = 16
N_KV_HEADS = 4
GQA_GROUP = 4
ATT_W = N_Q_HEADS * HEAD_DIM
KV_W = N_KV_HEADS * HEAD_DIM
GRID_W = 64
WINDOW = 128
BLOCK = 128
ROPE_BASE = 10000.0
ROT_FREQS = 32
D_H = 1024
HY_ORDER = 2
POS_BANDS = 16
POS_DIM = 1 + 2 * POS_BANDS
FILT_HID = 64
DECAY_MIN = math.log(1e-2) / 1.5
DECAY_MAX = math.log(1e-2) / 0.3
D_G = 1024
CHUNK = 128
N_GM_GROUPS = 8
D_FF = 4 * D_MODEL
IN_COLS = 3 * D_H + ATT_W + 2 * KV_W + 2 * D_G
EPS = 1e-6
NEG = -1e30
ATT_SCALE = HEAD_DIM ** -0.5

COL_HY = 0
COL_Q = 3 * D_H
COL_K = COL_Q + ATT_W
COL_V = COL_K + KV_W
COL_GM = COL_V + KV_W

V7X_LANES = 128
V7X_VMEM_BYTES = 64 * 1024 * 1024
VMEM_LIMIT = 60 * 1024 * 1024

TM = 1024
TN = 1024
TN_KSPLIT = 512
TN_MERGE = 1024
TK = 4096
NORM_ROWS = 512
HY_TC = 256
HY_TILE_ELEMS = 256 * 1024
HY_SEQS = 2
GM_ROWS = 512
MOD_TN = 512
PAD_ROWS = 8
BF16_SUBLANES = 16


def _params(sem):
    return pltpu.CompilerParams(dimension_semantics=sem, vmem_limit_bytes=VMEM_LIMIT)


def _split(a):
    hi = a.astype(BF16)
    lo = (a - hi.astype(F32)).astype(BF16)
    return hi, lo


def _dot(a, b):
    return jnp.dot(a, b, preferred_element_type=F32)


def _dot3(a_hi, a_lo, b_hi, b_lo):
    return _dot(a_hi, b_hi) + _dot(a_lo, b_hi) + _dot(a_hi, b_lo)


def _mod_kernel(c_ref, w_ref, b_ref, o_ref):
    c = c_ref[...]
    x = c * jax.nn.sigmoid(c)
    x_hi, x_lo = _split(x)
    w_hi, w_lo = _split(w_ref[...])
    o_ref[...] = _dot3(x_hi, x_lo, w_hi, w_lo) + b_ref[...]


def _modulation(c_rows, w_mod, b_mod):
    depth, d, n = w_mod.shape
    return pl.pallas_call(
        _mod_kernel,
        grid=(depth, n // MOD_TN),
        in_specs=[
            pl.BlockSpec((PAD_ROWS, d), lambda l, j: (0, 0)),
            pl.BlockSpec((None, d, MOD_TN), lambda l, j: (l, 0, j)),
            pl.BlockSpec((None, 1, MOD_TN), lambda l, j: (l, 0, j)),
        ],
        out_specs=pl.BlockSpec((None, PAD_ROWS, MOD_TN), lambda l, j: (l, 0, j)),
        out_shape=jax.ShapeDtypeStruct((depth, PAD_ROWS, n), F32),
        compiler_params=_params(("arbitrary", "arbitrary")),
        name="modulation",
    )(c_rows, w_mod, b_mod.reshape(depth, 1, n))


def _normmod_kernel(x_ref, g_ref, sc_ref, sh_ref, o_ref):
    x = x_ref[...]
    y = x * lax.rsqrt(jnp.mean(x * x, axis=-1, keepdims=True) + EPS) * g_ref[...]
    o_ref[...] = (y * (1.0 + sc_ref[...]) + sh_ref[...]).astype(o_ref.dtype)


def _normmod(x, g, mods, seg0, rows_per_seg, sh_chunk, sc_chunk):
    m, d = x.shape
    tiles_per_seg = rows_per_seg // NORM_ROWS
    seg = lambda i: seg0 + i // tiles_per_seg
    return pl.pallas_call(
        _normmod_kernel,
        grid=(m // NORM_ROWS,),
        in_specs=[
            pl.BlockSpec((NORM_ROWS, d), lambda i: (i, 0)),
            pl.BlockSpec((1, d), lambda i: (0, 0)),
            pl.BlockSpec((None, 1, d), lambda i: (seg(i), 0, sc_chunk)),
            pl.BlockSpec((None, 1, d), lambda i: (seg(i), 0, sh_chunk)),
        ],
        out_specs=pl.BlockSpec((NORM_ROWS, d), lambda i: (i, 0)),
        out_shape=jax.ShapeDtypeStruct((m, d), BF16),
        compiler_params=_params(("arbitrary",)),
        name="normmod",
    )(x, g.reshape(1, d), mods, mods)


def _mm_kernel(x_ref, w_ref, o_ref):
    o_ref[...] = _dot(x_ref[...], w_ref[...]).astype(o_ref.dtype)


def _mm_sigmoid_kernel(x_ref, w_ref, b_ref, o_ref):
    t = _dot(x_ref[...], w_ref[...]) + b_ref[...]
    o_ref[...] = (0.5 * jnp.tanh(0.5 * t) + 0.5).astype(o_ref.dtype)


def _mm_relu2_kernel(x_ref, w_ref, o_ref):
    a = jnp.maximum(_dot(x_ref[...], w_ref[...]), 0.0)
    o_ref[...] = (a * a).astype(o_ref.dtype)


def _call(body, grid, in_specs, out_specs, out_shape, args, name, scratch_shapes=(), cast=None):
    sem = ("arbitrary",) * len(grid)
    if cast is None:
        out = pl.pallas_call(body, grid=grid, in_specs=in_specs, out_specs=out_specs, out_shape=out_shape,
                             scratch_shapes=scratch_shapes, compiler_params=_params(sem), name=name)(*args)
        return tuple(out)
    src, layer = cast
    _, k2, n2 = src.shape
    steps = math.prod(grid)
    slabs = min(1 << (steps.bit_length() - 1), k2 // BF16_SUBLANES)
    rows = k2 // slabs

    def slab(*ids):
        step = ids[0]
        for extent, idx in zip(grid[1:], ids[1:]):
            step = step * extent + idx
        return jnp.minimum(step, slabs - 1)

    n_in, n_out = len(in_specs), len(out_specs)

    def kernel(*refs):
        src_ref, dst_ref = refs[n_in], refs[n_in + 1 + n_out]
        body(*refs[:n_in], *refs[n_in + 1:n_in + 1 + n_out], *refs[n_in + 2 + n_out:])
        dst_ref[...] = src_ref[...].astype(BF16)

    out = pl.pallas_call(
        kernel, grid=grid,
        in_specs=[*in_specs, pl.BlockSpec((None, rows, n2), lambda *ids: (layer, slab(*ids), 0))],
        out_specs=[*out_specs, pl.BlockSpec((None, rows, n2), lambda *ids: (0, slab(*ids), 0))],
        out_shape=[*out_shape, jax.ShapeDtypeStruct((1, k2, n2), BF16)],
        scratch_shapes=scratch_shapes, compiler_params=_params(sem), name=name)(*args, src)
    return tuple(out)


def _matmul(x, w, kernel, out_dtype, name, bias=None, cast=None):
    w_arr, layer = w
    m, k = x.shape
    n = w_arr.shape[2]
    in_specs = [
        pl.BlockSpec((TM, k), lambda i, j: (i, 0)),
        pl.BlockSpec((None, k, TN), lambda i, j: (layer, 0, j)),
    ]
    args = [x, w_arr]
    if bias is not None:
        in_specs.append(pl.BlockSpec((1, TN), lambda i, j: (0, j)))
        args.append(bias.reshape(1, n))
    return _call(kernel, (m // TM, n // TN), in_specs,
                 [pl.BlockSpec((TM, TN), lambda i, j: (i, j))],
                 [jax.ShapeDtypeStruct((m, n), out_dtype)], args, name, cast=cast)


def _mm_residual_kernel(a_ref, w_ref, x_ref, g_ref, o_ref):
    o_ref[...] = x_ref[...] + g_ref[...] * _dot(a_ref[...], w_ref[...])


def _mm_residual_ksplit_kernel(a_ref, w_ref, x_ref, g_ref, o_ref, acc_ref, *, nk):
    kk = pl.program_id(1)
    j = pl.program_id(2)

    @pl.when(kk == 0)
    def _():
        acc_ref[j] = _dot(a_ref[...], w_ref[...])

    @pl.when((kk > 0) & (kk < nk - 1))
    def _():
        acc_ref[j] += _dot(a_ref[...], w_ref[...])

    @pl.when(kk == nk - 1)
    def _():
        o_ref[...] = x_ref[...] + g_ref[...] * (acc_ref[j] + _dot(a_ref[...], w_ref[...]))


def _matmul_residual(a, w, x, mods, seg0, rows_per_seg, g_chunk, name, cast=None):
    w_arr, layer = w
    m, k = a.shape
    n = w_arr.shape[2]
    tiles_per_seg = rows_per_seg // TM
    seg = lambda i: seg0 + i // tiles_per_seg
    if k <= TK:
        nb = n // TN
        return _call(
            _mm_residual_kernel, (m // TM, nb),
            [
                pl.BlockSpec((TM, k), lambda i, j: (i, 0)),
                pl.BlockSpec((None, k, TN), lambda i, j: (layer, 0, j)),
                pl.BlockSpec((TM, TN), lambda i, j: (i, j)),
                pl.BlockSpec((None, 1, TN), lambda i, j: (seg(i), 0, g_chunk * nb + j)),
            ],
            [pl.BlockSpec((TM, TN), lambda i, j: (i, j))],
            [jax.ShapeDtypeStruct((m, n), F32)], (a, w_arr, x, mods), name, cast=cast)
    nk = k // TK
    tn = TN_KSPLIT
    nb = n // tn
    out_col = lambda kk, j: jnp.where(kk == nk - 1, j, 0)
    return _call(
        functools.partial(_mm_residual_ksplit_kernel, nk=nk), (m // TM, nk, nb),
        [
            pl.BlockSpec((TM, TK), lambda i, kk, j: (i, kk)),
            pl.BlockSpec((None, TK, tn), lambda i, kk, j: (layer, kk, j)),
            pl.BlockSpec((TM, tn), lambda i, kk, j: (i, out_col(kk, j))),
            pl.BlockSpec((None, 1, tn), lambda i, kk, j: (seg(i), 0, g_chunk * nb + out_col(kk, j))),
        ],
        [pl.BlockSpec((TM, tn), lambda i, kk, j: (i, out_col(kk, j)))],
        [jax.ShapeDtypeStruct((m, n), F32)], (a, w_arr, x, mods), name,
        scratch_shapes=[pltpu.VMEM((nb, TM, tn), F32)], cast=cast)


def _merge_kernel(hy_ref, at_ref, gm_ref, whv_ref, wat_ref, wgm_ref, ga_ref, gb_ref, gc_ref, o_ref):
    merged = (ga_ref[...] * _dot(hy_ref[...], whv_ref[...])
              + gb_ref[...] * _dot(at_ref[...], wat_ref[...])
              + gc_ref[...] * _dot(gm_ref[...], wgm_ref[...]))
    o_ref[...] = merged.astype(o_ref.dtype)


def _merge(hy, at, gm, w_hy, w_at, w_gm, gates):
    m = hy.shape[0]
    n = w_hy[0].shape[2]
    tn = TN_MERGE
    nb = n // tn
    row = lambda width: pl.BlockSpec((TM, width), lambda i, j: (i, 0))
    col = lambda w: pl.BlockSpec((None, w[0].shape[1], tn), lambda i, j: (w[1], 0, j))
    gate = lambda b: pl.BlockSpec((TM, tn), lambda i, j: (i, b * nb + j))
    return _call(
        _merge_kernel, (m // TM, nb),
        [row(hy.shape[1]), row(at.shape[1]), row(gm.shape[1]), col(w_hy), col(w_at), col(w_gm),
         gate(0), gate(1), gate(2)],
        [pl.BlockSpec((TM, tn), lambda i, j: (i, j))],
        [jax.ShapeDtypeStruct((m, n), BF16)],
        (hy, at, gm, w_hy[0], w_at[0], w_gm[0], gates, gates, gates), "merge")[0]


def _dft_tables(L):
    k = np.arange(L)
    ang = ((k[:, None] * k[None, :]) % (2 * L)) * (np.pi / L)
    cos, sin = np.cos(ang), np.sin(ang)
    const = lambda a: jnp.asarray(a.astype(np.float32).astype(BF16))
    return {
        'cos': const(cos),
        'sin': const(sin),
        'fwd': const(np.concatenate([cos, -sin], axis=0)),
        'inv': const(np.concatenate([cos, -sin], axis=1)),
    }


def _filter_features(L):
    t = np.linspace(0.0, 1.0, L)
    w = 2.0 * np.pi * np.arange(L) / L
    bands = np.linspace(1e-4, POS_BANDS - 1, POS_BANDS)
    z = np.concatenate([t[:, None], np.cos(w[:, None] * bands), -np.sin(w[:, None] * bands)], axis=-1)
    z = np.pad(z, ((0, 0), (0, V7X_LANES - POS_DIM)))
    return jnp.asarray(z.astype(np.float32)), jnp.asarray(t[:, None].astype(np.float32))


def _filter_kernel(z_ref, t_ref, w1_ref, b1_ref, f1_ref, w2_ref, b2_ref, f2_ref,
                   w3f_ref, w3b_ref, dl_ref, cos_ref, sin_ref, o_ref, *, L):
    z_hi, z_lo = _split(z_ref[...])
    a = jnp.sin(f1_ref[...] * (_dot3(z_hi, z_lo, *_split(w1_ref[...])) + b1_ref[...]))
    a = jnp.sin(f2_ref[...] * (_dot3(*_split(a), *_split(w2_ref[...])) + b2_ref[...]))
    a_hi, a_lo = _split(a)
    decay = jnp.exp(-t_ref[...] * dl_ref[...])
    row = lax.broadcasted_iota(jnp.int32, (L, 1), 0)
    alt = jnp.where(row % 2 == 0, 1.0, -1.0).astype(F32)
    for o in range(HY_ORDER):
        fwd = _dot3(a_hi, a_lo, *_split(w3f_ref[o])) * decay
        bwd = _dot3(a_hi, a_lo, *_split(w3b_ref[o])) * decay
        s = fwd + jnp.where(row == 0, 0.0, bwd)
        d = bwd - fwd
        hr = _dot(cos_ref[...], s.astype(BF16)) * (1.0 / L)
        hi = _dot(sin_ref[...], d.astype(BF16)) * (1.0 / L)
        nyq = jnp.sum(alt * s, axis=0, keepdims=True) * (0.5 / L)
        o_ref[o, 0] = jnp.where(row == 0, 0.5 * hr, hr)
        o_ref[o, 1] = hi
        o_ref[o, 2] = jnp.where(row == 0, nyq, hr)


def _filter_spectra(L, p, tables):
    z, t = _filter_features(L)
    w1 = jnp.pad(p['flt_w1'], ((0, V7X_LANES - POS_DIM), (0, 0)))
    w3 = p['flt_w3'].reshape(FILT_HID, HY_ORDER, 2, D_H)
    w3f = jnp.transpose(w3[:, :, 0], (1, 0, 2))
    w3b = jnp.transpose(w3[:, :, 1], (1, 0, 2))
    deltas = jnp.asarray(np.abs(np.linspace(DECAY_MIN, DECAY_MAX, D_H)).astype(np.float32).reshape(1, D_H))
    full = lambda shape: pl.BlockSpec(shape, lambda c: (0,) * len(shape))
    return pl.pallas_call(
        functools.partial(_filter_kernel, L=L),
        grid=(D_H // HY_TC,),
        in_specs=[
            full((L, V7X_LANES)), full((L, 1)),
            full((V7X_LANES, FILT_HID)), full((1, FILT_HID)), full((1, FILT_HID)),
            full((FILT_HID, FILT_HID)), full((1, FILT_HID)), full((1, FILT_HID)),
            pl.BlockSpec((HY_ORDER, FILT_HID, HY_TC), lambda c: (0, 0, c)),
            pl.BlockSpec((HY_ORDER, FILT_HID, HY_TC), lambda c: (0, 0, c)),
            pl.BlockSpec((1, HY_TC), lambda c: (0, c)),
            full((L, L)), full((L, L)),
        ],
        out_specs=pl.BlockSpec((HY_ORDER, 3, L, HY_TC), lambda c: (0, 0, 0, c)),
        out_shape=jax.ShapeDtypeStruct((HY_ORDER, 3, L, D_H), F32),
        compiler_params=_params(("arbitrary",)),
        name=f"hyena_filter_{L}",
    )(z, t, w1, p['flt_b1'].reshape(1, -1), p['flt_f1'].reshape(1, -1),
      p['flt_w2'], p['flt_b2'].reshape(1, -1), p['flt_f2'].reshape(1, -1),
      w3f, w3b, deltas, tables['cos'], tables['sin'])


def _hyena_kernel(v_ref, x1_ref, x2_ref, wv_ref, wx1_ref, wx2_ref, bv_ref, bx1_ref, bx2_ref,
                  tab_ref, skip_ref, fwd_ref, inv_ref, o_ref, *, L):
    row = lax.broadcasted_iota(jnp.int32, (L, 1), 0)
    alt = jnp.where(row % 2 == 0, 1.0, -1.0).astype(F32)

    def short_conv(u_ref, rows, w_ref, b_ref):
        u = u_ref[rows, :]
        prev = jnp.where(row == 0, 0.0, pltpu.roll(u, 1, 0))
        nxt = jnp.where(row == L - 1, 0.0, pltpu.roll(u, L - 1, 0))
        return prev * w_ref[0:1, :] + u * w_ref[1:2, :] + nxt * w_ref[2:3, :] + b_ref[...]

    seqs = [slice(s * L, (s + 1) * L) for s in range(v_ref.shape[0] // L)]
    zs = [short_conv(v_ref, rows, wv_ref, bv_ref) for rows in seqs]
    gates = [(short_conv(x1_ref, rows, wx1_ref, bx1_ref), short_conv(x2_ref, rows, wx2_ref, bx2_ref))
             for rows in seqs]
    for o in range(HY_ORDER):
        p, q, r = tab_ref[o, 0], tab_ref[o, 1], tab_ref[o, 2]
        specs = [_dot(fwd_ref[...], z.astype(BF16)) for z in zs]
        prods = []
        for z, spec in zip(zs, specs):
            re = spec[:L]
            nyq = jnp.sum(alt * z, axis=0, keepdims=True)
            im = jnp.where(row == 0, nyq, spec[L:])
            prods.append((re * p - im * q, re * q + im * r))
        ys = [_dot(inv_ref[...], jnp.concatenate([ya, yb], axis=0).astype(BF16)) + alt * yb[0:1, :]
              for ya, yb in prods]
        zs = [g[o] * (y + skip_ref[o:o + 1, :] * z) for g, y, z in zip(gates, ys, zs)]
    for rows, z in zip(seqs, zs):
        o_ref[rows, :] = z.astype(o_ref.dtype)


def _hyena(proj, L, spectra, tables, p):
    m = proj.shape[0]
    tc = min(D_H, HY_TILE_ELEMS // L)
    nc = D_H // tc
    rows = (HY_SEQS if tc < D_H else 1) * L
    u = lambda part: pl.BlockSpec((rows, tc), lambda c, b: (b, part * nc + c))
    cw = lambda part: pl.BlockSpec((3, tc), lambda c, b: (0, part * nc + c))
    cb = lambda part: pl.BlockSpec((1, tc), lambda c, b: (0, part * nc + c))
    return pl.pallas_call(
        functools.partial(_hyena_kernel, L=L),
        grid=(nc, m // rows),
        in_specs=[u(0), u(1), u(2), cw(0), cw(1), cw(2), cb(0), cb(1), cb(2),
                  pl.BlockSpec((HY_ORDER, 3, L, tc), lambda c, b: (0, 0, 0, c)),
                  pl.BlockSpec((HY_ORDER, tc), lambda c, b: (0, c)),
                  pl.BlockSpec((2 * L, L), lambda c, b: (0, 0)),
                  pl.BlockSpec((L, 2 * L), lambda c, b: (0, 0))],
        out_specs=pl.BlockSpec((rows, tc), lambda c, b: (b, c)),
        out_shape=jax.ShapeDtypeStruct((m, D_H), BF16),
        compiler_params=_params(("arbitrary", "arbitrary")),
        name=f"hyena_{L}",
    )(proj, proj, proj, p['conv_w'], p['conv_w'], p['conv_w'],
      p['conv_b'].reshape(1, -1), p['conv_b'].reshape(1, -1), p['conv_b'].reshape(1, -1),
      spectra, p['hy_skip'], tables['fwd'], tables['inv'])


def _gmlp_kernel(u_ref, v_ref, lg_ref, lb_ref, ws_ref, bs_ref, o_ref):
    v = v_ref[...]
    mu = jnp.mean(v, axis=-1, keepdims=True)
    vc = v - mu
    var = jnp.mean(vc * vc, axis=-1, keepdims=True)
    vn = (vc * lax.rsqrt(var + EPS) * lg_ref[...] + lb_ref[...]).astype(BF16)
    for n in range(GM_ROWS // CHUNK):
        rows = slice(n * CHUNK, (n + 1) * CHUNK)
        for g in range(N_GM_GROUPS):
            cols = slice(g * CHUNK, (g + 1) * CHUNK)
            mixed = _dot(ws_ref[g], vn[rows, cols]) + bs_ref[:, g:g + 1]
            o_ref[rows, cols] = (u_ref[rows, cols] * mixed).astype(o_ref.dtype)


def _gmlp(proj, p):
    m = proj.shape[0]
    cu = COL_GM // D_G
    return pl.pallas_call(
        _gmlp_kernel,
        grid=(m // GM_ROWS,),
        in_specs=[
            pl.BlockSpec((GM_ROWS, D_G), lambda i: (i, cu)),
            pl.BlockSpec((GM_ROWS, D_G), lambda i: (i, cu + 1)),
            pl.BlockSpec((1, D_G), lambda i: (0, 0)),
            pl.BlockSpec((1, D_G), lambda i: (0, 0)),
            pl.BlockSpec((N_GM_GROUPS, CHUNK, CHUNK), lambda i: (0, 0, 0)),
            pl.BlockSpec((CHUNK, N_GM_GROUPS), lambda i: (0, 0)),
        ],
        out_specs=pl.BlockSpec((GM_ROWS, D_G), lambda i: (i, 0)),
        out_shape=jax.ShapeDtypeStruct((m, D_G), BF16),
        compiler_params=_params(("arbitrary",)),
        name="gmlp",
    )(proj, proj, p['gm_ln_g'].reshape(1, -1), p['gm_ln_b'].reshape(1, -1),
      p['gm_ws'].astype(BF16), p['gm_bs'].T)


def _rms(x, g):
    return x * lax.rsqrt(jnp.mean(x * x, axis=-1, keepdims=True) + EPS) * g


def _rope(x, cos, sin_signed, lane):
    partner = jnp.where((lane % 64) < ROT_FREQS,
                        pltpu.roll(x, HEAD_DIM - ROT_FREQS, 1), pltpu.roll(x, ROT_FREQS, 1))
    return x * cos + partner * sin_signed


def _softmax_pv(q_all, k_all, v_all, mask, sink_col):
    s = lax.dot_general(q_all, k_all, (((1,), (1,)), ((), ())), preferred_element_type=F32) * ATT_SCALE
    if mask is not None:
        s = jnp.where(mask, s, NEG)
    m = jnp.maximum(jnp.max(s, axis=-1, keepdims=True), sink_col)
    e = jnp.exp(s - m)
    den = jnp.sum(e, axis=-1, keepdims=True) + jnp.exp(sink_col - m)
    return _dot(e.astype(BF16), v_all) / den


def _sink_column(sink_ref, rows):
    return jnp.concatenate(
        [jnp.broadcast_to(sink_ref[g:g + 1, 0:1], (rows, 1)) for g in range(GQA_GROUP)], axis=0)


def _ctx_att_kernel(qa_ref, qb_ref, k_ref, v_ref, qg_ref, kg_ref, sink_ref, o_ref, ko_ref, vo_ref, *, L):
    qw = GQA_GROUP * HEAD_DIM
    q_halves = (qa_ref, qb_ref)
    for h in range(N_KV_HEADS):
        q_ref = q_halves[h // 2]
        q0 = (h % 2) * qw
        q_all = jnp.concatenate(
            [_rms(q_ref[:, q0 + g * HEAD_DIM:q0 + (g + 1) * HEAD_DIM], qg_ref[...]) for g in range(GQA_GROUP)],
            axis=0)
        cols = slice(h * HEAD_DIM, (h + 1) * HEAD_DIM)
        k = _rms(k_ref[:, cols], kg_ref[...])
        v = v_ref[:, cols]
        ko_ref[:, cols] = k
        vo_ref[:, cols] = v
        out = _softmax_pv(q_all.astype(BF16), k.astype(BF16), v.astype(BF16), None,
                          _sink_column(sink_ref.at[h], L))
        for g in range(GQA_GROUP):
            o_ref[:, h * qw + g * HEAD_DIM:h * qw + (g + 1) * HEAD_DIM] = out[g * L:(g + 1) * L].astype(o_ref.dtype)


def _sink_rows(sink):
    s = sink.reshape(N_KV_HEADS, GQA_GROUP, 1)
    s = jnp.pad(s, ((0, 0), (0, PAD_ROWS - GQA_GROUP), (0, 0)))
    return jnp.broadcast_to(s, (N_KV_HEADS, PAD_ROWS, V7X_LANES))


def _context_attention(proj, L, p):
    m = proj.shape[0]
    b = m // L
    half = ATT_W // 2
    kv_shape = jax.ShapeDtypeStruct((b, L, KV_W), F32)
    return pl.pallas_call(
        functools.partial(_ctx_att_kernel, L=L),
        grid=(b,),
        in_specs=[
            pl.BlockSpec((L, half), lambda i: (i, COL_Q // half)),
            pl.BlockSpec((L, half), lambda i: (i, COL_Q // half + 1)),
            pl.BlockSpec((L, KV_W), lambda i: (i, COL_K // KV_W)),
            pl.BlockSpec((L, KV_W), lambda i: (i, COL_V // KV_W)),
            pl.BlockSpec((1, HEAD_DIM), lambda i: (0, 0)),
            pl.BlockSpec((1, HEAD_DIM), lambda i: (0, 0)),
            pl.BlockSpec((N_KV_HEADS, PAD_ROWS, V7X_LANES), lambda i: (0, 0, 0)),
        ],
        out_specs=[
            pl.BlockSpec((L, ATT_W), lambda i: (i, 0)),
            pl.BlockSpec((None, L, KV_W), lambda i: (i, 0, 0)),
            pl.BlockSpec((None, L, KV_W), lambda i: (i, 0, 0)),
        ],
        out_shape=[jax.ShapeDtypeStruct((m, ATT_W), BF16), kv_shape, kv_shape],
        compiler_params=_params(("arbitrary",)),
        name="context_attention",
    )(proj, proj, proj, proj, p['q_norm_g'].reshape(1, -1), p['k_norm_g'].reshape(1, -1),
      _sink_rows(p['sink']))


def _rope_tables(L):
    rows = L // GRID_W
    row = np.repeat(np.arange(rows), GRID_W)
    col = np.tile(np.arange(GRID_W), rows)
    inv = ROPE_BASE ** (-np.arange(ROT_FREQS) / ROT_FREQS)
    pos = np.stack([row, col], axis=-1).astype(np.float64)
    ang = pos[:, :, None] * inv
    cos, sin = np.cos(ang), np.sin(ang)
    cos_t = np.stack([cos, cos], axis=2).reshape(L, HEAD_DIM)
    sin_t = np.stack([-sin, sin], axis=2).reshape(L, HEAD_DIM)
    return jnp.asarray(cos_t.astype(np.float32)), jnp.asarray(sin_t.astype(np.float32))


def _lat_att_kernel(q_ref, k_ref, v_ref, kc_ref, vc_ref, cos_ref, sin_ref, qg_ref, kg_ref, sink_ref,
                    o_ref, *, nb):
    lane = lax.broadcasted_iota(jnp.int32, (1, HEAD_DIM), 1)
    k_seq = _rope(_rms(k_ref[...], kg_ref[...]), cos_ref[...], sin_ref[...], lane).astype(BF16)
    v_seq = v_ref[...].astype(BF16)
    k_ctx = kc_ref[...].astype(BF16)
    v_ctx = vc_ref[...].astype(BF16)
    n_ctx = k_ctx.shape[0]
    sink_col = _sink_column(sink_ref, BLOCK)

    def band_mask(has_prev, has_next):
        n_keys = n_ctx + (1 + has_prev + has_next) * BLOCK
        shape = (GQA_GROUP * BLOCK, n_keys)
        i = lax.broadcasted_iota(jnp.int32, shape, 0) % BLOCK
        c = lax.broadcasted_iota(jnp.int32, shape, 1)
        ok = None
        if has_prev:
            ok = (c < n_ctx) | (c >= n_ctx + BLOCK) | (c - n_ctx >= i)
        if has_next:
            start = n_keys - BLOCK
            nxt = (c < start) | (c - start <= i)
            ok = nxt if ok is None else ok & nxt
        return ok

    masks = {}
    for n in range(nb):
        lo, hi = max(n - 1, 0), min(n + 1, nb - 1)
        shape_key = (lo < n, hi > n)
        if shape_key not in masks:
            masks[shape_key] = band_mask(*shape_key)
        rows = slice(n * BLOCK, (n + 1) * BLOCK)
        cos, sin = cos_ref[rows, :], sin_ref[rows, :]
        q_all = jnp.concatenate(
            [_rope(_rms(q_ref[rows, g * HEAD_DIM:(g + 1) * HEAD_DIM], qg_ref[...]), cos, sin, lane)
             for g in range(GQA_GROUP)], axis=0).astype(BF16)
        band = slice(lo * BLOCK, (hi + 1) * BLOCK)
        k_all = jnp.concatenate([k_ctx, k_seq[band]], axis=0)
        v_all = jnp.concatenate([v_ctx, v_seq[band]], axis=0)
        out = _softmax_pv(q_all, k_all, v_all, masks[shape_key], sink_col)
        for g in range(GQA_GROUP):
            o_ref[rows, g * HEAD_DIM:(g + 1) * HEAD_DIM] = out[g * BLOCK:(g + 1) * BLOCK].astype(o_ref.dtype)


def _window_attention(proj, L, cache_k, cache_v, layer, p):
    m = proj.shape[0]
    b = m // L
    nb = L // BLOCK
    n_ctx = cache_k.shape[2]
    qw = GQA_GROUP * HEAD_DIM
    cos_t, sin_t = _rope_tables(L)
    seq = lambda col0: pl.BlockSpec((L, HEAD_DIM), lambda i, h: (i, col0 // HEAD_DIM + h))
    cache = pl.BlockSpec((None, None, n_ctx, HEAD_DIM), lambda i, h: (i, layer, 0, h))
    table = pl.BlockSpec((L, HEAD_DIM), lambda i, h: (0, 0))
    gain = pl.BlockSpec((1, HEAD_DIM), lambda i, h: (0, 0))
    return pl.pallas_call(
        functools.partial(_lat_att_kernel, nb=nb),
        grid=(b, N_KV_HEADS),
        in_specs=[
            pl.BlockSpec((L, qw), lambda i, h: (i, COL_Q // qw + h)),
            seq(COL_K), seq(COL_V), cache, cache, table, table, gain, gain,
            pl.BlockSpec((None, PAD_ROWS, V7X_LANES), lambda i, h: (h, 0, 0)),
        ],
        out_specs=pl.BlockSpec((L, qw), lambda i, h: (i, h)),
        out_shape=jax.ShapeDtypeStruct((m, ATT_W), BF16),
        compiler_params=_params(("arbitrary", "arbitrary")),
        name="window_attention",
    )(proj, proj, proj, cache_k, cache_v, cos_t, sin_t,
      p['q_norm_g'].reshape(1, -1), p['k_norm_g'].reshape(1, -1), _sink_rows(p['sink']))


def _mixer_branches(proj, L, p, tables, cache, layer):
    hy = _hyena(proj, L, _filter_spectra(L, p, tables), tables, p)
    gm = _gmlp(proj, p)
    if cache is None:
        att, k_new, v_new = _context_attention(proj, L, p)
    else:
        att = _window_attention(proj, L, cache[0], cache[1], layer, p)
        k_new = v_new = None
    return (hy, att, gm), k_new, v_new


_LAYER_PARAMS = ('norm1_g', 'norm2_g', 'conv_w', 'conv_b', 'flt_w1', 'flt_b1', 'flt_f1', 'flt_w2',
                 'flt_b2', 'flt_f2', 'flt_w3', 'hy_skip', 'q_norm_g', 'k_norm_g', 'sink',
                 'gm_ln_g', 'gm_ln_b', 'gm_ws', 'gm_bs', 'b_gate')


def kernel(x_prompt, x_sample, cache_k, cache_v, c, c_ctx, w_mod, b_mod, norm1_g, norm2_g,
           w_in, conv_w, conv_b, flt_w1, flt_b1, flt_f1, flt_w2, flt_b2, flt_f2, flt_w3,
           hy_skip, q_norm_g, k_norm_g, sink, gm_ln_g, gm_ln_b, gm_ws, gm_bs,
           w_p_hy, w_p_at, w_p_gm, w_gate, b_gate, w_out, w_up, w_down):
    args = dict(locals())
    batch, seq, d = x_prompt.shape
    dec_batch, dec_seq, _ = x_sample.shape
    depth = w_mod.shape[0]
    assert 1 + dec_batch <= PAD_ROWS

    c_rows = jnp.concatenate(
        [c_ctx[None, :], c, jnp.zeros((PAD_ROWS - 1 - dec_batch, d), F32)], axis=0)
    mods = _modulation(c_rows, w_mod, b_mod).reshape(depth, PAD_ROWS, 1, 6 * d)

    tables = {L: _dft_tables(L) for L in (seq, dec_seq)}
    cache = (cache_k.reshape(dec_batch, depth, -1, KV_W), cache_v.reshape(dec_batch, depth, -1, KV_W))

    y_p = x_prompt.reshape(batch * seq, d)
    y_s = x_sample.reshape(dec_batch * dec_seq, d)
    ks, vs = [], []
    ctx_seg = (0, batch * seq)
    lat_seg = (1, dec_seq)
    w_in_b = (w_in[:1].astype(BF16), 0)
    w_p_hy_b = w_p_hy.astype(BF16)
    w_p_gm_b = w_p_gm.astype(BF16)
    for l in range(depth):
        p = {name: args[name][l] for name in _LAYER_PARAMS}
        m_l = mods[l]
        h_c = _normmod(y_p, p['norm1_g'], m_l, *ctx_seg, sh_chunk=0, sc_chunk=1)
        h_s = _normmod(y_s, p['norm1_g'], m_l, *lat_seg, sh_chunk=0, sc_chunk=1)
        proj_c, w_gate_b = _matmul(h_c, w_in_b, _mm_kernel, F32, "in_proj", cast=(w_gate, l))
        proj_s, w_p_at_b = _matmul(h_s, w_in_b, _mm_kernel, F32, "in_proj", cast=(w_p_at, l))
        gates_c, w_up_b = _matmul(h_c, (w_gate_b, 0), _mm_sigmoid_kernel, BF16, "gate_proj",
                                  bias=p['b_gate'], cast=(w_up, l))
        gates_s, w_out_b = _matmul(h_s, (w_gate_b, 0), _mm_sigmoid_kernel, BF16, "gate_proj",
                                   bias=p['b_gate'], cast=(w_out, l))
        br_c, k_l, v_l = _mixer_branches(proj_c, seq, p, tables[seq], None, l)
        br_s, _, _ = _mixer_branches(proj_s, dec_seq, p, tables[dec_seq], cache, l)
        ks.append(k_l.reshape(batch, seq, N_KV_HEADS, HEAD_DIM))
        vs.append(v_l.reshape(batch, seq, N_KV_HEADS, HEAD_DIM))
        w_branch = ((w_p_hy_b, l), (w_p_at_b, 0), (w_p_gm_b, l))
        merged_c = _merge(*br_c, *w_branch, gates_c)
        merged_s = _merge(*br_s, *w_branch, gates_s)
        y_p, = _matmul_residual(merged_c, (w_out_b, 0), y_p, m_l, *ctx_seg, 2, "out_proj")
        y_s, = _matmul_residual(merged_s, (w_out_b, 0), y_s, m_l, *lat_seg, 2, "out_proj")
        h_c = _normmod(y_p, p['norm2_g'], m_l, *ctx_seg, sh_chunk=3, sc_chunk=4)
        h_s = _normmod(y_s, p['norm2_g'], m_l, *lat_seg, sh_chunk=3, sc_chunk=4)
        act_c, w_down_b = _matmul(h_c, (w_up_b, 0), _mm_relu2_kernel, BF16, "mlp_up", cast=(w_down, l))
        act_s, = _matmul(h_s, (w_up_b, 0), _mm_relu2_kernel, BF16, "mlp_up")
        if l + 1 < depth:
            y_p, w_in_next = _matmul_residual(act_c, (w_down_b, 0), y_p, m_l, *ctx_seg, 5, "mlp_down",
                                              cast=(w_in, l + 1))
            w_in_b = (w_in_next, 0)
        else:
            y_p, = _matmul_residual(act_c, (w_down_b, 0), y_p, m_l, *ctx_seg, 5, "mlp_down")
        y_s, = _matmul_residual(act_s, (w_down_b, 0), y_s, m_l, *lat_seg, 5, "mlp_down")
    return (y_p.reshape(batch, seq, d), y_s.reshape(dec_batch, dec_seq, d),
            jnp.stack(ks, axis=1), jnp.stack(vs, axis=1))
```

```python
import functools
import math

import jax
import jax.numpy as jnp
import numpy as np
from jax import lax
from jax.experimental import pallas as pl
from jax.experimental.pallas import tpu as pltpu

F32 = jnp.float32
BF16 = jnp.bfloat16

D_MODEL = 4096
HEAD_DIM = 128
N_Q_HEADS = 16
N_KV_HEADS = 4
GQA_GROUP = 4
ATT_W = N_Q_HEADS * HEAD_DIM
KV_W = N_KV_HEADS * HEAD_DIM
GRID_W = 64
WINDOW = 128
BLOCK = 128
ROPE_BASE = 10000.0
ROT_FREQS = 32
D_H = 1024
HY_ORDER = 2
POS_BANDS = 16
POS_DIM = 1 + 2 * POS_BANDS
FILT_HID = 64
DECAY_MIN = math.log(1e-2) / 1.5
DECAY_MAX = math.log(1e-2) / 0.3
D_G = 1024
CHUNK = 128
N_GM_GROUPS = 8
D_FF = 4 * D_MODEL
IN_COLS = 3 * D_H + ATT_W + 2 * KV_W + 2 * D_G
EPS = 1e-6
NEG = -1e30
ATT_SCALE = HEAD_DIM ** -0.5

COL_HY = 0
COL_Q = 3 * D_H
COL_K = COL_Q + ATT_W
COL_V = COL_K + KV_W
COL_GM = COL_V + KV_W

V7X_LANES = 128
V7X_VMEM_BYTES = 64 * 1024 * 1024
VMEM_LIMIT = 60 * 1024 * 1024

TM = 1024
TN = 1024
TN_KSPLIT = 512
TN_MERGE = 1024
TK = 4096
NORM_ROWS = 512
HY_TC = 256
HY_TILE_ELEMS = 256 * 1024
HY_SEQS = 2
GM_ROWS = 512
MOD_TN = 512
PAD_ROWS = 8
BF16_SUBLANES = 16


def _params(sem):
    return pltpu.CompilerParams(dimension_semantics=sem, vmem_limit_bytes=VMEM_LIMIT)


def _split(a):
    hi = a.astype(BF16)
    lo = (a - hi.astype(F32)).astype(BF16)
    return hi, lo


def _dot(a, b):
    return jnp.dot(a, b, preferred_element_type=F32)


def _dot3(a_hi, a_lo, b_hi, b_lo):
    return _dot(a_hi, b_hi) + _dot(a_lo, b_hi) + _dot(a_hi, b_lo)


def _mod_kernel(c_ref, w_ref, b_ref, o_ref):
    c = c_ref[...]
    x = c * jax.nn.sigmoid(c)
    o_ref[...] = _dot(x.astype(BF16), w_ref[...].astype(BF16)) + b_ref[...]


def _modulation(c_rows, w_mod, b_mod):
    depth, d, n = w_mod.shape
    return pl.pallas_call(
        _mod_kernel,
        grid=(depth, n // MOD_TN),
        in_specs=[
            pl.BlockSpec((PAD_ROWS, d), lambda l, j: (0, 0)),
            pl.BlockSpec((None, d, MOD_TN), lambda l, j: (l, 0, j)),
            pl.BlockSpec((None, 1, MOD_TN), lambda l, j: (l, 0, j)),
        ],
        out_specs=pl.BlockSpec((None, PAD_ROWS, MOD_TN), lambda l, j: (l, 0, j)),
        out_shape=jax.ShapeDtypeStruct((depth, PAD_ROWS, n), F32),
        compiler_params=_params(("arbitrary", "arbitrary")),
        name="modulation",
    )(c_rows, w_mod, b_mod.reshape(depth, 1, n))


def _normmod_kernel(x_ref, g_ref, sc_ref, sh_ref, o_ref):
    x = x_ref[...]
    y = x * lax.rsqrt(jnp.mean(x * x, axis=-1, keepdims=True) + EPS) * g_ref[...]
    o_ref[...] = (y * (1.0 + sc_ref[...]) + sh_ref[...]).astype(o_ref.dtype)


def _normmod(x, g, mods, seg0, rows_per_seg, sh_chunk, sc_chunk):
    m, d = x.shape
    tiles_per_seg = rows_per_seg // NORM_ROWS
    seg = lambda i: seg0 + i // tiles_per_seg
    return pl.pallas_call(
        _normmod_kernel,
        grid=(m // NORM_ROWS,),
        in_specs=[
            pl.BlockSpec((NORM_ROWS, d), lambda i: (i, 0)),
            pl.BlockSpec((1, d), lambda i: (0, 0)),
            pl.BlockSpec((None, 1, d), lambda i: (seg(i), 0, sc_chunk)),
            pl.BlockSpec((None, 1, d), lambda i: (seg(i), 0, sh_chunk)),
        ],
        out_specs=pl.BlockSpec((NORM_ROWS, d), lambda i: (i, 0)),
        out_shape=jax.ShapeDtypeStruct((m, d), BF16),
        compiler_params=_params(("arbitrary",)),
        name="normmod",
    )(x, g.reshape(1, d), mods, mods)


def _mm_kernel(x_ref, w_ref, o_ref):
    o_ref[...] = _dot(x_ref[...], w_ref[...]).astype(o_ref.dtype)


def _mm_sigmoid_kernel(x_ref, w_ref, b_ref, o_ref):
    t = _dot(x_ref[...], w_ref[...]) + b_ref[...]
    o_ref[...] = (0.5 * jnp.tanh(0.5 * t) + 0.5).astype(o_ref.dtype)


def _mm_relu2_kernel(x_ref, w_ref, o_ref):
    a = jnp.maximum(_dot(x_ref[...], w_ref[...]), 0.0)
    o_ref[...] = (a * a).astype(o_ref.dtype)


def _call(body, grid, in_specs, out_specs, out_shape, args, name, scratch_shapes=(), cast=None):
    sem = ("arbitrary",) * len(grid)
    if cast is None:
        out = pl.pallas_call(body, grid=grid, in_specs=in_specs, out_specs=out_specs, out_shape=out_shape,
                             scratch_shapes=scratch_shapes, compiler_params=_params(sem), name=name)(*args)
        return tuple(out)
    src, layer = cast
    _, k2, n2 = src.shape
    steps = math.prod(grid)
    slabs = min(1 << (steps.bit_length() - 1), k2 // BF16_SUBLANES)
    rows = k2 // slabs

    def slab(*ids):
        step = ids[0]
        for extent, idx in zip(grid[1:], ids[1:]):
            step = step * extent + idx
        return jnp.minimum(step, slabs - 1)

    n_in, n_out = len(in_specs), len(out_specs)

    def kernel(*refs):
        src_ref, dst_ref = refs[n_in], refs[n_in + 1 + n_out]
        body(*refs[:n_in], *refs[n_in + 1:n_in + 1 + n_out], *refs[n_in + 2 + n_out:])
        dst_ref[...] = src_ref[...].astype(BF16)

    out = pl.pallas_call(
        kernel, grid=grid,
        in_specs=[*in_specs, pl.BlockSpec((None, rows, n2), lambda *ids: (layer, slab(*ids), 0))],
        out_specs=[*out_specs, pl.BlockSpec((None, rows, n2), lambda *ids: (0, slab(*ids), 0))],
        out_shape=[*out_shape, jax.ShapeDtypeStruct((1, k2, n2), BF16)],
        scratch_shapes=scratch_shapes, compiler_params=_params(sem), name=name)(*args, src)
    return tuple(out)


def _matmul(x, w, kernel, out_dtype, name, bias=None, cast=None):
    w_arr, layer = w
    m, k = x.shape
    n = w_arr.shape[2]
    in_specs = [
        pl.BlockSpec((TM, k), lambda i, j: (i, 0)),
        pl.BlockSpec((None, k, TN), lambda i, j: (layer, 0, j)),
    ]
    args = [x, w_arr]
    if bias is not None:
        in_specs.append(pl.BlockSpec((1, TN), lambda i, j: (0, j)))
        args.append(bias.reshape(1, n))
    return _call(kernel, (m // TM, n // TN), in_specs,
                 [pl.BlockSpec((TM, TN), lambda i, j: (i, j))],
                 [jax.ShapeDtypeStruct((m, n), out_dtype)], args, name, cast=cast)


def _mm_residual_kernel(a_ref, w_ref, x_ref, g_ref, o_ref):
    o_ref[...] = x_ref[...] + g_ref[...] * _dot(a_ref[...], w_ref[...])


def _mm_residual_ksplit_kernel(a_ref, w_ref, x_ref, g_ref, o_ref, acc_ref, *, nk):
    kk = pl.program_id(1)
    j = pl.program_id(2)

    @pl.when(kk == 0)
    def _():
        acc_ref[j] = _dot(a_ref[...], w_ref[...])

    @pl.when((kk > 0) & (kk < nk - 1))
    def _():
        acc_ref[j] += _dot(a_ref[...], w_ref[...])

    @pl.when(kk == nk - 1)
    def _():
        o_ref[...] = x_ref[...] + g_ref[...] * (acc_ref[j] + _dot(a_ref[...], w_ref[...]))


def _matmul_residual(a, w, x, mods, seg0, rows_per_seg, g_chunk, name, cast=None):
    w_arr, layer = w
    m, k = a.shape
    n = w_arr.shape[2]
    tiles_per_seg = rows_per_seg // TM
    seg = lambda i: seg0 + i // tiles_per_seg
    if k <= TK:
        nb = n // TN
        return _call(
            _mm_residual_kernel, (m // TM, nb),
            [
                pl.BlockSpec((TM, k), lambda i, j: (i, 0)),
                pl.BlockSpec((None, k, TN), lambda i, j: (layer, 0, j)),
                pl.BlockSpec((TM, TN), lambda i, j: (i, j)),
                pl.BlockSpec((None, 1, TN), lambda i, j: (seg(i), 0, g_chunk * nb + j)),
            ],
            [pl.BlockSpec((TM, TN), lambda i, j: (i, j))],
            [jax.ShapeDtypeStruct((m, n), F32)], (a, w_arr, x, mods), name, cast=cast)
    nk = k // TK
    tn = TN_KSPLIT
    nb = n // tn
    out_col = lambda kk, j: jnp.where(kk == nk - 1, j, 0)
    return _call(
        functools.partial(_mm_residual_ksplit_kernel, nk=nk), (m // TM, nk, nb),
        [
            pl.BlockSpec((TM, TK), lambda i, kk, j: (i, kk)),
            pl.BlockSpec((None, TK, tn), lambda i, kk, j: (layer, kk, j)),
            pl.BlockSpec((TM, tn), lambda i, kk, j: (i, out_col(kk, j))),
            pl.BlockSpec((None, 1, tn), lambda i, kk, j: (seg(i), 0, g_chunk * nb + out_col(kk, j))),
        ],
        [pl.BlockSpec((TM, tn), lambda i, kk, j: (i, out_col(kk, j)))],
        [jax.ShapeDtypeStruct((m, n), F32)], (a, w_arr, x, mods), name,
        scratch_shapes=[pltpu.VMEM((nb, TM, tn), F32)], cast=cast)


def _merge_kernel(hy_ref, at_ref, gm_ref, whv_ref, wat_ref, wgm_ref, ga_ref, gb_ref, gc_ref, o_ref):
    merged = (ga_ref[...] * _dot(hy_ref[...], whv_ref[...])
              + gb_ref[...] * _dot(at_ref[...], wat_ref[...])
              + gc_ref[...] * _dot(gm_ref[...], wgm_ref[...]))
    o_ref[...] = merged.astype(o_ref.dtype)


def _merge(hy, at, gm, w_hy, w_at, w_gm, gates):
    m = hy.shape[0]
    n = w_hy[0].shape[2]
    tn = TN_MERGE
    nb = n // tn
    row = lambda width: pl.BlockSpec((TM, width), lambda i, j: (i, 0))
    col = lambda w: pl.BlockSpec((None, w[0].shape[1], tn), lambda i, j: (w[1], 0, j))
    gate = lambda b: pl.BlockSpec((TM, tn), lambda i, j: (i, b * nb + j))
    return _call(
        _merge_kernel, (m // TM, nb),
        [row(hy.shape[1]), row(at.shape[1]), row(gm.shape[1]), col(w_hy), col(w_at), col(w_gm),
         gate(0), gate(1), gate(2)],
        [pl.BlockSpec((TM, tn), lambda i, j: (i, j))],
        [jax.ShapeDtypeStruct((m, n), BF16)],
        (hy, at, gm, w_hy[0], w_at[0], w_gm[0], gates, gates, gates), "merge")[0]


def _dft_tables(L):
    k = np.arange(L)
    ang = ((k[:, None] * k[None, :]) % (2 * L)) * (np.pi / L)
    cos, sin = np.cos(ang), np.sin(ang)
    const = lambda a: jnp.asarray(a.astype(np.float32).astype(BF16))
    return {
        'cos': const(cos),
        'sin': const(sin),
        'fwd': const(np.concatenate([cos, -sin], axis=0)),
        'inv': const(np.concatenate([cos, -sin], axis=1)),
    }


def _filter_features(L):
    t = np.linspace(0.0, 1.0, L)
    w = 2.0 * np.pi * np.arange(L) / L
    bands = np.linspace(1e-4, POS_BANDS - 1, POS_BANDS)
    z = np.concatenate([t[:, None], np.cos(w[:, None] * bands), -np.sin(w[:, None] * bands)], axis=-1)
    z = np.pad(z, ((0, 0), (0, V7X_LANES - POS_DIM)))
    return jnp.asarray(z.astype(np.float32)), jnp.asarray(t[:, None].astype(np.float32))


def _filter_kernel(z_ref, t_ref, w1_ref, b1_ref, f1_ref, w2_ref, b2_ref, f2_ref,
                   w3f_ref, w3b_ref, dl_ref, cos_ref, sin_ref, o_ref, *, L):
    z_hi, z_lo = _split(z_ref[...])
    a = jnp.sin(f1_ref[...] * (_dot3(z_hi, z_lo, *_split(w1_ref[...])) + b1_ref[...]))
    a = jnp.sin(f2_ref[...] * (_dot3(*_split(a), *_split(w2_ref[...])) + b2_ref[...]))
    a_hi, a_lo = _split(a)
    decay = jnp.exp(-t_ref[...] * dl_ref[...])
    row = lax.broadcasted_iota(jnp.int32, (L, 1), 0)
    alt = jnp.where(row % 2 == 0, 1.0, -1.0).astype(F32)
    for o in range(HY_ORDER):
        fwd = _dot3(a_hi, a_lo, *_split(w3f_ref[o])) * decay
        bwd = _dot3(a_hi, a_lo, *_split(w3b_ref[o])) * decay
        s = fwd + jnp.where(row == 0, 0.0, bwd)
        d = bwd - fwd
        hr = _dot(cos_ref[...], s.astype(BF16)) * (1.0 / L)
        hi = _dot(sin_ref[...], d.astype(BF16)) * (1.0 / L)
        nyq = jnp.sum(alt * s, axis=0, keepdims=True) * (0.5 / L)
        o_ref[o, 0] = jnp.where(row == 0, 0.5 * hr, hr)
        o_ref[o, 1] = hi
        o_ref[o, 2] = jnp.where(row == 0, nyq, hr)


def _filter_spectra(L, p, tables):
    z, t = _filter_features(L)
    w1 = jnp.pad(p['flt_w1'], ((0, V7X_LANES - POS_DIM), (0, 0)))
    w3 = p['flt_w3'].reshape(FILT_HID, HY_ORDER, 2, D_H)
    w3f = jnp.transpose(w3[:, :, 0], (1, 0, 2))
    w3b = jnp.transpose(w3[:, :, 1], (1, 0, 2))
    deltas = jnp.asarray(np.abs(np.linspace(DECAY_MIN, DECAY_MAX, D_H)).astype(np.float32).reshape(1, D_H))
    full = lambda shape: pl.BlockSpec(shape, lambda c: (0,) * len(shape))
    return pl.pallas_call(
        functools.partial(_filter_kernel, L=L),
        grid=(D_H // HY_TC,),
        in_specs=[
            full((L, V7X_LANES)), full((L, 1)),
            full((V7X_LANES, FILT_HID)), full((1, FILT_HID)), full((1, FILT_HID)),
            full((FILT_HID, FILT_HID)), full((1, FILT_HID)), full((1, FILT_HID)),
            pl.BlockSpec((HY_ORDER, FILT_HID, HY_TC), lambda c: (0, 0, c)),
            pl.BlockSpec((HY_ORDER, FILT_HID, HY_TC), lambda c: (0, 0, c)),
            pl.BlockSpec((1, HY_TC), lambda c: (0, c)),
            full((L, L)), full((L, L)),
        ],
        out_specs=pl.BlockSpec((HY_ORDER, 3, L, HY_TC), lambda c: (0, 0, 0, c)),
        out_shape=jax.ShapeDtypeStruct((HY_ORDER, 3, L, D_H), F32),
        compiler_params=_params(("arbitrary",)),
        name=f"hyena_filter_{L}",
    )(z, t, w1, p['flt_b1'].reshape(1, -1), p['flt_f1'].reshape(1, -1),
      p['flt_w2'], p['flt_b2'].reshape(1, -1), p['flt_f2'].reshape(1, -1),
      w3f, w3b, deltas, tables['cos'], tables['sin'])


def _hyena_kernel(v_ref, x1_ref, x2_ref, wv_ref, wx1_ref, wx2_ref, bv_ref, bx1_ref, bx2_ref,
                  tab_ref, skip_ref, fwd_ref, inv_ref, o_ref, *, L):
    row = lax.broadcasted_iota(jnp.int32, (L, 1), 0)
    alt = jnp.where(row % 2 == 0, 1.0, -1.0).astype(F32)

    def short_conv(u_ref, rows, w_ref, b_ref):
        u = u_ref[rows, :]
        prev = jnp.where(row == 0, 0.0, pltpu.roll(u, 1, 0))
        nxt = jnp.where(row == L - 1, 0.0, pltpu.roll(u, L - 1, 0))
        return prev * w_ref[0:1, :] + u * w_ref[1:2, :] + nxt * w_ref[2:3, :] + b_ref[...]

    seqs = [slice(s * L, (s + 1) * L) for s in range(v_ref.shape[0] // L)]
    zs = [short_conv(v_ref, rows, wv_ref, bv_ref) for rows in seqs]
    gates = [(short_conv(x1_ref, rows, wx1_ref, bx1_ref), short_conv(x2_ref, rows, wx2_ref, bx2_ref))
             for rows in seqs]
    for o in range(HY_ORDER):
        p, q, r = tab_ref[o, 0], tab_ref[o, 1], tab_ref[o, 2]
        specs = [_dot(fwd_ref[...], z.astype(BF16)) for z in zs]
        prods = []
        for z, spec in zip(zs, specs):
            re = spec[:L]
            nyq = jnp.sum(alt * z, axis=0, keepdims=True)
            im = jnp.where(row == 0, nyq, spec[L:])
            prods.append((re * p - im * q, re * q + im * r))
        ys = [_dot(inv_ref[...], jnp.concatenate([ya, yb], axis=0).astype(BF16)) + alt * yb[0:1, :]
              for ya, yb in prods]
        zs = [g[o] * (y + skip_ref[o:o + 1, :] * z) for g, y, z in zip(gates, ys, zs)]
    for rows, z in zip(seqs, zs):
        o_ref[rows, :] = z.astype(o_ref.dtype)


def _hyena(proj, L, spectra, tables, p):
    m = proj.shape[0]
    tc = min(D_H, HY_TILE_ELEMS // L)
    nc = D_H // tc
    rows = (HY_SEQS if tc < D_H else 1) * L
    u = lambda part: pl.BlockSpec((rows, tc), lambda c, b: (b, part * nc + c))
    cw = lambda part: pl.BlockSpec((3, tc), lambda c, b: (0, part * nc + c))
    cb = lambda part: pl.BlockSpec((1, tc), lambda c, b: (0, part * nc + c))
    return pl.pallas_call(
        functools.partial(_hyena_kernel, L=L),
        grid=(nc, m // rows),
        in_specs=[u(0), u(1), u(2), cw(0), cw(1), cw(2), cb(0), cb(1), cb(2),
                  pl.BlockSpec((HY_ORDER, 3, L, tc), lambda c, b: (0, 0, 0, c)),
                  pl.BlockSpec((HY_ORDER, tc), lambda c, b: (0, c)),
                  pl.BlockSpec((2 * L, L), lambda c, b: (0, 0)),
                  pl.BlockSpec((L, 2 * L), lambda c, b: (0, 0))],
        out_specs=pl.BlockSpec((rows, tc), lambda c, b: (b, c)),
        out_shape=jax.ShapeDtypeStruct((m, D_H), BF16),
        compiler_params=_params(("arbitrary", "arbitrary")),
        name=f"hyena_{L}",
    )(proj, proj, proj, p['conv_w'], p['conv_w'], p['conv_w'],
      p['conv_b'].reshape(1, -1), p['conv_b'].reshape(1, -1), p['conv_b'].reshape(1, -1),
      spectra, p['hy_skip'], tables['fwd'], tables['inv'])


def _gmlp_kernel(u_ref, v_ref, lg_ref, lb_ref, ws_ref, bs_ref, o_ref):
    v = v_ref[...]
    mu = jnp.mean(v, axis=-1, keepdims=True)
    vc = v - mu
    var = jnp.mean(vc * vc, axis=-1, keepdims=True)
    vn = (vc * lax.rsqrt(var + EPS) * lg_ref[...] + lb_ref[...]).astype(BF16)
    for n in range(GM_ROWS // CHUNK):
        rows = slice(n * CHUNK, (n + 1) * CHUNK)
        for g in range(N_GM_GROUPS):
            cols = slice(g * CHUNK, (g + 1) * CHUNK)
            mixed = _dot(ws_ref[g], vn[rows, cols]) + bs_ref[:, g:g + 1]
            o_ref[rows, cols] = (u_ref[rows, cols] * mixed).astype(o_ref.dtype)


def _gmlp(proj, p):
    m = proj.shape[0]
    cu = COL_GM // D_G
    return pl.pallas_call(
        _gmlp_kernel,
        grid=(m // GM_ROWS,),
        in_specs=[
            pl.BlockSpec((GM_ROWS, D_G), lambda i: (i, cu)),
            pl.BlockSpec((GM_ROWS, D_G), lambda i: (i, cu + 1)),
            pl.BlockSpec((1, D_G), lambda i: (0, 0)),
            pl.BlockSpec((1, D_G), lambda i: (0, 0)),
            pl.BlockSpec((N_GM_GROUPS, CHUNK, CHUNK), lambda i: (0, 0, 0)),
            pl.BlockSpec((CHUNK, N_GM_GROUPS), lambda i: (0, 0)),
        ],
        out_specs=pl.BlockSpec((GM_ROWS, D_G), lambda i: (i, 0)),
        out_shape=jax.ShapeDtypeStruct((m, D_G), BF16),
        compiler_params=_params(("arbitrary",)),
        name="gmlp",
    )(proj, proj, p['gm_ln_g'].reshape(1, -1), p['gm_ln_b'].reshape(1, -1),
      p['gm_ws'].astype(BF16), p['gm_bs'].T)


def _rms(x, g):
    return x * lax.rsqrt(jnp.mean(x * x, axis=-1, keepdims=True) + EPS) * g


def _rope(x, cos, sin_signed, lane):
    partner = jnp.where((lane % 64) < ROT_FREQS,
                        pltpu.roll(x, HEAD_DIM - ROT_FREQS, 1), pltpu.roll(x, ROT_FREQS, 1))
    return x * cos + partner * sin_signed


def _softmax_pv(q_all, k_all, v_all, mask, sink_col):
    s = lax.dot_general(q_all, k_all, (((1,), (1,)), ((), ())), preferred_element_type=F32) * ATT_SCALE
    if mask is not None:
        s = jnp.where(mask, s, NEG)
    m = jnp.maximum(jnp.max(s, axis=-1, keepdims=True), sink_col)
    e = jnp.exp(s - m)
    den = jnp.sum(e, axis=-1, keepdims=True) + jnp.exp(sink_col - m)
    return _dot(e.astype(BF16), v_all) / den


def _sink_column(sink_ref, rows):
    return jnp.concatenate(
        [jnp.broadcast_to(sink_ref[g:g + 1, 0:1], (rows, 1)) for g in range(GQA_GROUP)], axis=0)


def _ctx_att_kernel(qa_ref, qb_ref, k_ref, v_ref, qg_ref, kg_ref, sink_ref, o_ref, ko_ref, vo_ref, *, L):
    qw = GQA_GROUP * HEAD_DIM
    q_halves = (qa_ref, qb_ref)
    for h in range(N_KV_HEADS):
        q_ref = q_halves[h // 2]
        q0 = (h % 2) * qw
        q_all = jnp.concatenate(
            [_rms(q_ref[:, q0 + g * HEAD_DIM:q0 + (g + 1) * HEAD_DIM], qg_ref[...]) for g in range(GQA_GROUP)],
            axis=0)
        cols = slice(h * HEAD_DIM, (h + 1) * HEAD_DIM)
        k = _rms(k_ref[:, cols], kg_ref[...])
        v = v_ref[:, cols]
        ko_ref[:, h, :] = k
        vo_ref[:, h, :] = v
        out = _softmax_pv(q_all.astype(BF16), k.astype(BF16), v.astype(BF16), None,
                          _sink_column(sink_ref.at[h], L))
        for g in range(GQA_GROUP):
            o_ref[:, h * qw + g * HEAD_DIM:h * qw + (g + 1) * HEAD_DIM] = out[g * L:(g + 1) * L].astype(o_ref.dtype)


def _sink_rows(sink):
    s = sink.reshape(N_KV_HEADS, GQA_GROUP, 1)
    s = jnp.pad(s, ((0, 0), (0, PAD_ROWS - GQA_GROUP), (0, 0)))
    return jnp.broadcast_to(s, (N_KV_HEADS, PAD_ROWS, V7X_LANES))


def _context_attention(proj, L, p, layer, depth, kv_prev):
    m = proj.shape[0]
    b = m // L
    half = ATT_W // 2
    kv_shape = jax.ShapeDtypeStruct((b, depth, L, N_KV_HEADS, HEAD_DIM), F32)
    kv_spec = pl.BlockSpec((None, None, L, N_KV_HEADS, HEAD_DIM), lambda i: (i, layer, 0, 0, 0))
    in_specs = [
        pl.BlockSpec((L, half), lambda i: (i, COL_Q // half)),
        pl.BlockSpec((L, half), lambda i: (i, COL_Q // half + 1)),
        pl.BlockSpec((L, KV_W), lambda i: (i, COL_K // KV_W)),
        pl.BlockSpec((L, KV_W), lambda i: (i, COL_V // KV_W)),
        pl.BlockSpec((1, HEAD_DIM), lambda i: (0, 0)),
        pl.BlockSpec((1, HEAD_DIM), lambda i: (0, 0)),
        pl.BlockSpec((N_KV_HEADS, PAD_ROWS, V7X_LANES), lambda i: (0, 0, 0)),
    ]
    args = [proj, proj, proj, proj, p['q_norm_g'].reshape(1, -1), p['k_norm_g'].reshape(1, -1),
            _sink_rows(p['sink'])]
    kernel = functools.partial(_ctx_att_kernel, L=L)
    aliases = {}
    if kv_prev is not None:
        n_in = len(in_specs)
        in_specs += [pl.BlockSpec(memory_space=pl.ANY)] * 2
        args += list(kv_prev)
        aliases = {n_in: 1, n_in + 1: 2}
        inner = kernel
        kernel = lambda *refs: inner(*refs[:n_in], *refs[n_in + 2:])
    return pl.pallas_call(
        kernel,
        grid=(b,),
        in_specs=in_specs,
        out_specs=[pl.BlockSpec((L, ATT_W), lambda i: (i, 0)), kv_spec, kv_spec],
        out_shape=[jax.ShapeDtypeStruct((m, ATT_W), BF16), kv_shape, kv_shape],
        input_output_aliases=aliases,
        compiler_params=_params(("arbitrary",)),
        name="context_attention",
    )(*args)


def _rope_tables(L):
    rows = L // GRID_W
    row = np.repeat(np.arange(rows), GRID_W)
    col = np.tile(np.arange(GRID_W), rows)
    inv = ROPE_BASE ** (-np.arange(ROT_FREQS) / ROT_FREQS)
    pos = np.stack([row, col], axis=-1).astype(np.float64)
    ang = pos[:, :, None] * inv
    cos, sin = np.cos(ang), np.sin(ang)
    cos_t = np.stack([cos, cos], axis=2).reshape(L, HEAD_DIM)
    sin_t = np.stack([-sin, sin], axis=2).reshape(L, HEAD_DIM)
    return jnp.asarray(cos_t.astype(np.float32)), jnp.asarray(sin_t.astype(np.float32))


def _lat_att_kernel(q_ref, k_ref, v_ref, kc_ref, vc_ref, cos_ref, sin_ref, qg_ref, kg_ref, sink_ref,
                    o_ref, *, nb):
    lane = lax.broadcasted_iota(jnp.int32, (1, HEAD_DIM), 1)
    k_seq = _rope(_rms(k_ref[...], kg_ref[...]), cos_ref[...], sin_ref[...], lane).astype(BF16)
    v_seq = v_ref[...].astype(BF16)
    k_ctx = kc_ref[...].astype(BF16)
    v_ctx = vc_ref[...].astype(BF16)
    n_ctx = k_ctx.shape[0]
    sink_col = _sink_column(sink_ref, BLOCK)

    def band_mask(has_prev, has_next):
        n_keys = n_ctx + (1 + has_prev + has_next) * BLOCK
        shape = (GQA_GROUP * BLOCK, n_keys)
        i = lax.broadcasted_iota(jnp.int32, shape, 0) % BLOCK
        c = lax.broadcasted_iota(jnp.int32, shape, 1)
        ok = None
        if has_prev:
            ok = (c < n_ctx) | (c >= n_ctx + BLOCK) | (c - n_ctx >= i)
        if has_next:
            start = n_keys - BLOCK
            nxt = (c < start) | (c - start <= i)
            ok = nxt if ok is None else ok & nxt
        return ok

    masks = {}
    for n in range(nb):
        lo, hi = max(n - 1, 0), min(n + 1, nb - 1)
        shape_key = (lo < n, hi > n)
        if shape_key not in masks:
            masks[shape_key] = band_mask(*shape_key)
        rows = slice(n * BLOCK, (n + 1) * BLOCK)
        cos, sin = cos_ref[rows, :], sin_ref[rows, :]
        q_all = jnp.concatenate(
            [_rope(_rms(q_ref[rows, g * HEAD_DIM:(g + 1) * HEAD_DIM], qg_ref[...]), cos, sin, lane)
             for g in range(GQA_GROUP)], axis=0).astype(BF16)
        band = slice(lo * BLOCK, (hi + 1) * BLOCK)
        k_all = jnp.concatenate([k_ctx, k_seq[band]], axis=0)
        v_all = jnp.concatenate([v_ctx, v_seq[band]], axis=0)
        out = _softmax_pv(q_all, k_all, v_all, masks[shape_key], sink_col)
        for g in range(GQA_GROUP):
            o_ref[rows, g * HEAD_DIM:(g + 1) * HEAD_DIM] = out[g * BLOCK:(g + 1) * BLOCK].astype(o_ref.dtype)


def _window_attention(proj, L, cache_k, cache_v, layer, p):
    m = proj.shape[0]
    b = m // L
    nb = L // BLOCK
    n_ctx = cache_k.shape[2]
    qw = GQA_GROUP * HEAD_DIM
    cos_t, sin_t = _rope_tables(L)
    seq = lambda col0: pl.BlockSpec((L, HEAD_DIM), lambda i, h: (i, col0 // HEAD_DIM + h))
    cache = pl.BlockSpec((None, None, n_ctx, HEAD_DIM), lambda i, h: (i, layer, 0, h))
    table = pl.BlockSpec((L, HEAD_DIM), lambda i, h: (0, 0))
    gain = pl.BlockSpec((1, HEAD_DIM), lambda i, h: (0, 0))
    return pl.pallas_call(
        functools.partial(_lat_att_kernel, nb=nb),
        grid=(b, N_KV_HEADS),
        in_specs=[
            pl.BlockSpec((L, qw), lambda i, h: (i, COL_Q // qw + h)),
            seq(COL_K), seq(COL_V), cache, cache, table, table, gain, gain,
            pl.BlockSpec((None, PAD_ROWS, V7X_LANES), lambda i, h: (h, 0, 0)),
        ],
        out_specs=pl.BlockSpec((L, qw), lambda i, h: (i, h)),
        out_shape=jax.ShapeDtypeStruct((m, ATT_W), BF16),
        compiler_params=_params(("arbitrary", "arbitrary")),
        name="window_attention",
    )(proj, proj, proj, cache_k, cache_v, cos_t, sin_t,
      p['q_norm_g'].reshape(1, -1), p['k_norm_g'].reshape(1, -1), _sink_rows(p['sink']))


def _mixer_branches(proj, L, p, tables, cache, layer, depth=None, kv_prev=None):
    hy = _hyena(proj, L, _filter_spectra(L, p, tables), tables, p)
    gm = _gmlp(proj, p)
    if cache is None:
        att, k_new, v_new = _context_attention(proj, L, p, layer, depth, kv_prev)
    else:
        att = _window_attention(proj, L, cache[0], cache[1], layer, p)
        k_new = v_new = None
    return (hy, att, gm), k_new, v_new


_LAYER_PARAMS = ('norm1_g', 'norm2_g', 'conv_w', 'conv_b', 'flt_w1', 'flt_b1', 'flt_f1', 'flt_w2',
                 'flt_b2', 'flt_f2', 'flt_w3', 'hy_skip', 'q_norm_g', 'k_norm_g', 'sink',
                 'gm_ln_g', 'gm_ln_b', 'gm_ws', 'gm_bs', 'b_gate')


def kernel(x_prompt, x_sample, cache_k, cache_v, c, c_ctx, w_mod, b_mod, norm1_g, norm2_g,
           w_in, conv_w, conv_b, flt_w1, flt_b1, flt_f1, flt_w2, flt_b2, flt_f2, flt_w3,
           hy_skip, q_norm_g, k_norm_g, sink, gm_ln_g, gm_ln_b, gm_ws, gm_bs,
           w_p_hy, w_p_at, w_p_gm, w_gate, b_gate, w_out, w_up, w_down):
    args = dict(locals())
    batch, seq, d = x_prompt.shape
    dec_batch, dec_seq, _ = x_sample.shape
    depth = w_mod.shape[0]
    assert 1 + dec_batch <= PAD_ROWS

    c_rows = jnp.concatenate(
        [c_ctx[None, :], c, jnp.zeros((PAD_ROWS - 1 - dec_batch, d), F32)], axis=0)
    mods = _modulation(c_rows, w_mod, b_mod).reshape(depth, PAD_ROWS, 1, 6 * d)

    tables = {L: _dft_tables(L) for L in (seq, dec_seq)}
    cache = (cache_k.reshape(dec_batch, depth, -1, KV_W), cache_v.reshape(dec_batch, depth, -1, KV_W))

    y_p = x_prompt.reshape(batch * seq, d)
    y_s = x_sample.reshape(dec_batch * dec_seq, d)
    new_kv = None
    ctx_seg = (0, batch * seq)
    lat_seg = (1, dec_seq)
    w_in_b = (w_in[:1].astype(BF16), 0)
    w_p_hy_b = w_p_hy.astype(BF16)
    w_p_gm_b = w_p_gm.astype(BF16)
    for l in range(depth):
        p = {name: args[name][l] for name in _LAYER_PARAMS}
        m_l = mods[l]
        h_c = _normmod(y_p, p['norm1_g'], m_l, *ctx_seg, sh_chunk=0, sc_chunk=1)
        h_s = _normmod(y_s, p['norm1_g'], m_l, *lat_seg, sh_chunk=0, sc_chunk=1)
        proj_c, w_gate_b = _matmul(h_c, w_in_b, _mm_kernel, F32, "in_proj", cast=(w_gate, l))
        proj_s, w_p_at_b = _matmul(h_s, w_in_b, _mm_kernel, F32, "in_proj", cast=(w_p_at, l))
        gates_c, w_up_b = _matmul(h_c, (w_gate_b, 0), _mm_sigmoid_kernel, BF16, "gate_proj",
                                  bias=p['b_gate'], cast=(w_up, l))
        gates_s, w_out_b = _matmul(h_s, (w_gate_b, 0), _mm_sigmoid_kernel, BF16, "gate_proj",
                                   bias=p['b_gate'], cast=(w_out, l))
        br_c, *new_kv = _mixer_branches(proj_c, seq, p, tables[seq], None, l, depth, new_kv)
        br_s, _, _ = _mixer_branches(proj_s, dec_seq, p, tables[dec_seq], cache, l)
        w_branch = ((w_p_hy_b, l), (w_p_at_b, 0), (w_p_gm_b, l))
        merged_c = _merge(*br_c, *w_branch, gates_c)
        merged_s = _merge(*br_s, *w_branch, gates_s)
        y_p, = _matmul_residual(merged_c, (w_out_b, 0), y_p, m_l, *ctx_seg, 2, "out_proj")
        y_s, = _matmul_residual(merged_s, (w_out_b, 0), y_s, m_l, *lat_seg, 2, "out_proj")
        h_c = _normmod(y_p, p['norm2_g'], m_l, *ctx_seg, sh_chunk=3, sc_chunk=4)
        h_s = _normmod(y_s, p['norm2_g'], m_l, *lat_seg, sh_chunk=3, sc_chunk=4)
        act_c, w_down_b = _matmul(h_c, (w_up_b, 0), _mm_relu2_kernel, BF16, "mlp_up", cast=(w_down, l))
        act_s, = _matmul(h_s, (w_up_b, 0), _mm_relu2_kernel, BF16, "mlp_up")
        if l + 1 < depth:
            y_p, w_in_next = _matmul_residual(act_c, (w_down_b, 0), y_p, m_l, *ctx_seg, 5, "mlp_down",
                                              cast=(w_in, l + 1))
            w_in_b = (w_in_next, 0)
        else:
            y_p, = _matmul_residual(act_c, (w_down_b, 0), y_p, m_l, *ctx_seg, 5, "mlp_down")
        y_s, = _matmul_residual(act_s, (w_down_b, 0), y_s, m_l, *lat_seg, 5, "mlp_down")
    return (y_p.reshape(batch, seq, d), y_s.reshape(dec_batch, dec_seq, d),
            new_kv[0], new_kv[1])
```

```python
import functools
import math

import jax
import jax.numpy as jnp
import numpy as np
from jax import lax
from jax.experimental import pallas as pl
from jax.experimental.pallas import tpu as pltpu

F32 = jnp.float32
BF16 = jnp.bfloat16

D_MODEL = 4096
HEAD_DIM = 128
N_Q_HEADS = 16
N_KV_HEADS = 4
GQA_GROUP = 4
ATT_W = N_Q_HEADS * HEAD_DIM
KV_W = N_KV_HEADS * HEAD_DIM
GRID_W = 64
WINDOW = 128
BLOCK = 128
ROPE_BASE = 10000.0
ROT_FREQS = 32
D_H = 1024
HY_ORDER = 2
POS_BANDS = 16
POS_DIM = 1 + 2 * POS_BANDS
FILT_HID = 64
DECAY_MIN = math.log(1e-2) / 1.5
DECAY_MAX = math.log(1e-2) / 0.3
D_G = 1024
CHUNK = 128
N_GM_GROUPS = 8
D_FF = 4 * D_MODEL
IN_COLS = 3 * D_H + ATT_W + 2 * KV_W + 2 * D_G
EPS = 1e-6
NEG = -1e30
ATT_SCALE = HEAD_DIM ** -0.5

COL_HY = 0
COL_Q = 3 * D_H
COL_K = COL_Q + ATT_W
COL_V = COL_K + KV_W
COL_GM = COL_V + KV_W

V7X_LANES = 128
V7X_VMEM_BYTES = 64 * 1024 * 1024
VMEM_LIMIT = 60 * 1024 * 1024

TM = 1024
TN = 1024
TN_KSPLIT = 512
TN_MERGE = 1024
TK = 4096
NORM_ROWS = 512
HY_TC = 256
HY_TILE_ELEMS = 256 * 1024
HY_SEQS = 2
ATT_ROWS = 512
GM_ROWS = 512
MOD_TN = 512
PAD_ROWS = 8
BF16_SUBLANES = 16


def _params(sem):
    return pltpu.CompilerParams(dimension_semantics=sem, vmem_limit_bytes=VMEM_LIMIT)


def _split(a):
    hi = a.astype(BF16)
    lo = (a - hi.astype(F32)).astype(BF16)
    return hi, lo


def _dot(a, b):
    return jnp.dot(a, b, preferred_element_type=F32)


def _dot3(a_hi, a_lo, b_hi, b_lo):
    return _dot(a_hi, b_hi) + _dot(a_lo, b_hi) + _dot(a_hi, b_lo)


def _mod_kernel(c_ref, w_ref, b_ref, o_ref):
    c = c_ref[...]
    x = c * jax.nn.sigmoid(c)
    o_ref[...] = _dot(x.astype(BF16), w_ref[...].astype(BF16)) + b_ref[...]


def _modulation(c_rows, w_mod, b_mod):
    depth, d, n = w_mod.shape
    return pl.pallas_call(
        _mod_kernel,
        grid=(depth, n // MOD_TN),
        in_specs=[
            pl.BlockSpec((PAD_ROWS, d), lambda l, j: (0, 0)),
            pl.BlockSpec((None, d, MOD_TN), lambda l, j: (l, 0, j)),
            pl.BlockSpec((None, 1, MOD_TN), lambda l, j: (l, 0, j)),
        ],
        out_specs=pl.BlockSpec((None, PAD_ROWS, MOD_TN), lambda l, j: (l, 0, j)),
        out_shape=jax.ShapeDtypeStruct((depth, PAD_ROWS, n), F32),
        compiler_params=_params(("arbitrary", "arbitrary")),
        name="modulation",
    )(c_rows, w_mod, b_mod.reshape(depth, 1, n))


def _normmod_kernel(x_ref, g_ref, sc_ref, sh_ref, o_ref):
    x = x_ref[...]
    y = x * lax.rsqrt(jnp.mean(x * x, axis=-1, keepdims=True) + EPS) * g_ref[...]
    o_ref[...] = (y * (1.0 + sc_ref[...]) + sh_ref[...]).astype(o_ref.dtype)


def _normmod(x, g, mods, seg0, rows_per_seg, sh_chunk, sc_chunk):
    m, d = x.shape
    tiles_per_seg = rows_per_seg // NORM_ROWS
    seg = lambda i: seg0 + i // tiles_per_seg
    return pl.pallas_call(
        _normmod_kernel,
        grid=(m // NORM_ROWS,),
        in_specs=[
            pl.BlockSpec((NORM_ROWS, d), lambda i: (i, 0)),
            pl.BlockSpec((1, d), lambda i: (0, 0)),
            pl.BlockSpec((None, 1, d), lambda i: (seg(i), 0, sc_chunk)),
            pl.BlockSpec((None, 1, d), lambda i: (seg(i), 0, sh_chunk)),
        ],
        out_specs=pl.BlockSpec((NORM_ROWS, d), lambda i: (i, 0)),
        out_shape=jax.ShapeDtypeStruct((m, d), BF16),
        compiler_params=_params(("arbitrary",)),
        name="normmod",
    )(x, g.reshape(1, d), mods, mods)


def _mm_kernel(x_ref, w_ref, o_ref):
    o_ref[...] = _dot(x_ref[...], w_ref[...]).astype(o_ref.dtype)


def _mm_sigmoid_kernel(x_ref, w_ref, b_ref, o_ref):
    t = _dot(x_ref[...], w_ref[...]) + b_ref[...]
    o_ref[...] = (0.5 * jnp.tanh(0.5 * t) + 0.5).astype(o_ref.dtype)


def _mm_relu2_kernel(x_ref, w_ref, o_ref):
    a = jnp.maximum(_dot(x_ref[...], w_ref[...]), 0.0)
    o_ref[...] = (a * a).astype(o_ref.dtype)


def _call(body, grid, in_specs, out_specs, out_shape, args, name, scratch_shapes=(), cast=None):
    sem = ("arbitrary",) * len(grid)
    if cast is None:
        out = pl.pallas_call(body, grid=grid, in_specs=in_specs, out_specs=out_specs, out_shape=out_shape,
                             scratch_shapes=scratch_shapes, compiler_params=_params(sem), name=name)(*args)
        return tuple(out)
    src, layer = cast
    _, k2, n2 = src.shape
    steps = math.prod(grid)
    slabs = min(1 << (steps.bit_length() - 1), k2 // BF16_SUBLANES)
    rows = k2 // slabs

    def slab(*ids):
        step = ids[0]
        for extent, idx in zip(grid[1:], ids[1:]):
            step = step * extent + idx
        return jnp.minimum(step, slabs - 1)

    n_in, n_out = len(in_specs), len(out_specs)

    def kernel(*refs):
        src_ref, dst_ref = refs[n_in], refs[n_in + 1 + n_out]
        body(*refs[:n_in], *refs[n_in + 1:n_in + 1 + n_out], *refs[n_in + 2 + n_out:])
        dst_ref[...] = src_ref[...].astype(BF16)

    out = pl.pallas_call(
        kernel, grid=grid,
        in_specs=[*in_specs, pl.BlockSpec((None, rows, n2), lambda *ids: (layer, slab(*ids), 0))],
        out_specs=[*out_specs, pl.BlockSpec((None, rows, n2), lambda *ids: (0, slab(*ids), 0))],
        out_shape=[*out_shape, jax.ShapeDtypeStruct((1, k2, n2), BF16)],
        scratch_shapes=scratch_shapes, compiler_params=_params(sem), name=name)(*args, src)
    return tuple(out)


def _matmul(x, w, kernel, out_dtype, name, bias=None, cast=None):
    w_arr, layer = w
    m, k = x.shape
    n = w_arr.shape[2]
    in_specs = [
        pl.BlockSpec((TM, k), lambda i, j: (i, 0)),
        pl.BlockSpec((None, k, TN), lambda i, j: (layer, 0, j)),
    ]
    args = [x, w_arr]
    if bias is not None:
        in_specs.append(pl.BlockSpec((1, TN), lambda i, j: (0, j)))
        args.append(bias.reshape(1, n))
    return _call(kernel, (m // TM, n // TN), in_specs,
                 [pl.BlockSpec((TM, TN), lambda i, j: (i, j))],
                 [jax.ShapeDtypeStruct((m, n), out_dtype)], args, name, cast=cast)


def _mm_residual_kernel(a_ref, w_ref, x_ref, g_ref, o_ref):
    o_ref[...] = x_ref[...] + g_ref[...] * _dot(a_ref[...], w_ref[...])


def _mm_residual_ksplit_kernel(a_ref, w_ref, x_ref, g_ref, o_ref, acc_ref, *, nk):
    kk = pl.program_id(1)
    j = pl.program_id(2)

    @pl.when(kk == 0)
    def _():
        acc_ref[j] = _dot(a_ref[...], w_ref[...])

    @pl.when((kk > 0) & (kk < nk - 1))
    def _():
        acc_ref[j] += _dot(a_ref[...], w_ref[...])

    @pl.when(kk == nk - 1)
    def _():
        o_ref[...] = x_ref[...] + g_ref[...] * (acc_ref[j] + _dot(a_ref[...], w_ref[...]))


def _matmul_residual(a, w, x, mods, seg0, rows_per_seg, g_chunk, name, cast=None):
    w_arr, layer = w
    m, k = a.shape
    n = w_arr.shape[2]
    tiles_per_seg = rows_per_seg // TM
    seg = lambda i: seg0 + i // tiles_per_seg
    if k <= TK:
        nb = n // TN
        return _call(
            _mm_residual_kernel, (m // TM, nb),
            [
                pl.BlockSpec((TM, k), lambda i, j: (i, 0)),
                pl.BlockSpec((None, k, TN), lambda i, j: (layer, 0, j)),
                pl.BlockSpec((TM, TN), lambda i, j: (i, j)),
                pl.BlockSpec((None, 1, TN), lambda i, j: (seg(i), 0, g_chunk * nb + j)),
            ],
            [pl.BlockSpec((TM, TN), lambda i, j: (i, j))],
            [jax.ShapeDtypeStruct((m, n), F32)], (a, w_arr, x, mods), name, cast=cast)
    nk = k // TK
    tn = TN_KSPLIT
    nb = n // tn
    out_col = lambda kk, j: jnp.where(kk == nk - 1, j, 0)
    return _call(
        functools.partial(_mm_residual_ksplit_kernel, nk=nk), (m // TM, nk, nb),
        [
            pl.BlockSpec((TM, TK), lambda i, kk, j: (i, kk)),
            pl.BlockSpec((None, TK, tn), lambda i, kk, j: (layer, kk, j)),
            pl.BlockSpec((TM, tn), lambda i, kk, j: (i, out_col(kk, j))),
            pl.BlockSpec((None, 1, tn), lambda i, kk, j: (seg(i), 0, g_chunk * nb + out_col(kk, j))),
        ],
        [pl.BlockSpec((TM, tn), lambda i, kk, j: (i, out_col(kk, j)))],
        [jax.ShapeDtypeStruct((m, n), F32)], (a, w_arr, x, mods), name,
        scratch_shapes=[pltpu.VMEM((nb, TM, tn), F32)], cast=cast)


def _merge_kernel(hy_ref, at_ref, gm_ref, whv_ref, wat_ref, wgm_ref, ga_ref, gb_ref, gc_ref, o_ref):
    merged = (ga_ref[...] * _dot(hy_ref[...], whv_ref[...])
              + gb_ref[...] * _dot(at_ref[...], wat_ref[...])
              + gc_ref[...] * _dot(gm_ref[...], wgm_ref[...]))
    o_ref[...] = merged.astype(o_ref.dtype)


def _merge(hy, at, gm, w_hy, w_at, w_gm, gates):
    m = hy.shape[0]
    n = w_hy[0].shape[2]
    tn = TN_MERGE
    nb = n // tn
    row = lambda width: pl.BlockSpec((TM, width), lambda i, j: (i, 0))
    col = lambda w: pl.BlockSpec((None, w[0].shape[1], tn), lambda i, j: (w[1], 0, j))
    gate = lambda b: pl.BlockSpec((TM, tn), lambda i, j: (i, b * nb + j))
    return _call(
        _merge_kernel, (m // TM, nb),
        [row(hy.shape[1]), row(at.shape[1]), row(gm.shape[1]), col(w_hy), col(w_at), col(w_gm),
         gate(0), gate(1), gate(2)],
        [pl.BlockSpec((TM, tn), lambda i, j: (i, j))],
        [jax.ShapeDtypeStruct((m, n), BF16)],
        (hy, at, gm, w_hy[0], w_at[0], w_gm[0], gates, gates, gates), "merge")[0]


def _dft_tables(L):
    k = np.arange(L)
    ang = ((k[:, None] * k[None, :]) % (2 * L)) * (np.pi / L)
    cos, sin = np.cos(ang), np.sin(ang)
    alt = np.where(k % 2 == 0, 1.0, -1.0)
    fwd = np.concatenate([cos, -sin], axis=0)
    inv = np.concatenate([cos, -sin], axis=1)
    fwd[L, :] = alt
    inv[:, L] = alt
    const = lambda a: jnp.asarray(a.astype(np.float32).astype(BF16))
    return {'cos': const(cos), 'sin': const(sin), 'fwd': const(fwd), 'inv': const(inv)}


def _filter_features(L):
    t = np.linspace(0.0, 1.0, L)
    w = 2.0 * np.pi * np.arange(L) / L
    bands = np.linspace(1e-4, POS_BANDS - 1, POS_BANDS)
    z = np.concatenate([t[:, None], np.cos(w[:, None] * bands), -np.sin(w[:, None] * bands)], axis=-1)
    z = np.pad(z, ((0, 0), (0, V7X_LANES - POS_DIM)))
    return jnp.asarray(z.astype(np.float32)), jnp.asarray(t[:, None].astype(np.float32))


def _filter_kernel(z_ref, t_ref, w1_ref, b1_ref, f1_ref, w2_ref, b2_ref, f2_ref,
                   w3f_ref, w3b_ref, dl_ref, cos_ref, sin_ref, o_ref, *, L):
    z_hi, z_lo = _split(z_ref[...])
    a = jnp.sin(f1_ref[...] * (_dot3(z_hi, z_lo, *_split(w1_ref[...])) + b1_ref[...]))
    a = jnp.sin(f2_ref[...] * (_dot3(*_split(a), *_split(w2_ref[...])) + b2_ref[...]))
    a_hi, a_lo = _split(a)
    decay = jnp.exp(-t_ref[...] * dl_ref[...])
    row = lax.broadcasted_iota(jnp.int32, (L, 1), 0)
    alt = jnp.where(row % 2 == 0, 1.0, -1.0).astype(F32)
    for o in range(HY_ORDER):
        fwd = _dot3(a_hi, a_lo, *_split(w3f_ref[o])) * decay
        bwd = _dot3(a_hi, a_lo, *_split(w3b_ref[o])) * decay
        s = fwd + jnp.where(row == 0, 0.0, bwd)
        d = bwd - fwd
        hr = _dot(cos_ref[...], s.astype(BF16)) * (1.0 / L)
        hi = _dot(sin_ref[...], d.astype(BF16)) * (1.0 / L)
        nyq = jnp.sum(alt * s, axis=0, keepdims=True) * (0.5 / L)
        o_ref[o, 0] = jnp.where(row == 0, 0.5 * hr, hr)
        o_ref[o, 1] = hi
        o_ref[o, 2] = jnp.where(row == 0, nyq, hr)


def _filter_spectra(L, p, tables):
    z, t = _filter_features(L)
    w1 = jnp.pad(p['flt_w1'], ((0, V7X_LANES - POS_DIM), (0, 0)))
    w3 = p['flt_w3'].reshape(FILT_HID, HY_ORDER, 2, D_H)
    w3f = jnp.transpose(w3[:, :, 0], (1, 0, 2))
    w3b = jnp.transpose(w3[:, :, 1], (1, 0, 2))
    deltas = jnp.asarray(np.abs(np.linspace(DECAY_MIN, DECAY_MAX, D_H)).astype(np.float32).reshape(1, D_H))
    full = lambda shape: pl.BlockSpec(shape, lambda c: (0,) * len(shape))
    return pl.pallas_call(
        functools.partial(_filter_kernel, L=L),
        grid=(D_H // HY_TC,),
        in_specs=[
            full((L, V7X_LANES)), full((L, 1)),
            full((V7X_LANES, FILT_HID)), full((1, FILT_HID)), full((1, FILT_HID)),
            full((FILT_HID, FILT_HID)), full((1, FILT_HID)), full((1, FILT_HID)),
            pl.BlockSpec((HY_ORDER, FILT_HID, HY_TC), lambda c: (0, 0, c)),
            pl.BlockSpec((HY_ORDER, FILT_HID, HY_TC), lambda c: (0, 0, c)),
            pl.BlockSpec((1, HY_TC), lambda c: (0, c)),
            full((L, L)), full((L, L)),
        ],
        out_specs=pl.BlockSpec((HY_ORDER, 3, L, HY_TC), lambda c: (0, 0, 0, c)),
        out_shape=jax.ShapeDtypeStruct((HY_ORDER, 3, L, D_H), F32),
        compiler_params=_params(("arbitrary",)),
        name=f"hyena_filter_{L}",
    )(z, t, w1, p['flt_b1'].reshape(1, -1), p['flt_f1'].reshape(1, -1),
      p['flt_w2'], p['flt_b2'].reshape(1, -1), p['flt_f2'].reshape(1, -1),
      w3f, w3b, deltas, tables['cos'], tables['sin'])


def _hyena_kernel(v_ref, x1_ref, x2_ref, wv_ref, wx1_ref, wx2_ref, bv_ref, bx1_ref, bx2_ref,
                  tab_ref, skip_ref, fwd_ref, inv_ref, o_ref, *, L):
    row = lax.broadcasted_iota(jnp.int32, (L, 1), 0)

    def short_conv(u_ref, rows, w_ref, b_ref):
        u = u_ref[rows, :]
        prev = jnp.where(row == 0, 0.0, pltpu.roll(u, 1, 0))
        nxt = jnp.where(row == L - 1, 0.0, pltpu.roll(u, L - 1, 0))
        return prev * w_ref[0:1, :] + u * w_ref[1:2, :] + nxt * w_ref[2:3, :] + b_ref[...]

    seqs = [slice(s * L, (s + 1) * L) for s in range(v_ref.shape[0] // L)]
    zs = [short_conv(v_ref, rows, wv_ref, bv_ref) for rows in seqs]
    gates = [(short_conv(x1_ref, rows, wx1_ref, bx1_ref), short_conv(x2_ref, rows, wx2_ref, bx2_ref))
             for rows in seqs]
    for o in range(HY_ORDER):
        p, q, r = tab_ref[o, 0], tab_ref[o, 1], tab_ref[o, 2]
        specs = [_dot(fwd_ref[...], z.astype(BF16)) for z in zs]
        prods = [(spec[:L] * p - spec[L:] * q, spec[:L] * q + spec[L:] * r) for spec in specs]
        ys = [_dot(inv_ref[...], jnp.concatenate([ya, yb], axis=0).astype(BF16)) for ya, yb in prods]
        zs = [g[o] * (y + skip_ref[o:o + 1, :] * z) for g, y, z in zip(gates, ys, zs)]
    for rows, z in zip(seqs, zs):
        o_ref[rows, :] = z.astype(o_ref.dtype)


def _hyena(proj, L, spectra, tables, p, cast):
    m = proj.shape[0]
    tc = min(D_H, HY_TILE_ELEMS // L)
    nc = D_H // tc
    rows = (HY_SEQS if tc < D_H else 1) * L
    u = lambda part: pl.BlockSpec((rows, tc), lambda c, b: (b, part * nc + c))
    cw = lambda part: pl.BlockSpec((3, tc), lambda c, b: (0, part * nc + c))
    cb = lambda part: pl.BlockSpec((1, tc), lambda c, b: (0, part * nc + c))
    conv_b = p['conv_b'].reshape(1, -1)
    return _call(
        functools.partial(_hyena_kernel, L=L), (nc, m // rows),
        [u(0), u(1), u(2), cw(0), cw(1), cw(2), cb(0), cb(1), cb(2),
         pl.BlockSpec((HY_ORDER, 3, L, tc), lambda c, b: (0, 0, 0, c)),
         pl.BlockSpec((HY_ORDER, tc), lambda c, b: (0, c)),
         pl.BlockSpec((2 * L, L), lambda c, b: (0, 0)),
         pl.BlockSpec((L, 2 * L), lambda c, b: (0, 0))],
        [pl.BlockSpec((rows, tc), lambda c, b: (b, c))],
        [jax.ShapeDtypeStruct((m, D_H), BF16)],
        (proj, proj, proj, p['conv_w'], p['conv_w'], p['conv_w'], conv_b, conv_b, conv_b,
         spectra, p['hy_skip'], tables['fwd'], tables['inv']),
        f"hyena_{L}", cast=cast)


def _gmlp_kernel(u_ref, v_ref, lg_ref, lb_ref, ws_ref, bs_ref, o_ref):
    v = v_ref[...]
    mu = jnp.mean(v, axis=-1, keepdims=True)
    vc = v - mu
    var = jnp.mean(vc * vc, axis=-1, keepdims=True)
    vn = (vc * lax.rsqrt(var + EPS) * lg_ref[...] + lb_ref[...]).astype(BF16)
    for n in range(GM_ROWS // CHUNK):
        rows = slice(n * CHUNK, (n + 1) * CHUNK)
        for g in range(N_GM_GROUPS):
            cols = slice(g * CHUNK, (g + 1) * CHUNK)
            mixed = _dot(ws_ref[g], vn[rows, cols]) + bs_ref[:, g:g + 1]
            o_ref[rows, cols] = (u_ref[rows, cols] * mixed).astype(o_ref.dtype)


def _gmlp(proj, p):
    m = proj.shape[0]
    cu = COL_GM // D_G
    return pl.pallas_call(
        _gmlp_kernel,
        grid=(m // GM_ROWS,),
        in_specs=[
            pl.BlockSpec((GM_ROWS, D_G), lambda i: (i, cu)),
            pl.BlockSpec((GM_ROWS, D_G), lambda i: (i, cu + 1)),
            pl.BlockSpec((1, D_G), lambda i: (0, 0)),
            pl.BlockSpec((1, D_G), lambda i: (0, 0)),
            pl.BlockSpec((N_GM_GROUPS, CHUNK, CHUNK), lambda i: (0, 0, 0)),
            pl.BlockSpec((CHUNK, N_GM_GROUPS), lambda i: (0, 0)),
        ],
        out_specs=pl.BlockSpec((GM_ROWS, D_G), lambda i: (i, 0)),
        out_shape=jax.ShapeDtypeStruct((m, D_G), BF16),
        compiler_params=_params(("arbitrary",)),
        name="gmlp",
    )(proj, proj, p['gm_ln_g'].reshape(1, -1), p['gm_ln_b'].reshape(1, -1),
      p['gm_ws'].astype(BF16), p['gm_bs'].T)


def _rms(x, g):
    return x * lax.rsqrt(jnp.mean(x * x, axis=-1, keepdims=True) + EPS) * g


def _rope(x, cos, sin_signed, lane):
    partner = jnp.where((lane % 64) < ROT_FREQS,
                        pltpu.roll(x, HEAD_DIM - ROT_FREQS, 1), pltpu.roll(x, ROT_FREQS, 1))
    return x * cos + partner * sin_signed


def _softmax_pv(q_all, k_all, v_all, mask, sink_col):
    outs = []
    for r in range(0, q_all.shape[0], ATT_ROWS):
        rows = slice(r, r + ATT_ROWS)
        s = lax.dot_general(q_all[rows], k_all, (((1,), (1,)), ((), ())),
                            preferred_element_type=F32) * ATT_SCALE
        if mask is not None:
            s = jnp.where(mask[rows], s, NEG)
        sink = sink_col[rows]
        m = jnp.maximum(jnp.max(s, axis=-1, keepdims=True), sink)
        e = jnp.exp(s - m)
        den = jnp.sum(e, axis=-1, keepdims=True) + jnp.exp(sink - m)
        outs.append(_dot(e.astype(BF16), v_all) / den)
    return jnp.concatenate(outs, axis=0)


def _sink_column(sink_ref, rows):
    return jnp.concatenate(
        [jnp.broadcast_to(sink_ref[g:g + 1, 0:1], (rows, 1)) for g in range(GQA_GROUP)], axis=0)


def _ctx_att_kernel(qa_ref, qb_ref, k_ref, v_ref, qg_ref, kg_ref, sink_ref, o_ref, ko_ref, vo_ref, *, L):
    qw = GQA_GROUP * HEAD_DIM
    q_halves = (qa_ref, qb_ref)
    for h in range(N_KV_HEADS):
        q_ref = q_halves[h // 2]
        q0 = (h % 2) * qw
        q_all = jnp.concatenate(
            [_rms(q_ref[:, q0 + g * HEAD_DIM:q0 + (g + 1) * HEAD_DIM], qg_ref[...]) for g in range(GQA_GROUP)],
            axis=0)
        cols = slice(h * HEAD_DIM, (h + 1) * HEAD_DIM)
        k = _rms(k_ref[:, cols], kg_ref[...])
        v = v_ref[:, cols]
        ko_ref[:, h, :] = k
        vo_ref[:, h, :] = v
        out = _softmax_pv(q_all.astype(BF16), k.astype(BF16), v.astype(BF16), None,
                          _sink_column(sink_ref.at[h], L))
        for g in range(GQA_GROUP):
            o_ref[:, h * qw + g * HEAD_DIM:h * qw + (g + 1) * HEAD_DIM] = out[g * L:(g + 1) * L].astype(o_ref.dtype)


def _sink_rows(sink):
    s = sink.reshape(N_KV_HEADS, GQA_GROUP, 1)
    s = jnp.pad(s, ((0, 0), (0, PAD_ROWS - GQA_GROUP), (0, 0)))
    return jnp.broadcast_to(s, (N_KV_HEADS, PAD_ROWS, V7X_LANES))


def _context_attention(proj, L, p, layer, depth, kv_prev):
    m = proj.shape[0]
    b = m // L
    half = ATT_W // 2
    kv_shape = jax.ShapeDtypeStruct((b, depth, L, N_KV_HEADS, HEAD_DIM), F32)
    kv_spec = pl.BlockSpec((None, None, L, N_KV_HEADS, HEAD_DIM), lambda i: (i, layer, 0, 0, 0))
    in_specs = [
        pl.BlockSpec((L, half), lambda i: (i, COL_Q // half)),
        pl.BlockSpec((L, half), lambda i: (i, COL_Q // half + 1)),
        pl.BlockSpec((L, KV_W), lambda i: (i, COL_K // KV_W)),
        pl.BlockSpec((L, KV_W), lambda i: (i, COL_V // KV_W)),
        pl.BlockSpec((1, HEAD_DIM), lambda i: (0, 0)),
        pl.BlockSpec((1, HEAD_DIM), lambda i: (0, 0)),
        pl.BlockSpec((N_KV_HEADS, PAD_ROWS, V7X_LANES), lambda i: (0, 0, 0)),
    ]
    args = [proj, proj, proj, proj, p['q_norm_g'].reshape(1, -1), p['k_norm_g'].reshape(1, -1),
            _sink_rows(p['sink'])]
    kernel = functools.partial(_ctx_att_kernel, L=L)
    aliases = {}
    if kv_prev is not None:
        n_in = len(in_specs)
        in_specs += [pl.BlockSpec(memory_space=pl.ANY)] * 2
        args += list(kv_prev)
        aliases = {n_in: 1, n_in + 1: 2}
        inner = kernel
        kernel = lambda *refs: inner(*refs[:n_in], *refs[n_in + 2:])
    return pl.pallas_call(
        kernel,
        grid=(b,),
        in_specs=in_specs,
        out_specs=[pl.BlockSpec((L, ATT_W), lambda i: (i, 0)), kv_spec, kv_spec],
        out_shape=[jax.ShapeDtypeStruct((m, ATT_W), BF16), kv_shape, kv_shape],
        input_output_aliases=aliases,
        compiler_params=_params(("arbitrary",)),
        name="context_attention",
    )(*args)


def _rope_tables(L):
    rows = L // GRID_W
    row = np.repeat(np.arange(rows), GRID_W)
    col = np.tile(np.arange(GRID_W), rows)
    inv = ROPE_BASE ** (-np.arange(ROT_FREQS) / ROT_FREQS)
    pos = np.stack([row, col], axis=-1).astype(np.float64)
    ang = pos[:, :, None] * inv
    cos, sin = np.cos(ang), np.sin(ang)
    cos_t = np.stack([cos, cos], axis=2).reshape(L, HEAD_DIM)
    sin_t = np.stack([-sin, sin], axis=2).reshape(L, HEAD_DIM)
    return jnp.asarray(cos_t.astype(np.float32)), jnp.asarray(sin_t.astype(np.float32))


def _lat_att_kernel(q_ref, k_ref, v_ref, kc_ref, vc_ref, cos_ref, sin_ref, qg_ref, kg_ref, sink_ref,
                    o_ref, *, nb):
    lane = lax.broadcasted_iota(jnp.int32, (1, HEAD_DIM), 1)
    k_seq = _rope(_rms(k_ref[...], kg_ref[...]), cos_ref[...], sin_ref[...], lane).astype(BF16)
    v_seq = v_ref[...].astype(BF16)
    k_ctx = kc_ref[...].astype(BF16)
    v_ctx = vc_ref[...].astype(BF16)
    n_ctx = k_ctx.shape[0]
    sink_col = _sink_column(sink_ref, BLOCK)

    def band_mask(has_prev, has_next):
        n_keys = n_ctx + (1 + has_prev + has_next) * BLOCK
        shape = (GQA_GROUP * BLOCK, n_keys)
        i = lax.broadcasted_iota(jnp.int32, shape, 0) % BLOCK
        c = lax.broadcasted_iota(jnp.int32, shape, 1)
        ok = None
        if has_prev:
            ok = (c < n_ctx) | (c >= n_ctx + BLOCK) | (c - n_ctx >= i)
        if has_next:
            start = n_keys - BLOCK
            nxt = (c < start) | (c - start <= i)
            ok = nxt if ok is None else ok & nxt
        return ok

    masks = {}
    for n in range(nb):
        lo, hi = max(n - 1, 0), min(n + 1, nb - 1)
        shape_key = (lo < n, hi > n)
        if shape_key not in masks:
            masks[shape_key] = band_mask(*shape_key)
        rows = slice(n * BLOCK, (n + 1) * BLOCK)
        cos, sin = cos_ref[rows, :], sin_ref[rows, :]
        q_all = jnp.concatenate(
            [_rope(_rms(q_ref[rows, g * HEAD_DIM:(g + 1) * HEAD_DIM], qg_ref[...]), cos, sin, lane)
             for g in range(GQA_GROUP)], axis=0).astype(BF16)
        band = slice(lo * BLOCK, (hi + 1) * BLOCK)
        k_all = jnp.concatenate([k_ctx, k_seq[band]], axis=0)
        v_all = jnp.concatenate([v_ctx, v_seq[band]], axis=0)
        out = _softmax_pv(q_all, k_all, v_all, masks[shape_key], sink_col)
        for g in range(GQA_GROUP):
            o_ref[rows, g * HEAD_DIM:(g + 1) * HEAD_DIM] = out[g * BLOCK:(g + 1) * BLOCK].astype(o_ref.dtype)


def _window_attention(proj, L, cache_k, cache_v, layer, p):
    m = proj.shape[0]
    b = m // L
    nb = L // BLOCK
    n_ctx = cache_k.shape[2]
    qw = GQA_GROUP * HEAD_DIM
    cos_t, sin_t = _rope_tables(L)
    seq = lambda col0: pl.BlockSpec((L, HEAD_DIM), lambda i, h: (i, col0 // HEAD_DIM + h))
    cache = pl.BlockSpec((None, None, n_ctx, HEAD_DIM), lambda i, h: (i, layer, 0, h))
    table = pl.BlockSpec((L, HEAD_DIM), lambda i, h: (0, 0))
    gain = pl.BlockSpec((1, HEAD_DIM), lambda i, h: (0, 0))
    return pl.pallas_call(
        functools.partial(_lat_att_kernel, nb=nb),
        grid=(b, N_KV_HEADS),
        in_specs=[
            pl.BlockSpec((L, qw), lambda i, h: (i, COL_Q // qw + h)),
            seq(COL_K), seq(COL_V), cache, cache, table, table, gain, gain,
            pl.BlockSpec((None, PAD_ROWS, V7X_LANES), lambda i, h: (h, 0, 0)),
        ],
        out_specs=pl.BlockSpec((L, qw), lambda i, h: (i, h)),
        out_shape=jax.ShapeDtypeStruct((m, ATT_W), BF16),
        compiler_params=_params(("arbitrary", "arbitrary")),
        name="window_attention",
    )(proj, proj, proj, cache_k, cache_v, cos_t, sin_t,
      p['q_norm_g'].reshape(1, -1), p['k_norm_g'].reshape(1, -1), _sink_rows(p['sink']))


def _mixer_branches(proj, L, p, tables, cache, layer, cast, depth=None, kv_prev=None):
    hy, w_cast = _hyena(proj, L, _filter_spectra(L, p, tables), tables, p, cast)
    gm = _gmlp(proj, p)
    if cache is None:
        att, k_new, v_new = _context_attention(proj, L, p, layer, depth, kv_prev)
    else:
        att = _window_attention(proj, L, cache[0], cache[1], layer, p)
        k_new = v_new = None
    return (hy, att, gm), w_cast, k_new, v_new


_LAYER_PARAMS = ('norm1_g', 'norm2_g', 'conv_w', 'conv_b', 'flt_w1', 'flt_b1', 'flt_f1', 'flt_w2',
                 'flt_b2', 'flt_f2', 'flt_w3', 'hy_skip', 'q_norm_g', 'k_norm_g', 'sink',
                 'gm_ln_g', 'gm_ln_b', 'gm_ws', 'gm_bs', 'b_gate')


def kernel(x_prompt, x_sample, cache_k, cache_v, c, c_ctx, w_mod, b_mod, norm1_g, norm2_g,
           w_in, conv_w, conv_b, flt_w1, flt_b1, flt_f1, flt_w2, flt_b2, flt_f2, flt_w3,
           hy_skip, q_norm_g, k_norm_g, sink, gm_ln_g, gm_ln_b, gm_ws, gm_bs,
           w_p_hy, w_p_at, w_p_gm, w_gate, b_gate, w_out, w_up, w_down):
    args = dict(locals())
    batch, seq, d = x_prompt.shape
    dec_batch, dec_seq, _ = x_sample.shape
    depth = w_mod.shape[0]
    assert 1 + dec_batch <= PAD_ROWS

    c_rows = jnp.concatenate(
        [c_ctx[None, :], c, jnp.zeros((PAD_ROWS - 1 - dec_batch, d), F32)], axis=0)
    mods = _modulation(c_rows, w_mod, b_mod).reshape(depth, PAD_ROWS, 1, 6 * d)

    tables = {L: _dft_tables(L) for L in (seq, dec_seq)}
    cache = (cache_k.reshape(dec_batch, depth, -1, KV_W), cache_v.reshape(dec_batch, depth, -1, KV_W))

    y_p = x_prompt.reshape(batch * seq, d)
    y_s = x_sample.reshape(dec_batch * dec_seq, d)
    new_kv = None
    ctx_seg = (0, batch * seq)
    lat_seg = (1, dec_seq)
    w_in_b = (w_in[:1].astype(BF16), 0)
    for l in range(depth):
        p = {name: args[name][l] for name in _LAYER_PARAMS}
        m_l = mods[l]
        h_c = _normmod(y_p, p['norm1_g'], m_l, *ctx_seg, sh_chunk=0, sc_chunk=1)
        h_s = _normmod(y_s, p['norm1_g'], m_l, *lat_seg, sh_chunk=0, sc_chunk=1)
        proj_c, w_gate_b = _matmul(h_c, w_in_b, _mm_kernel, F32, "in_proj", cast=(w_gate, l))
        proj_s, w_p_at_b = _matmul(h_s, w_in_b, _mm_kernel, F32, "in_proj", cast=(w_p_at, l))
        gates_c, w_up_b = _matmul(h_c, (w_gate_b, 0), _mm_sigmoid_kernel, BF16, "gate_proj",
                                  bias=p['b_gate'], cast=(w_up, l))
        gates_s, w_out_b = _matmul(h_s, (w_gate_b, 0), _mm_sigmoid_kernel, BF16, "gate_proj",
                                   bias=p['b_gate'], cast=(w_out, l))
        br_c, w_p_hy_b, *new_kv = _mixer_branches(proj_c, seq, p, tables[seq], None, l, (w_p_hy, l),
                                                  depth, new_kv)
        br_s, w_p_gm_b, _, _ = _mixer_branches(proj_s, dec_seq, p, tables[dec_seq], cache, l, (w_p_gm, l))
        w_branch = ((w_p_hy_b, 0), (w_p_at_b, 0), (w_p_gm_b, 0))
        merged_c = _merge(*br_c, *w_branch, gates_c)
        merged_s = _merge(*br_s, *w_branch, gates_s)
        y_p, = _matmul_residual(merged_c, (w_out_b, 0), y_p, m_l, *ctx_seg, 2, "out_proj")
        y_s, = _matmul_residual(merged_s, (w_out_b, 0), y_s, m_l, *lat_seg, 2, "out_proj")
        h_c = _normmod(y_p, p['norm2_g'], m_l, *ctx_seg, sh_chunk=3, sc_chunk=4)
        h_s = _normmod(y_s, p['norm2_g'], m_l, *lat_seg, sh_chunk=3, sc_chunk=4)
        act_c, w_down_b = _matmul(h_c, (w_up_b, 0), _mm_relu2_kernel, BF16, "mlp_up", cast=(w_down, l))
        act_s, = _matmul(h_s, (w_up_b, 0), _mm_relu2_kernel, BF16, "mlp_up")
        if l + 1 < depth:
            y_p, w_in_next = _matmul_residual(act_c, (w_down_b, 0), y_p, m_l, *ctx_seg, 5, "mlp_down",
                                              cast=(w_in, l + 1))
            w_in_b = (w_in_next, 0)
        else:
            y_p, = _matmul_residual(act_c, (w_down_b, 0), y_p, m_l, *ctx_seg, 5, "mlp_down")
        y_s, = _matmul_residual(act_s, (w_down_b, 0), y_s, m_l, *lat_seg, 5, "mlp_down")
    return (y_p.reshape(batch, seq, d), y_s.reshape(dec_batch, dec_seq, d),
            new_kv[0], new_kv[1])
```

```python
import functools
import math

import jax
import jax.numpy as jnp
import numpy as np
from jax import lax
from jax.experimental import pallas as pl
from jax.experimental.pallas import tpu as pltpu

F32 = jnp.float32
BF16 = jnp.bfloat16

D_MODEL = 4096
HEAD_DIM = 128
N_Q_HEADS = 16
N_KV_HEADS = 4
GQA_GROUP = 4
ATT_W = N_Q_HEADS * HEAD_DIM
KV_W = N_KV_HEADS * HEAD_DIM
GRID_W = 64
WINDOW = 128
BLOCK = 128
ROPE_BASE = 10000.0
ROT_FREQS = 32
D_H = 1024
HY_ORDER = 2
POS_BANDS = 16
POS_DIM = 1 + 2 * POS_BANDS
FILT_HID = 64
DECAY_MIN = math.log(1e-2) / 1.5
DECAY_MAX = math.log(1e-2) / 0.3
D_G = 1024
CHUNK = 128
N_GM_GROUPS = 8
D_FF = 4 * D_MODEL
IN_COLS = 3 * D_H + ATT_W + 2 * KV_W + 2 * D_G
EPS = 1e-6
NEG = -1e30
ATT_SCALE = HEAD_DIM ** -0.5

COL_HY = 0
COL_Q = 3 * D_H
COL_K = COL_Q + ATT_W
COL_V = COL_K + KV_W
COL_GM = COL_V + KV_W

V7X_LANES = 128
V7X_VMEM_BYTES = 64 * 1024 * 1024
VMEM_LIMIT = 60 * 1024 * 1024

TM = 1024
TN = 1024
TN_KSPLIT = 512
TN_MERGE = 1024
TK = 4096
NORM_ROWS = 512
HY_TC = 256
HY_TILE_ELEMS = 256 * 1024
HY_SEQS = 2
ATT_ROWS = 512
GM_ROWS = 512
MOD_TN = 512
PAD_ROWS = 8
BF16_SUBLANES = 16


def _params(sem):
    return pltpu.CompilerParams(dimension_semantics=sem, vmem_limit_bytes=VMEM_LIMIT)


def _split(a):
    hi = a.astype(BF16)
    lo = (a - hi.astype(F32)).astype(BF16)
    return hi, lo


def _dot(a, b):
    return jnp.dot(a, b, preferred_element_type=F32)


def _dot3(a_hi, a_lo, b_hi, b_lo):
    return _dot(a_hi, b_hi) + _dot(a_lo, b_hi) + _dot(a_hi, b_lo)


def _mod_kernel(c_ref, w_ref, b_ref, o_ref):
    c = c_ref[...]
    x = c * jax.nn.sigmoid(c)
    o_ref[...] = _dot(x.astype(BF16), w_ref[...].astype(BF16)) + b_ref[...]


def _modulation(c_rows, w_mod, b_mod):
    depth, d, n = w_mod.shape
    return pl.pallas_call(
        _mod_kernel,
        grid=(depth, n // MOD_TN),
        in_specs=[
            pl.BlockSpec((PAD_ROWS, d), lambda l, j: (0, 0)),
            pl.BlockSpec((None, d, MOD_TN), lambda l, j: (l, 0, j)),
            pl.BlockSpec((None, 1, MOD_TN), lambda l, j: (l, 0, j)),
        ],
        out_specs=pl.BlockSpec((None, PAD_ROWS, MOD_TN), lambda l, j: (l, 0, j)),
        out_shape=jax.ShapeDtypeStruct((depth, PAD_ROWS, n), F32),
        compiler_params=_params(("arbitrary", "arbitrary")),
        name="modulation",
    )(c_rows, w_mod, b_mod.reshape(depth, 1, n))


def _normmod_kernel(x_ref, g_ref, sc_ref, sh_ref, o_ref):
    x = x_ref[...]
    y = x * lax.rsqrt(jnp.mean(x * x, axis=-1, keepdims=True) + EPS) * g_ref[...]
    o_ref[...] = (y * (1.0 + sc_ref[...]) + sh_ref[...]).astype(o_ref.dtype)


def _normmod(x, g, mods, seg0, rows_per_seg, sh_chunk, sc_chunk):
    m, d = x.shape
    tiles_per_seg = rows_per_seg // NORM_ROWS
    seg = lambda i: seg0 + i // tiles_per_seg
    return pl.pallas_call(
        _normmod_kernel,
        grid=(m // NORM_ROWS,),
        in_specs=[
            pl.BlockSpec((NORM_ROWS, d), lambda i: (i, 0)),
            pl.BlockSpec((1, d), lambda i: (0, 0)),
            pl.BlockSpec((None, 1, d), lambda i: (seg(i), 0, sc_chunk)),
            pl.BlockSpec((None, 1, d), lambda i: (seg(i), 0, sh_chunk)),
        ],
        out_specs=pl.BlockSpec((NORM_ROWS, d), lambda i: (i, 0)),
        out_shape=jax.ShapeDtypeStruct((m, d), BF16),
        compiler_params=_params(("arbitrary",)),
        name="normmod",
    )(x, g.reshape(1, d), mods, mods)


def _mm_kernel(x_ref, w_ref, o_ref):
    o_ref[...] = _dot(x_ref[...], w_ref[...]).astype(o_ref.dtype)


def _mm_sigmoid_kernel(x_ref, w_ref, b_ref, o_ref):
    t = _dot(x_ref[...], w_ref[...]) + b_ref[...]
    o_ref[...] = (0.5 * jnp.tanh(0.5 * t) + 0.5).astype(o_ref.dtype)


def _mm_relu2_kernel(x_ref, w_ref, o_ref):
    a = jnp.maximum(_dot(x_ref[...], w_ref[...]), 0.0)
    o_ref[...] = (a * a).astype(o_ref.dtype)


def _call(body, grid, in_specs, out_specs, out_shape, args, name, scratch_shapes=(), cast=None):
    sem = ("arbitrary",) * len(grid)
    if cast is None:
        out = pl.pallas_call(body, grid=grid, in_specs=in_specs, out_specs=out_specs, out_shape=out_shape,
                             scratch_shapes=scratch_shapes, compiler_params=_params(sem), name=name)(*args)
        return tuple(out)
    src, layer = cast
    _, k2, n2 = src.shape
    steps = math.prod(grid)
    slabs = min(1 << (steps.bit_length() - 1), k2 // BF16_SUBLANES)
    rows = k2 // slabs

    def slab(*ids):
        step = ids[0]
        for extent, idx in zip(grid[1:], ids[1:]):
            step = step * extent + idx
        return jnp.minimum(step, slabs - 1)

    n_in, n_out = len(in_specs), len(out_specs)

    def kernel(*refs):
        src_ref, dst_ref = refs[n_in], refs[n_in + 1 + n_out]
        body(*refs[:n_in], *refs[n_in + 1:n_in + 1 + n_out], *refs[n_in + 2 + n_out:])
        dst_ref[...] = src_ref[...].astype(BF16)

    out = pl.pallas_call(
        kernel, grid=grid,
        in_specs=[*in_specs, pl.BlockSpec((None, rows, n2), lambda *ids: (layer, slab(*ids), 0))],
        out_specs=[*out_specs, pl.BlockSpec((None, rows, n2), lambda *ids: (0, slab(*ids), 0))],
        out_shape=[*out_shape, jax.ShapeDtypeStruct((1, k2, n2), BF16)],
        scratch_shapes=scratch_shapes, compiler_params=_params(sem), name=name)(*args, src)
    return tuple(out)


def _matmul(x, w, kernel, out_dtype, name, bias=None, cast=None):
    w_arr, layer = w
    m, k = x.shape
    n = w_arr.shape[2]
    in_specs = [
        pl.BlockSpec((TM, k), lambda i, j: (i, 0)),
        pl.BlockSpec((None, k, TN), lambda i, j: (layer, 0, j)),
    ]
    args = [x, w_arr]
    if bias is not None:
        in_specs.append(pl.BlockSpec((1, TN), lambda i, j: (0, j)))
        args.append(bias.reshape(1, n))
    return _call(kernel, (m // TM, n // TN), in_specs,
                 [pl.BlockSpec((TM, TN), lambda i, j: (i, j))],
                 [jax.ShapeDtypeStruct((m, n), out_dtype)], args, name, cast=cast)


def _mm_residual_kernel(a_ref, w_ref, x_ref, g_ref, o_ref):
    o_ref[...] = x_ref[...] + g_ref[...] * _dot(a_ref[...], w_ref[...])


def _mm_residual_ksplit_kernel(a_ref, w_ref, x_ref, g_ref, o_ref, acc_ref, *, nk):
    kk = pl.program_id(1)
    j = pl.program_id(2)

    @pl.when(kk == 0)
    def _():
        acc_ref[j] = _dot(a_ref[...], w_ref[...])

    @pl.when((kk > 0) & (kk < nk - 1))
    def _():
        acc_ref[j] += _dot(a_ref[...], w_ref[...])

    @pl.when(kk == nk - 1)
    def _():
        o_ref[...] = x_ref[...] + g_ref[...] * (acc_ref[j] + _dot(a_ref[...], w_ref[...]))


def _matmul_residual(a, w, x, mods, seg0, rows_per_seg, g_chunk, name, cast=None):
    w_arr, layer = w
    m, k = a.shape
    n = w_arr.shape[2]
    tiles_per_seg = rows_per_seg // TM
    seg = lambda i: seg0 + i // tiles_per_seg
    if k <= TK:
        nb = n // TN
        return _call(
            _mm_residual_kernel, (m // TM, nb),
            [
                pl.BlockSpec((TM, k), lambda i, j: (i, 0)),
                pl.BlockSpec((None, k, TN), lambda i, j: (layer, 0, j)),
                pl.BlockSpec((TM, TN), lambda i, j: (i, j)),
                pl.BlockSpec((None, 1, TN), lambda i, j: (seg(i), 0, g_chunk * nb + j)),
            ],
            [pl.BlockSpec((TM, TN), lambda i, j: (i, j))],
            [jax.ShapeDtypeStruct((m, n), F32)], (a, w_arr, x, mods), name, cast=cast)
    nk = k // TK
    tn = TN_KSPLIT
    nb = n // tn
    out_col = lambda kk, j: jnp.where(kk == nk - 1, j, 0)
    return _call(
        functools.partial(_mm_residual_ksplit_kernel, nk=nk), (m // TM, nk, nb),
        [
            pl.BlockSpec((TM, TK), lambda i, kk, j: (i, kk)),
            pl.BlockSpec((None, TK, tn), lambda i, kk, j: (layer, kk, j)),
            pl.BlockSpec((TM, tn), lambda i, kk, j: (i, out_col(kk, j))),
            pl.BlockSpec((None, 1, tn), lambda i, kk, j: (seg(i), 0, g_chunk * nb + out_col(kk, j))),
        ],
        [pl.BlockSpec((TM, tn), lambda i, kk, j: (i, out_col(kk, j)))],
        [jax.ShapeDtypeStruct((m, n), F32)], (a, w_arr, x, mods), name,
        scratch_shapes=[pltpu.VMEM((nb, TM, tn), F32)], cast=cast)


def _merge_kernel(hy_ref, at_ref, gm_ref, whv_ref, wat_ref, wgm_ref, ga_ref, gb_ref, gc_ref, o_ref):
    merged = (ga_ref[...] * _dot(hy_ref[...], whv_ref[...])
              + gb_ref[...] * _dot(at_ref[...], wat_ref[...])
              + gc_ref[...] * _dot(gm_ref[...], wgm_ref[...]))
    o_ref[...] = merged.astype(o_ref.dtype)


def _merge(hy, at, gm, w_hy, w_at, w_gm, gates):
    m = hy.shape[0]
    n = w_hy[0].shape[2]
    tn = TN_MERGE
    nb = n // tn
    row = lambda width: pl.BlockSpec((TM, width), lambda i, j: (i, 0))
    col = lambda w: pl.BlockSpec((None, w[0].shape[1], tn), lambda i, j: (w[1], 0, j))
    gate = lambda b: pl.BlockSpec((TM, tn), lambda i, j: (i, b * nb + j))
    return _call(
        _merge_kernel, (m // TM, nb),
        [row(hy.shape[1]), row(at.shape[1]), row(gm.shape[1]), col(w_hy), col(w_at), col(w_gm),
         gate(0), gate(1), gate(2)],
        [pl.BlockSpec((TM, tn), lambda i, j: (i, j))],
        [jax.ShapeDtypeStruct((m, n), BF16)],
        (hy, at, gm, w_hy[0], w_at[0], w_gm[0], gates, gates, gates), "merge")[0]


def _dft_tables(L):
    k = np.arange(L)
    ang = ((k[:, None] * k[None, :]) % (2 * L)) * (np.pi / L)
    cos, sin = np.cos(ang), np.sin(ang)
    alt = np.where(k % 2 == 0, 1.0, -1.0)
    fwd = np.concatenate([cos, -sin], axis=0)
    inv = np.concatenate([cos, -sin], axis=1)
    fwd[L, :] = alt
    inv[:, L] = alt
    const = lambda a: jnp.asarray(a.astype(np.float32).astype(BF16))
    return {'cos': const(cos), 'sin': const(sin), 'fwd': const(fwd), 'inv': const(inv)}


def _filter_features(L):
    t = np.linspace(0.0, 1.0, L)
    w = 2.0 * np.pi * np.arange(L) / L
    bands = np.linspace(1e-4, POS_BANDS - 1, POS_BANDS)
    z = np.concatenate([t[:, None], np.cos(w[:, None] * bands), -np.sin(w[:, None] * bands)], axis=-1)
    z = np.pad(z, ((0, 0), (0, V7X_LANES - POS_DIM)))
    return jnp.asarray(z.astype(np.float32)), jnp.asarray(t[:, None].astype(np.float32))


def _filter_kernel(z_ref, t_ref, w1_ref, b1_ref, f1_ref, w2_ref, b2_ref, f2_ref,
                   w3f_ref, w3b_ref, dl_ref, cos_ref, sin_ref, o_ref, *, L):
    z_hi, z_lo = _split(z_ref[...])
    a = jnp.sin(f1_ref[...] * (_dot3(z_hi, z_lo, *_split(w1_ref[...])) + b1_ref[...]))
    a = jnp.sin(f2_ref[...] * (_dot3(*_split(a), *_split(w2_ref[...])) + b2_ref[...]))
    a_hi, a_lo = _split(a)
    decay = jnp.exp(-t_ref[...] * dl_ref[...])
    row = lax.broadcasted_iota(jnp.int32, (L, 1), 0)
    alt = jnp.where(row % 2 == 0, 1.0, -1.0).astype(F32)
    for o in range(HY_ORDER):
        fwd = _dot3(a_hi, a_lo, *_split(w3f_ref[o])) * decay
        bwd = _dot3(a_hi, a_lo, *_split(w3b_ref[o])) * decay
        s = fwd + jnp.where(row == 0, 0.0, bwd)
        d = bwd - fwd
        hr = _dot(cos_ref[...], s.astype(BF16)) * (1.0 / L)
        hi = _dot(sin_ref[...], d.astype(BF16)) * (1.0 / L)
        nyq = jnp.sum(alt * s, axis=0, keepdims=True) * (0.5 / L)
        o_ref[o, 0] = jnp.where(row == 0, 0.5 * hr, hr)
        o_ref[o, 1] = hi
        o_ref[o, 2] = jnp.where(row == 0, nyq, hr)


def _filter_spectra(L, p, tables):
    z, t = _filter_features(L)
    w1 = jnp.pad(p['flt_w1'], ((0, V7X_LANES - POS_DIM), (0, 0)))
    w3 = p['flt_w3'].reshape(FILT_HID, HY_ORDER, 2, D_H)
    w3f = jnp.transpose(w3[:, :, 0], (1, 0, 2))
    w3b = jnp.transpose(w3[:, :, 1], (1, 0, 2))
    deltas = jnp.asarray(np.abs(np.linspace(DECAY_MIN, DECAY_MAX, D_H)).astype(np.float32).reshape(1, D_H))
    full = lambda shape: pl.BlockSpec(shape, lambda c: (0,) * len(shape))
    return pl.pallas_call(
        functools.partial(_filter_kernel, L=L),
        grid=(D_H // HY_TC,),
        in_specs=[
            full((L, V7X_LANES)), full((L, 1)),
            full((V7X_LANES, FILT_HID)), full((1, FILT_HID)), full((1, FILT_HID)),
            full((FILT_HID, FILT_HID)), full((1, FILT_HID)), full((1, FILT_HID)),
            pl.BlockSpec((HY_ORDER, FILT_HID, HY_TC), lambda c: (0, 0, c)),
            pl.BlockSpec((HY_ORDER, FILT_HID, HY_TC), lambda c: (0, 0, c)),
            pl.BlockSpec((1, HY_TC), lambda c: (0, c)),
            full((L, L)), full((L, L)),
        ],
        out_specs=pl.BlockSpec((HY_ORDER, 3, L, HY_TC), lambda c: (0, 0, 0, c)),
        out_shape=jax.ShapeDtypeStruct((HY_ORDER, 3, L, D_H), F32),
        compiler_params=_params(("arbitrary",)),
        name=f"hyena_filter_{L}",
    )(z, t, w1, p['flt_b1'].reshape(1, -1), p['flt_f1'].reshape(1, -1),
      p['flt_w2'], p['flt_b2'].reshape(1, -1), p['flt_f2'].reshape(1, -1),
      w3f, w3b, deltas, tables['cos'], tables['sin'])


def _hyena_kernel(v_ref, x1_ref, x2_ref, wv_ref, wx1_ref, wx2_ref, bv_ref, bx1_ref, bx2_ref,
                  tab_ref, skip_ref, fwd_ref, inv_ref, o_ref, *, L):
    row = lax.broadcasted_iota(jnp.int32, (L, 1), 0)

    def short_conv(u_ref, rows, w_ref, b_ref):
        u = u_ref[rows, :]
        prev = jnp.where(row == 0, 0.0, pltpu.roll(u, 1, 0))
        nxt = jnp.where(row == L - 1, 0.0, pltpu.roll(u, L - 1, 0))
        return prev * w_ref[0:1, :] + u * w_ref[1:2, :] + nxt * w_ref[2:3, :] + b_ref[...]

    seqs = [slice(s * L, (s + 1) * L) for s in range(v_ref.shape[0] // L)]
    zs = [short_conv(v_ref, rows, wv_ref, bv_ref) for rows in seqs]
    gates = [(short_conv(x1_ref, rows, wx1_ref, bx1_ref), short_conv(x2_ref, rows, wx2_ref, bx2_ref))
             for rows in seqs]
    for o in range(HY_ORDER):
        p, q, r = tab_ref[o, 0], tab_ref[o, 1], tab_ref[o, 2]
        specs = [_dot(fwd_ref[...], z.astype(BF16)) for z in zs]
        prods = [(spec[:L] * p - spec[L:] * q, spec[:L] * q + spec[L:] * r) for spec in specs]
        ys = [_dot(inv_ref[...], jnp.concatenate([ya, yb], axis=0).astype(BF16)) for ya, yb in prods]
        zs = [g[o] * (y + skip_ref[o:o + 1, :] * z) for g, y, z in zip(gates, ys, zs)]
    for rows, z in zip(seqs, zs):
        o_ref[rows, :] = z.astype(o_ref.dtype)


def _hyena(proj, L, spectra, tables, p, cast):
    m = proj.shape[0]
    tc = min(D_H, HY_TILE_ELEMS // L)
    nc = D_H // tc
    rows = (HY_SEQS if tc < D_H else 1) * L
    u = lambda part: pl.BlockSpec((rows, tc), lambda c, b: (b, part * nc + c))
    cw = lambda part: pl.BlockSpec((3, tc), lambda c, b: (0, part * nc + c))
    cb = lambda part: pl.BlockSpec((1, tc), lambda c, b: (0, part * nc + c))
    conv_b = p['conv_b'].reshape(1, -1)
    return _call(
        functools.partial(_hyena_kernel, L=L), (nc, m // rows),
        [u(0), u(1), u(2), cw(0), cw(1), cw(2), cb(0), cb(1), cb(2),
         pl.BlockSpec((HY_ORDER, 3, L, tc), lambda c, b: (0, 0, 0, c)),
         pl.BlockSpec((HY_ORDER, tc), lambda c, b: (0, c)),
         pl.BlockSpec((2 * L, L), lambda c, b: (0, 0)),
         pl.BlockSpec((L, 2 * L), lambda c, b: (0, 0))],
        [pl.BlockSpec((rows, tc), lambda c, b: (b, c))],
        [jax.ShapeDtypeStruct((m, D_H), BF16)],
        (proj, proj, proj, p['conv_w'], p['conv_w'], p['conv_w'], conv_b, conv_b, conv_b,
         spectra, p['hy_skip'], tables['fwd'], tables['inv']),
        f"hyena_{L}", cast=cast)


def _gmlp_kernel(u_ref, v_ref, lg_ref, lb_ref, ws_ref, bs_ref, o_ref):
    v = v_ref[...]
    mu = jnp.mean(v, axis=-1, keepdims=True)
    vc = v - mu
    var = jnp.mean(vc * vc, axis=-1, keepdims=True)
    vn = (vc * lax.rsqrt(var + EPS) * lg_ref[...] + lb_ref[...]).astype(BF16)
    for n in range(GM_ROWS // CHUNK):
        rows = slice(n * CHUNK, (n + 1) * CHUNK)
        for g in range(N_GM_GROUPS):
            cols = slice(g * CHUNK, (g + 1) * CHUNK)
            mixed = _dot(ws_ref[g], vn[rows, cols]) + bs_ref[:, g:g + 1]
            o_ref[rows, cols] = (u_ref[rows, cols] * mixed).astype(o_ref.dtype)


def _gmlp(proj, p):
    m = proj.shape[0]
    cu = COL_GM // D_G
    return pl.pallas_call(
        _gmlp_kernel,
        grid=(m // GM_ROWS,),
        in_specs=[
            pl.BlockSpec((GM_ROWS, D_G), lambda i: (i, cu)),
            pl.BlockSpec((GM_ROWS, D_G), lambda i: (i, cu + 1)),
            pl.BlockSpec((1, D_G), lambda i: (0, 0)),
            pl.BlockSpec((1, D_G), lambda i: (0, 0)),
            pl.BlockSpec((N_GM_GROUPS, CHUNK, CHUNK), lambda i: (0, 0, 0)),
            pl.BlockSpec((CHUNK, N_GM_GROUPS), lambda i: (0, 0)),
        ],
        out_specs=pl.BlockSpec((GM_ROWS, D_G), lambda i: (i, 0)),
        out_shape=jax.ShapeDtypeStruct((m, D_G), BF16),
        compiler_params=_params(("arbitrary",)),
        name="gmlp",
    )(proj, proj, p['gm_ln_g'].reshape(1, -1), p['gm_ln_b'].reshape(1, -1),
      p['gm_ws'].astype(BF16), p['gm_bs'].T)


def _rms(x, g):
    return x * lax.rsqrt(jnp.mean(x * x, axis=-1, keepdims=True) + EPS) * g


def _rope(x, cos, sin_signed, lane):
    partner = jnp.where((lane % 64) < ROT_FREQS,
                        pltpu.roll(x, HEAD_DIM - ROT_FREQS, 1), pltpu.roll(x, ROT_FREQS, 1))
    return x * cos + partner * sin_signed


def _softmax_pv(q_all, k_all, v_all, mask, sink_col):
    outs = []
    for r in range(0, q_all.shape[0], ATT_ROWS):
        rows = slice(r, r + ATT_ROWS)
        s = lax.dot_general(q_all[rows], k_all, (((1,), (1,)), ((), ())),
                            preferred_element_type=F32) * ATT_SCALE
        if mask is not None:
            s = jnp.where(mask[rows], s, NEG)
        sink = sink_col[rows]
        m = jnp.maximum(jnp.max(s, axis=-1, keepdims=True), sink)
        e = jnp.exp(s - m)
        den = jnp.sum(e, axis=-1, keepdims=True) + jnp.exp(sink - m)
        outs.append(_dot(e.astype(BF16), v_all) / den)
    return jnp.concatenate(outs, axis=0)


def _sink_column(sink_ref, rows):
    return jnp.concatenate(
        [jnp.broadcast_to(sink_ref[g:g + 1, 0:1], (rows, 1)) for g in range(GQA_GROUP)], axis=0)


def _ctx_att_kernel(qa_ref, qb_ref, k_ref, v_ref, qg_ref, kg_ref, sink_ref, o_ref, ko_ref, vo_ref, *,
                    L, layer, zero_other_layers):
    if zero_other_layers:
        for d in range(ko_ref.shape[0]):
            if d != layer:
                ko_ref[d] = jnp.zeros(ko_ref.shape[1:], ko_ref.dtype)
                vo_ref[d] = jnp.zeros(vo_ref.shape[1:], vo_ref.dtype)
        ko_ref, vo_ref = ko_ref.at[layer], vo_ref.at[layer]
    qw = GQA_GROUP * HEAD_DIM
    q_halves = (qa_ref, qb_ref)
    for h in range(N_KV_HEADS):
        q_ref = q_halves[h // 2]
        q0 = (h % 2) * qw
        q_all = jnp.concatenate(
            [_rms(q_ref[:, q0 + g * HEAD_DIM:q0 + (g + 1) * HEAD_DIM], qg_ref[...]) for g in range(GQA_GROUP)],
            axis=0)
        cols = slice(h * HEAD_DIM, (h + 1) * HEAD_DIM)
        k = _rms(k_ref[:, cols], kg_ref[...])
        v = v_ref[:, cols]
        ko_ref[:, h, :] = k
        vo_ref[:, h, :] = v
        out = _softmax_pv(q_all.astype(BF16), k.astype(BF16), v.astype(BF16), None,
                          _sink_column(sink_ref.at[h], L))
        for g in range(GQA_GROUP):
            o_ref[:, h * qw + g * HEAD_DIM:h * qw + (g + 1) * HEAD_DIM] = out[g * L:(g + 1) * L].astype(o_ref.dtype)


def _sink_rows(sink):
    s = sink.reshape(N_KV_HEADS, GQA_GROUP, 1)
    s = jnp.pad(s, ((0, 0), (0, PAD_ROWS - GQA_GROUP), (0, 0)))
    return jnp.broadcast_to(s, (N_KV_HEADS, PAD_ROWS, V7X_LANES))


def _context_attention(proj, L, p, layer, depth, kv_prev):
    m = proj.shape[0]
    b = m // L
    half = ATT_W // 2
    kv_shape = jax.ShapeDtypeStruct((b, depth, L, N_KV_HEADS, HEAD_DIM), F32)
    if kv_prev is None:
        kv_spec = pl.BlockSpec((None, depth, L, N_KV_HEADS, HEAD_DIM), lambda i: (i, 0, 0, 0, 0))
    else:
        kv_spec = pl.BlockSpec((None, None, L, N_KV_HEADS, HEAD_DIM), lambda i: (i, layer, 0, 0, 0))
    in_specs = [
        pl.BlockSpec((L, half), lambda i: (i, COL_Q // half)),
        pl.BlockSpec((L, half), lambda i: (i, COL_Q // half + 1)),
        pl.BlockSpec((L, KV_W), lambda i: (i, COL_K // KV_W)),
        pl.BlockSpec((L, KV_W), lambda i: (i, COL_V // KV_W)),
        pl.BlockSpec((1, HEAD_DIM), lambda i: (0, 0)),
        pl.BlockSpec((1, HEAD_DIM), lambda i: (0, 0)),
        pl.BlockSpec((N_KV_HEADS, PAD_ROWS, V7X_LANES), lambda i: (0, 0, 0)),
    ]
    args = [proj, proj, proj, proj, p['q_norm_g'].reshape(1, -1), p['k_norm_g'].reshape(1, -1),
            _sink_rows(p['sink'])]
    kernel = functools.partial(_ctx_att_kernel, L=L, layer=layer, zero_other_layers=kv_prev is None)
    aliases = {}
    if kv_prev is not None:
        n_in = len(in_specs)
        in_specs += [pl.BlockSpec(memory_space=pl.ANY)] * 2
        args += list(kv_prev)
        aliases = {n_in: 1, n_in + 1: 2}
        inner = kernel
        kernel = lambda *refs: inner(*refs[:n_in], *refs[n_in + 2:])
    return pl.pallas_call(
        kernel,
        grid=(b,),
        in_specs=in_specs,
        out_specs=[pl.BlockSpec((L, ATT_W), lambda i: (i, 0)), kv_spec, kv_spec],
        out_shape=[jax.ShapeDtypeStruct((m, ATT_W), BF16), kv_shape, kv_shape],
        input_output_aliases=aliases,
        compiler_params=_params(("arbitrary",)),
        name="context_attention",
    )(*args)


def _rope_tables(L):
    rows = L // GRID_W
    row = np.repeat(np.arange(rows), GRID_W)
    col = np.tile(np.arange(GRID_W), rows)
    inv = ROPE_BASE ** (-np.arange(ROT_FREQS) / ROT_FREQS)
    pos = np.stack([row, col], axis=-1).astype(np.float64)
    ang = pos[:, :, None] * inv
    cos, sin = np.cos(ang), np.sin(ang)
    cos_t = np.stack([cos, cos], axis=2).reshape(L, HEAD_DIM)
    sin_t = np.stack([-sin, sin], axis=2).reshape(L, HEAD_DIM)
    return jnp.asarray(cos_t.astype(np.float32)), jnp.asarray(sin_t.astype(np.float32))


def _lat_att_kernel(q_ref, k_ref, v_ref, kc_ref, vc_ref, cos_ref, sin_ref, qg_ref, kg_ref, sink_ref,
                    o_ref, *, nb):
    lane = lax.broadcasted_iota(jnp.int32, (1, HEAD_DIM), 1)
    k_seq = _rope(_rms(k_ref[...], kg_ref[...]), cos_ref[...], sin_ref[...], lane).astype(BF16)
    v_seq = v_ref[...].astype(BF16)
    k_ctx = kc_ref[...].astype(BF16)
    v_ctx = vc_ref[...].astype(BF16)
    n_ctx = k_ctx.shape[0]
    sink_col = _sink_column(sink_ref, BLOCK)

    def band_mask(has_prev, has_next):
        n_keys = n_ctx + (1 + has_prev + has_next) * BLOCK
        shape = (GQA_GROUP * BLOCK, n_keys)
        i = lax.broadcasted_iota(jnp.int32, shape, 0) % BLOCK
        c = lax.broadcasted_iota(jnp.int32, shape, 1)
        ok = None
        if has_prev:
            ok = (c < n_ctx) | (c >= n_ctx + BLOCK) | (c - n_ctx >= i)
        if has_next:
            start = n_keys - BLOCK
            nxt = (c < start) | (c - start <= i)
            ok = nxt if ok is None else ok & nxt
        return ok

    masks = {}
    for n in range(nb):
        lo, hi = max(n - 1, 0), min(n + 1, nb - 1)
        shape_key = (lo < n, hi > n)
        if shape_key not in masks:
            masks[shape_key] = band_mask(*shape_key)
        rows = slice(n * BLOCK, (n + 1) * BLOCK)
        cos, sin = cos_ref[rows, :], sin_ref[rows, :]
        q_all = jnp.concatenate(
            [_rope(_rms(q_ref[rows, g * HEAD_DIM:(g + 1) * HEAD_DIM], qg_ref[...]), cos, sin, lane)
             for g in range(GQA_GROUP)], axis=0).astype(BF16)
        band = slice(lo * BLOCK, (hi + 1) * BLOCK)
        k_all = jnp.concatenate([k_ctx, k_seq[band]], axis=0)
        v_all = jnp.concatenate([v_ctx, v_seq[band]], axis=0)
        out = _softmax_pv(q_all, k_all, v_all, masks[shape_key], sink_col)
        for g in range(GQA_GROUP):
            o_ref[rows, g * HEAD_DIM:(g + 1) * HEAD_DIM] = out[g * BLOCK:(g + 1) * BLOCK].astype(o_ref.dtype)


def _window_attention(proj, L, cache_k, cache_v, layer, p):
    m = proj.shape[0]
    b = m // L
    nb = L // BLOCK
    n_ctx = cache_k.shape[2]
    qw = GQA_GROUP * HEAD_DIM
    cos_t, sin_t = _rope_tables(L)
    seq = lambda col0: pl.BlockSpec((L, HEAD_DIM), lambda i, h: (i, col0 // HEAD_DIM + h))
    cache = pl.BlockSpec((None, None, n_ctx, HEAD_DIM), lambda i, h: (i, layer, 0, h))
    table = pl.BlockSpec((L, HEAD_DIM), lambda i, h: (0, 0))
    gain = pl.BlockSpec((1, HEAD_DIM), lambda i, h: (0, 0))
    return pl.pallas_call(
        functools.partial(_lat_att_kernel, nb=nb),
        grid=(b, N_KV_HEADS),
        in_specs=[
            pl.BlockSpec((L, qw), lambda i, h: (i, COL_Q // qw + h)),
            seq(COL_K), seq(COL_V), cache, cache, table, table, gain, gain,
            pl.BlockSpec((None, PAD_ROWS, V7X_LANES), lambda i, h: (h, 0, 0)),
        ],
        out_specs=pl.BlockSpec((L, qw), lambda i, h: (i, h)),
        out_shape=jax.ShapeDtypeStruct((m, ATT_W), BF16),
        compiler_params=_params(("arbitrary", "arbitrary")),
        name="window_attention",
    )(proj, proj, proj, cache_k, cache_v, cos_t, sin_t,
      p['q_norm_g'].reshape(1, -1), p['k_norm_g'].reshape(1, -1), _sink_rows(p['sink']))


def _mixer_branches(proj, L, p, tables, cache, layer, cast, depth=None, kv_prev=None):
    hy, w_cast = _hyena(proj, L, _filter_spectra(L, p, tables), tables, p, cast)
    gm = _gmlp(proj, p)
    if cache is None:
        att, k_new, v_new = _context_attention(proj, L, p, layer, depth, kv_prev)
    else:
        att = _window_attention(proj, L, cache[0], cache[1], layer, p)
        k_new = v_new = None
    return (hy, att, gm), w_cast, k_new, v_new


_LAYER_PARAMS = ('norm1_g', 'norm2_g', 'conv_w', 'conv_b', 'flt_w1', 'flt_b1', 'flt_f1', 'flt_w2',
                 'flt_b2', 'flt_f2', 'flt_w3', 'hy_skip', 'q_norm_g', 'k_norm_g', 'sink',
                 'gm_ln_g', 'gm_ln_b', 'gm_ws', 'gm_bs', 'b_gate')


def kernel(x_prompt, x_sample, cache_k, cache_v, c, c_ctx, w_mod, b_mod, norm1_g, norm2_g,
           w_in, conv_w, conv_b, flt_w1, flt_b1, flt_f1, flt_w2, flt_b2, flt_f2, flt_w3,
           hy_skip, q_norm_g, k_norm_g, sink, gm_ln_g, gm_ln_b, gm_ws, gm_bs,
           w_p_hy, w_p_at, w_p_gm, w_gate, b_gate, w_out, w_up, w_down):
    args = dict(locals())
    batch, seq, d = x_prompt.shape
    dec_batch, dec_seq, _ = x_sample.shape
    depth = w_mod.shape[0]
    assert 1 + dec_batch <= PAD_ROWS

    c_rows = jnp.concatenate(
        [c_ctx[None, :], c, jnp.zeros((PAD_ROWS - 1 - dec_batch, d), F32)], axis=0)
    mods = _modulation(c_rows, w_mod, b_mod).reshape(depth, PAD_ROWS, 1, 6 * d)

    tables = {L: _dft_tables(L) for L in (seq, dec_seq)}
    cache = (cache_k.reshape(dec_batch, depth, -1, KV_W), cache_v.reshape(dec_batch, depth, -1, KV_W))

    y_p = x_prompt.reshape(batch * seq, d)
    y_s = x_sample.reshape(dec_batch * dec_seq, d)
    new_kv = None
    ctx_seg = (0, batch * seq)
    lat_seg = (1, dec_seq)
    w_in_b = (w_in[:1].astype(BF16), 0)
    for l in range(depth):
        p = {name: args[name][l] for name in _LAYER_PARAMS}
        m_l = mods[l]
        h_c = _normmod(y_p, p['norm1_g'], m_l, *ctx_seg, sh_chunk=0, sc_chunk=1)
        h_s = _normmod(y_s, p['norm1_g'], m_l, *lat_seg, sh_chunk=0, sc_chunk=1)
        proj_c, w_gate_b = _matmul(h_c, w_in_b, _mm_kernel, F32, "in_proj", cast=(w_gate, l))
        proj_s, w_p_at_b = _matmul(h_s, w_in_b, _mm_kernel, F32, "in_proj", cast=(w_p_at, l))
        gates_c, w_up_b = _matmul(h_c, (w_gate_b, 0), _mm_sigmoid_kernel, BF16, "gate_proj",
                                  bias=p['b_gate'], cast=(w_up, l))
        gates_s, w_out_b = _matmul(h_s, (w_gate_b, 0), _mm_sigmoid_kernel, BF16, "gate_proj",
                                   bias=p['b_gate'], cast=(w_out, l))
        br_c, w_p_hy_b, *new_kv = _mixer_branches(proj_c, seq, p, tables[seq], None, l, (w_p_hy, l),
                                                  depth, new_kv)
        br_s, w_p_gm_b, _, _ = _mixer_branches(proj_s, dec_seq, p, tables[dec_seq], cache, l, (w_p_gm, l))
        w_branch = ((w_p_hy_b, 0), (w_p_at_b, 0), (w_p_gm_b, 0))
        merged_c = _merge(*br_c, *w_branch, gates_c)
        merged_s = _merge(*br_s, *w_branch, gates_s)
        y_p, = _matmul_residual(merged_c, (w_out_b, 0), y_p, m_l, *ctx_seg, 2, "out_proj")
        y_s, = _matmul_residual(merged_s, (w_out_b, 0), y_s, m_l, *lat_seg, 2, "out_proj")
        h_c = _normmod(y_p, p['norm2_g'], m_l, *ctx_seg, sh_chunk=3, sc_chunk=4)
        h_s = _normmod(y_s, p['norm2_g'], m_l, *lat_seg, sh_chunk=3, sc_chunk=4)
        act_c, w_down_b = _matmul(h_c, (w_up_b, 0), _mm_relu2_kernel, BF16, "mlp_up", cast=(w_down, l))
        act_s, = _matmul(h_s, (w_up_b, 0), _mm_relu2_kernel, BF16, "mlp_up")
        if l + 1 < depth:
            y_p, w_in_next = _matmul_residual(act_c, (w_down_b, 0), y_p, m_l, *ctx_seg, 5, "mlp_down",
                                              cast=(w_in, l + 1))
            w_in_b = (w_in_next, 0)
        else:
            y_p, = _matmul_residual(act_c, (w_down_b, 0), y_p, m_l, *ctx_seg, 5, "mlp_down")
        y_s, = _matmul_residual(act_s, (w_down_b, 0), y_s, m_l, *lat_seg, 5, "mlp_down")
    return (y_p.reshape(batch, seq, d), y_s.reshape(dec_batch, dec_seq, d),
            new_kv[0], new_kv[1])
```

```python
import functools
import math

import jax
import jax.numpy as jnp
import numpy as np
from jax import lax
from jax.experimental import pallas as pl
from jax.experimental.pallas import tpu as pltpu

F32 = jnp.float32
BF16 = jnp.bfloat16

D_MODEL = 4096
HEAD_DIM = 128
N_Q_HEADS = 16
N_KV_HEADS = 4
GQA_GROUP = 4
ATT_W = N_Q_HEADS * HEAD_DIM
KV_W = N_KV_HEADS * HEAD_DIM
GRID_W = 64
WINDOW = 128
BLOCK = 128
ROPE_BASE = 10000.0
ROT_FREQS = 32
D_H = 1024
HY_ORDER = 2
POS_BANDS = 16
POS_DIM = 1 + 2 * POS_BANDS
FILT_HID = 64
DECAY_MIN = math.log(1e-2) / 1.5
DECAY_MAX = math.log(1e-2) / 0.3
D_G = 1024
CHUNK = 128
N_GM_GROUPS = 8
D_FF = 4 * D_MODEL
IN_COLS = 3 * D_H + ATT_W + 2 * KV_W + 2 * D_G
EPS = 1e-6
NEG = -1e30
ATT_SCALE = HEAD_DIM ** -0.5

COL_HY = 0
COL_Q = 3 * D_H
COL_K = COL_Q + ATT_W
COL_V = COL_K + KV_W
COL_GM = COL_V + KV_W

V7X_LANES = 128
V7X_VMEM_BYTES = 64 * 1024 * 1024
VMEM_LIMIT = 60 * 1024 * 1024

TM = 1024
TN = 1024
TN_KSPLIT = 512
TN_MERGE = 1024
TK = 4096
NORM_ROWS = 512
HY_TC = 256
HY_TILE_ELEMS = 256 * 1024
HY_SEQS = 2
ATT_ROWS = 1024
ATT_GROUP = 8
GM_ROWS = 512
MOD_TN = 512
PAD_ROWS = 8
BF16_SUBLANES = 16


def _params(sem):
    return pltpu.CompilerParams(dimension_semantics=sem, vmem_limit_bytes=VMEM_LIMIT)


def _split(a):
    hi = a.astype(BF16)
    lo = (a - hi.astype(F32)).astype(BF16)
    return hi, lo


def _dot(a, b):
    return jnp.dot(a, b, preferred_element_type=F32)


def _dot3(a_hi, a_lo, b_hi, b_lo):
    return _dot(a_hi, b_hi) + _dot(a_lo, b_hi) + _dot(a_hi, b_lo)


def _mod_kernel(c_ref, w_ref, b_ref, o_ref):
    c = c_ref[...]
    x = c * jax.nn.sigmoid(c)
    o_ref[...] = _dot(x.astype(BF16), w_ref[...].astype(BF16)) + b_ref[...]


def _modulation(c_rows, w_mod, b_mod):
    depth, d, n = w_mod.shape
    return pl.pallas_call(
        _mod_kernel,
        grid=(depth, n // MOD_TN),
        in_specs=[
            pl.BlockSpec((PAD_ROWS, d), lambda l, j: (0, 0)),
            pl.BlockSpec((None, d, MOD_TN), lambda l, j: (l, 0, j)),
            pl.BlockSpec((None, 1, MOD_TN), lambda l, j: (l, 0, j)),
        ],
        out_specs=pl.BlockSpec((None, PAD_ROWS, MOD_TN), lambda l, j: (l, 0, j)),
        out_shape=jax.ShapeDtypeStruct((depth, PAD_ROWS, n), F32),
        compiler_params=_params(("arbitrary", "arbitrary")),
        name="modulation",
    )(c_rows, w_mod, b_mod.reshape(depth, 1, n))


def _normmod_kernel(x_ref, g_ref, sc_ref, sh_ref, o_ref):
    x = x_ref[...]
    y = x * lax.rsqrt(jnp.mean(x * x, axis=-1, keepdims=True) + EPS) * g_ref[...]
    o_ref[...] = (y * (1.0 + sc_ref[...]) + sh_ref[...]).astype(o_ref.dtype)


def _normmod(x, g, mods, seg0, rows_per_seg, sh_chunk, sc_chunk):
    m, d = x.shape
    tiles_per_seg = rows_per_seg // NORM_ROWS
    seg = lambda i: seg0 + i // tiles_per_seg
    return pl.pallas_call(
        _normmod_kernel,
        grid=(m // NORM_ROWS,),
        in_specs=[
            pl.BlockSpec((NORM_ROWS, d), lambda i: (i, 0)),
            pl.BlockSpec((1, d), lambda i: (0, 0)),
            pl.BlockSpec((None, 1, d), lambda i: (seg(i), 0, sc_chunk)),
            pl.BlockSpec((None, 1, d), lambda i: (seg(i), 0, sh_chunk)),
        ],
        out_specs=pl.BlockSpec((NORM_ROWS, d), lambda i: (i, 0)),
        out_shape=jax.ShapeDtypeStruct((m, d), BF16),
        compiler_params=_params(("arbitrary",)),
        name="normmod",
    )(x, g.reshape(1, d), mods, mods)


def _mm_kernel(x_ref, w_ref, o_ref):
    o_ref[...] = _dot(x_ref[...], w_ref[...]).astype(o_ref.dtype)


def _mm_sigmoid_kernel(x_ref, w_ref, b_ref, o_ref):
    t = _dot(x_ref[...], w_ref[...]) + b_ref[...]
    o_ref[...] = (0.5 * jnp.tanh(0.5 * t) + 0.5).astype(o_ref.dtype)


def _mm_relu2_kernel(x_ref, w_ref, o_ref):
    a = jnp.maximum(_dot(x_ref[...], w_ref[...]), 0.0)
    o_ref[...] = (a * a).astype(o_ref.dtype)


def _call(body, grid, in_specs, out_specs, out_shape, args, name, scratch_shapes=(), cast=None):
    sem = ("arbitrary",) * len(grid)
    if cast is None:
        out = pl.pallas_call(body, grid=grid, in_specs=in_specs, out_specs=out_specs, out_shape=out_shape,
                             scratch_shapes=scratch_shapes, compiler_params=_params(sem), name=name)(*args)
        return tuple(out)
    src, layer = cast
    _, k2, n2 = src.shape
    steps = math.prod(grid)
    slabs = min(1 << (steps.bit_length() - 1), k2 // BF16_SUBLANES)
    rows = k2 // slabs

    def slab(*ids):
        step = ids[0]
        for extent, idx in zip(grid[1:], ids[1:]):
            step = step * extent + idx
        return jnp.minimum(step, slabs - 1)

    n_in, n_out = len(in_specs), len(out_specs)

    def kernel(*refs):
        src_ref, dst_ref = refs[n_in], refs[n_in + 1 + n_out]
        body(*refs[:n_in], *refs[n_in + 1:n_in + 1 + n_out], *refs[n_in + 2 + n_out:])
        dst_ref[...] = src_ref[...].astype(BF16)

    out = pl.pallas_call(
        kernel, grid=grid,
        in_specs=[*in_specs, pl.BlockSpec((None, rows, n2), lambda *ids: (layer, slab(*ids), 0))],
        out_specs=[*out_specs, pl.BlockSpec((None, rows, n2), lambda *ids: (0, slab(*ids), 0))],
        out_shape=[*out_shape, jax.ShapeDtypeStruct((1, k2, n2), BF16)],
        scratch_shapes=scratch_shapes, compiler_params=_params(sem), name=name)(*args, src)
    return tuple(out)


def _matmul(x, w, kernel, out_dtype, name, bias=None, cast=None):
    w_arr, layer = w
    m, k = x.shape
    n = w_arr.shape[2]
    in_specs = [
        pl.BlockSpec((TM, k), lambda i, j: (i, 0)),
        pl.BlockSpec((None, k, TN), lambda i, j: (layer, 0, j)),
    ]
    args = [x, w_arr]
    if bias is not None:
        in_specs.append(pl.BlockSpec((1, TN), lambda i, j: (0, j)))
        args.append(bias.reshape(1, n))
    return _call(kernel, (m // TM, n // TN), in_specs,
                 [pl.BlockSpec((TM, TN), lambda i, j: (i, j))],
                 [jax.ShapeDtypeStruct((m, n), out_dtype)], args, name, cast=cast)


def _mm_residual_kernel(a_ref, w_ref, x_ref, g_ref, o_ref):
    o_ref[...] = x_ref[...] + g_ref[...] * _dot(a_ref[...], w_ref[...])


def _mm_residual_ksplit_kernel(a_ref, w_ref, x_ref, g_ref, o_ref, acc_ref, *, nk):
    kk = pl.program_id(1)
    j = pl.program_id(2)

    @pl.when(kk == 0)
    def _():
        acc_ref[j] = _dot(a_ref[...], w_ref[...])

    @pl.when((kk > 0) & (kk < nk - 1))
    def _():
        acc_ref[j] += _dot(a_ref[...], w_ref[...])

    @pl.when(kk == nk - 1)
    def _():
        o_ref[...] = x_ref[...] + g_ref[...] * (acc_ref[j] + _dot(a_ref[...], w_ref[...]))


def _matmul_residual(a, w, x, mods, seg0, rows_per_seg, g_chunk, name, cast=None):
    w_arr, layer = w
    m, k = a.shape
    n = w_arr.shape[2]
    tiles_per_seg = rows_per_seg // TM
    seg = lambda i: seg0 + i // tiles_per_seg
    if k <= TK:
        nb = n // TN
        return _call(
            _mm_residual_kernel, (m // TM, nb),
            [
                pl.BlockSpec((TM, k), lambda i, j: (i, 0)),
                pl.BlockSpec((None, k, TN), lambda i, j: (layer, 0, j)),
                pl.BlockSpec((TM, TN), lambda i, j: (i, j)),
                pl.BlockSpec((None, 1, TN), lambda i, j: (seg(i), 0, g_chunk * nb + j)),
            ],
            [pl.BlockSpec((TM, TN), lambda i, j: (i, j))],
            [jax.ShapeDtypeStruct((m, n), F32)], (a, w_arr, x, mods), name, cast=cast)
    nk = k // TK
    tn = TN_KSPLIT
    nb = n // tn
    out_col = lambda kk, j: jnp.where(kk == nk - 1, j, 0)
    return _call(
        functools.partial(_mm_residual_ksplit_kernel, nk=nk), (m // TM, nk, nb),
        [
            pl.BlockSpec((TM, TK), lambda i, kk, j: (i, kk)),
            pl.BlockSpec((None, TK, tn), lambda i, kk, j: (layer, kk, j)),
            pl.BlockSpec((TM, tn), lambda i, kk, j: (i, out_col(kk, j))),
            pl.BlockSpec((None, 1, tn), lambda i, kk, j: (seg(i), 0, g_chunk * nb + out_col(kk, j))),
        ],
        [pl.BlockSpec((TM, tn), lambda i, kk, j: (i, out_col(kk, j)))],
        [jax.ShapeDtypeStruct((m, n), F32)], (a, w_arr, x, mods), name,
        scratch_shapes=[pltpu.VMEM((nb, TM, tn), F32)], cast=cast)


def _merge_kernel(hy_ref, at_ref, gm_ref, whv_ref, wat_ref, wgm_ref, ga_ref, gb_ref, gc_ref, o_ref):
    merged = (ga_ref[...] * _dot(hy_ref[...], whv_ref[...])
              + gb_ref[...] * _dot(at_ref[...], wat_ref[...])
              + gc_ref[...] * _dot(gm_ref[...], wgm_ref[...]))
    o_ref[...] = merged.astype(o_ref.dtype)


def _merge(hy, at, gm, w_hy, w_at, w_gm, gates):
    m = hy.shape[0]
    n = w_hy[0].shape[2]
    tn = TN_MERGE
    nb = n // tn
    row = lambda width: pl.BlockSpec((TM, width), lambda i, j: (i, 0))
    col = lambda w: pl.BlockSpec((None, w[0].shape[1], tn), lambda i, j: (w[1], 0, j))
    gate = lambda b: pl.BlockSpec((TM, tn), lambda i, j: (i, b * nb + j))
    return _call(
        _merge_kernel, (m // TM, nb),
        [row(hy.shape[1]), row(at.shape[1]), row(gm.shape[1]), col(w_hy), col(w_at), col(w_gm),
         gate(0), gate(1), gate(2)],
        [pl.BlockSpec((TM, tn), lambda i, j: (i, j))],
        [jax.ShapeDtypeStruct((m, n), BF16)],
        (hy, at, gm, w_hy[0], w_at[0], w_gm[0], gates, gates, gates), "merge")[0]


def _dft_tables(L):
    k = np.arange(L)
    ang = ((k[:, None] * k[None, :]) % (2 * L)) * (np.pi / L)
    cos, sin = np.cos(ang), np.sin(ang)
    alt = np.where(k % 2 == 0, 1.0, -1.0)
    fwd = np.concatenate([cos, -sin], axis=0)
    inv = np.concatenate([cos, -sin], axis=1)
    fwd[L, :] = alt
    inv[:, L] = alt
    const = lambda a: jnp.asarray(a.astype(np.float32).astype(BF16))
    return {'cos': const(cos), 'sin': const(sin), 'fwd': const(fwd), 'inv': const(inv)}


def _filter_features(L):
    t = np.linspace(0.0, 1.0, L)
    w = 2.0 * np.pi * np.arange(L) / L
    bands = np.linspace(1e-4, POS_BANDS - 1, POS_BANDS)
    z = np.concatenate([t[:, None], np.cos(w[:, None] * bands), -np.sin(w[:, None] * bands)], axis=-1)
    z = np.pad(z, ((0, 0), (0, V7X_LANES - POS_DIM)))
    return jnp.asarray(z.astype(np.float32)), jnp.asarray(t[:, None].astype(np.float32))


def _filter_kernel(z_ref, t_ref, w1_ref, b1_ref, f1_ref, w2_ref, b2_ref, f2_ref,
                   w3f_ref, w3b_ref, dl_ref, cos_ref, sin_ref, o_ref, *, L):
    z_hi, z_lo = _split(z_ref[...])
    a = jnp.sin(f1_ref[...] * (_dot3(z_hi, z_lo, *_split(w1_ref[...])) + b1_ref[...]))
    a = jnp.sin(f2_ref[...] * (_dot3(*_split(a), *_split(w2_ref[...])) + b2_ref[...]))
    a_hi, a_lo = _split(a)
    decay = jnp.exp(-t_ref[...] * dl_ref[...])
    row = lax.broadcasted_iota(jnp.int32, (L, 1), 0)
    alt = jnp.where(row % 2 == 0, 1.0, -1.0).astype(F32)
    for o in range(HY_ORDER):
        fwd = _dot3(a_hi, a_lo, *_split(w3f_ref[o])) * decay
        bwd = _dot3(a_hi, a_lo, *_split(w3b_ref[o])) * decay
        s = fwd + jnp.where(row == 0, 0.0, bwd)
        d = bwd - fwd
        hr = _dot(cos_ref[...], s.astype(BF16)) * (1.0 / L)
        hi = _dot(sin_ref[...], d.astype(BF16)) * (1.0 / L)
        nyq = jnp.sum(alt * s, axis=0, keepdims=True) * (0.5 / L)
        o_ref[o, 0] = jnp.where(row == 0, 0.5 * hr, hr)
        o_ref[o, 1] = hi
        o_ref[o, 2] = jnp.where(row == 0, nyq, hr)


def _filter_spectra(L, p, tables):
    z, t = _filter_features(L)
    w1 = jnp.pad(p['flt_w1'], ((0, V7X_LANES - POS_DIM), (0, 0)))
    w3 = p['flt_w3'].reshape(FILT_HID, HY_ORDER, 2, D_H)
    w3f = jnp.transpose(w3[:, :, 0], (1, 0, 2))
    w3b = jnp.transpose(w3[:, :, 1], (1, 0, 2))
    deltas = jnp.asarray(np.abs(np.linspace(DECAY_MIN, DECAY_MAX, D_H)).astype(np.float32).reshape(1, D_H))
    full = lambda shape: pl.BlockSpec(shape, lambda c: (0,) * len(shape))
    return pl.pallas_call(
        functools.partial(_filter_kernel, L=L),
        grid=(D_H // HY_TC,),
        in_specs=[
            full((L, V7X_LANES)), full((L, 1)),
            full((V7X_LANES, FILT_HID)), full((1, FILT_HID)), full((1, FILT_HID)),
            full((FILT_HID, FILT_HID)), full((1, FILT_HID)), full((1, FILT_HID)),
            pl.BlockSpec((HY_ORDER, FILT_HID, HY_TC), lambda c: (0, 0, c)),
            pl.BlockSpec((HY_ORDER, FILT_HID, HY_TC), lambda c: (0, 0, c)),
            pl.BlockSpec((1, HY_TC), lambda c: (0, c)),
            full((L, L)), full((L, L)),
        ],
        out_specs=pl.BlockSpec((HY_ORDER, 3, L, HY_TC), lambda c: (0, 0, 0, c)),
        out_shape=jax.ShapeDtypeStruct((HY_ORDER, 3, L, D_H), F32),
        compiler_params=_params(("arbitrary",)),
        name=f"hyena_filter_{L}",
    )(z, t, w1, p['flt_b1'].reshape(1, -1), p['flt_f1'].reshape(1, -1),
      p['flt_w2'], p['flt_b2'].reshape(1, -1), p['flt_f2'].reshape(1, -1),
      w3f, w3b, deltas, tables['cos'], tables['sin'])


def _hyena_kernel(v_ref, x1_ref, x2_ref, wv_ref, wx1_ref, wx2_ref, bv_ref, bx1_ref, bx2_ref,
                  tab_ref, skip_ref, fwd_ref, inv_ref, o_ref, *, L):
    row = lax.broadcasted_iota(jnp.int32, (L, 1), 0)

    def short_conv(u_ref, rows, w_ref, b_ref):
        u = u_ref[rows, :]
        prev = jnp.where(row == 0, 0.0, pltpu.roll(u, 1, 0))
        nxt = jnp.where(row == L - 1, 0.0, pltpu.roll(u, L - 1, 0))
        return prev * w_ref[0:1, :] + u * w_ref[1:2, :] + nxt * w_ref[2:3, :] + b_ref[...]

    seqs = [slice(s * L, (s + 1) * L) for s in range(v_ref.shape[0] // L)]
    zs = [short_conv(v_ref, rows, wv_ref, bv_ref) for rows in seqs]
    gates = [(short_conv(x1_ref, rows, wx1_ref, bx1_ref), short_conv(x2_ref, rows, wx2_ref, bx2_ref))
             for rows in seqs]
    for o in range(HY_ORDER):
        p, q, r = tab_ref[o, 0], tab_ref[o, 1], tab_ref[o, 2]
        specs = [_dot(fwd_ref[...], z.astype(BF16)) for z in zs]
        prods = [(spec[:L] * p - spec[L:] * q, spec[:L] * q + spec[L:] * r) for spec in specs]
        ys = [_dot(inv_ref[...], jnp.concatenate([ya, yb], axis=0).astype(BF16)) for ya, yb in prods]
        zs = [g[o] * (y + skip_ref[o:o + 1, :] * z) for g, y, z in zip(gates, ys, zs)]
    for rows, z in zip(seqs, zs):
        o_ref[rows, :] = z.astype(o_ref.dtype)


def _hyena(proj, L, spectra, tables, p, cast):
    m = proj.shape[0]
    tc = min(D_H, HY_TILE_ELEMS // L)
    nc = D_H // tc
    rows = (HY_SEQS if tc < D_H else 1) * L
    u = lambda part: pl.BlockSpec((rows, tc), lambda c, b: (b, part * nc + c))
    cw = lambda part: pl.BlockSpec((3, tc), lambda c, b: (0, part * nc + c))
    cb = lambda part: pl.BlockSpec((1, tc), lambda c, b: (0, part * nc + c))
    conv_b = p['conv_b'].reshape(1, -1)
    return _call(
        functools.partial(_hyena_kernel, L=L), (nc, m // rows),
        [u(0), u(1), u(2), cw(0), cw(1), cw(2), cb(0), cb(1), cb(2),
         pl.BlockSpec((HY_ORDER, 3, L, tc), lambda c, b: (0, 0, 0, c)),
         pl.BlockSpec((HY_ORDER, tc), lambda c, b: (0, c)),
         pl.BlockSpec((2 * L, L), lambda c, b: (0, 0)),
         pl.BlockSpec((L, 2 * L), lambda c, b: (0, 0))],
        [pl.BlockSpec((rows, tc), lambda c, b: (b, c))],
        [jax.ShapeDtypeStruct((m, D_H), BF16)],
        (proj, proj, proj, p['conv_w'], p['conv_w'], p['conv_w'], conv_b, conv_b, conv_b,
         spectra, p['hy_skip'], tables['fwd'], tables['inv']),
        f"hyena_{L}", cast=cast)


def _gmlp_kernel(u_ref, v_ref, lg_ref, lb_ref, ws_ref, bs_ref, o_ref):
    v = v_ref[...]
    mu = jnp.mean(v, axis=-1, keepdims=True)
    vc = v - mu
    var = jnp.mean(vc * vc, axis=-1, keepdims=True)
    vn = (vc * lax.rsqrt(var + EPS) * lg_ref[...] + lb_ref[...]).astype(BF16)
    for n in range(GM_ROWS // CHUNK):
        rows = slice(n * CHUNK, (n + 1) * CHUNK)
        for g in range(N_GM_GROUPS):
            cols = slice(g * CHUNK, (g + 1) * CHUNK)
            mixed = _dot(ws_ref[g], vn[rows, cols]) + bs_ref[:, g:g + 1]
            o_ref[rows, cols] = (u_ref[rows, cols] * mixed).astype(o_ref.dtype)


def _gmlp(proj, p):
    m = proj.shape[0]
    cu = COL_GM // D_G
    return pl.pallas_call(
        _gmlp_kernel,
        grid=(m // GM_ROWS,),
        in_specs=[
            pl.BlockSpec((GM_ROWS, D_G), lambda i: (i, cu)),
            pl.BlockSpec((GM_ROWS, D_G), lambda i: (i, cu + 1)),
            pl.BlockSpec((1, D_G), lambda i: (0, 0)),
            pl.BlockSpec((1, D_G), lambda i: (0, 0)),
            pl.BlockSpec((N_GM_GROUPS, CHUNK, CHUNK), lambda i: (0, 0, 0)),
            pl.BlockSpec((CHUNK, N_GM_GROUPS), lambda i: (0, 0)),
        ],
        out_specs=pl.BlockSpec((GM_ROWS, D_G), lambda i: (i, 0)),
        out_shape=jax.ShapeDtypeStruct((m, D_G), BF16),
        compiler_params=_params(("arbitrary",)),
        name="gmlp",
    )(proj, proj, p['gm_ln_g'].reshape(1, -1), p['gm_ln_b'].reshape(1, -1),
      p['gm_ws'].astype(BF16), p['gm_bs'].T)


def _rms(x, g):
    return x * lax.rsqrt(jnp.mean(x * x, axis=-1, keepdims=True) + EPS) * g


def _rope(x, cos, sin_signed, lane):
    partner = jnp.where((lane % 64) < ROT_FREQS,
                        pltpu.roll(x, HEAD_DIM - ROT_FREQS, 1), pltpu.roll(x, ROT_FREQS, 1))
    return x * cos + partner * sin_signed


def _softmax_pv(problems):
    dims = (((1,), (1,)), ((), ()))
    outs = []
    for first in range(0, len(problems), ATT_GROUP):
        group = problems[first:first + ATT_GROUP]
        ss = [lax.dot_general(q, k, dims, preferred_element_type=F32) * ATT_SCALE for q, k, _, _, _ in group]
        ss = [s if pr[3] is None else jnp.where(pr[3], s, NEG) for s, pr in zip(ss, group)]
        ms = [jnp.maximum(jnp.max(s, axis=-1, keepdims=True), pr[4]) for s, pr in zip(ss, group)]
        es = [jnp.exp(s - m) for s, m in zip(ss, ms)]
        dens = [jnp.sum(e, axis=-1, keepdims=True) + jnp.exp(pr[4] - m) for e, m, pr in zip(es, ms, group)]
        outs += [_dot(e.astype(BF16), pr[2]) / den for e, den, pr in zip(es, dens, group)]
    return outs


def _sink_column(sink_ref, rows):
    return jnp.concatenate(
        [jnp.broadcast_to(sink_ref[g:g + 1, 0:1], (rows, 1)) for g in range(GQA_GROUP)], axis=0)


def _ctx_att_kernel(qa_ref, qb_ref, k_ref, v_ref, qg_ref, kg_ref, sink_ref, o_ref, ko_ref, vo_ref, *,
                    L, layer, zero_other_layers):
    if zero_other_layers:
        for d in range(ko_ref.shape[0]):
            if d != layer:
                ko_ref[d] = jnp.zeros(ko_ref.shape[1:], ko_ref.dtype)
                vo_ref[d] = jnp.zeros(vo_ref.shape[1:], vo_ref.dtype)
        ko_ref, vo_ref = ko_ref.at[layer], vo_ref.at[layer]
    qw = GQA_GROUP * HEAD_DIM
    q_halves = (qa_ref, qb_ref)
    chunks = range(0, GQA_GROUP * L, ATT_ROWS)
    problems = []
    for h in range(N_KV_HEADS):
        q_ref = q_halves[h // 2]
        q0 = (h % 2) * qw
        q_all = jnp.concatenate(
            [_rms(q_ref[:, q0 + g * HEAD_DIM:q0 + (g + 1) * HEAD_DIM], qg_ref[...]) for g in range(GQA_GROUP)],
            axis=0).astype(BF16)
        cols = slice(h * HEAD_DIM, (h + 1) * HEAD_DIM)
        k = _rms(k_ref[:, cols], kg_ref[...])
        v = v_ref[:, cols]
        ko_ref[:, h, :] = k
        vo_ref[:, h, :] = v
        k, v = k.astype(BF16), v.astype(BF16)
        sink = _sink_column(sink_ref.at[h], L)
        problems += [(q_all[r:r + ATT_ROWS], k, v, None, sink[r:r + ATT_ROWS]) for r in chunks]
    outs = _softmax_pv(problems)
    for h in range(N_KV_HEADS):
        out = jnp.concatenate(outs[h * len(chunks):(h + 1) * len(chunks)], axis=0)
        for g in range(GQA_GROUP):
            o_ref[:, h * qw + g * HEAD_DIM:h * qw + (g + 1) * HEAD_DIM] = out[g * L:(g + 1) * L].astype(o_ref.dtype)


def _sink_rows(sink):
    s = sink.reshape(N_KV_HEADS, GQA_GROUP, 1)
    s = jnp.pad(s, ((0, 0), (0, PAD_ROWS - GQA_GROUP), (0, 0)))
    return jnp.broadcast_to(s, (N_KV_HEADS, PAD_ROWS, V7X_LANES))


def _context_attention(proj, L, p, layer, depth, kv_prev):
    m = proj.shape[0]
    b = m // L
    half = ATT_W // 2
    kv_shape = jax.ShapeDtypeStruct((b, depth, L, N_KV_HEADS, HEAD_DIM), F32)
    if kv_prev is None:
        kv_spec = pl.BlockSpec((None, depth, L, N_KV_HEADS, HEAD_DIM), lambda i: (i, 0, 0, 0, 0))
    else:
        kv_spec = pl.BlockSpec((None, None, L, N_KV_HEADS, HEAD_DIM), lambda i: (i, layer, 0, 0, 0))
    in_specs = [
        pl.BlockSpec((L, half), lambda i: (i, COL_Q // half)),
        pl.BlockSpec((L, half), lambda i: (i, COL_Q // half + 1)),
        pl.BlockSpec((L, KV_W), lambda i: (i, COL_K // KV_W)),
        pl.BlockSpec((L, KV_W), lambda i: (i, COL_V // KV_W)),
        pl.BlockSpec((1, HEAD_DIM), lambda i: (0, 0)),
        pl.BlockSpec((1, HEAD_DIM), lambda i: (0, 0)),
        pl.BlockSpec((N_KV_HEADS, PAD_ROWS, V7X_LANES), lambda i: (0, 0, 0)),
    ]
    args = [proj, proj, proj, proj, p['q_norm_g'].reshape(1, -1), p['k_norm_g'].reshape(1, -1),
            _sink_rows(p['sink'])]
    kernel = functools.partial(_ctx_att_kernel, L=L, layer=layer, zero_other_layers=kv_prev is None)
    aliases = {}
    if kv_prev is not None:
        n_in = len(in_specs)
        in_specs += [pl.BlockSpec(memory_space=pl.ANY)] * 2
        args += list(kv_prev)
        aliases = {n_in: 1, n_in + 1: 2}
        inner = kernel
        kernel = lambda *refs: inner(*refs[:n_in], *refs[n_in + 2:])
    return pl.pallas_call(
        kernel,
        grid=(b,),
        in_specs=in_specs,
        out_specs=[pl.BlockSpec((L, ATT_W), lambda i: (i, 0)), kv_spec, kv_spec],
        out_shape=[jax.ShapeDtypeStruct((m, ATT_W), BF16), kv_shape, kv_shape],
        input_output_aliases=aliases,
        compiler_params=_params(("arbitrary",)),
        name="context_attention",
    )(*args)


def _rope_tables(L):
    rows = L // GRID_W
    row = np.repeat(np.arange(rows), GRID_W)
    col = np.tile(np.arange(GRID_W), rows)
    inv = ROPE_BASE ** (-np.arange(ROT_FREQS) / ROT_FREQS)
    pos = np.stack([row, col], axis=-1).astype(np.float64)
    ang = pos[:, :, None] * inv
    cos, sin = np.cos(ang), np.sin(ang)
    cos_t = np.stack([cos, cos], axis=2).reshape(L, HEAD_DIM)
    sin_t = np.stack([-sin, sin], axis=2).reshape(L, HEAD_DIM)
    return jnp.asarray(cos_t.astype(np.float32)), jnp.asarray(sin_t.astype(np.float32))


def _lat_att_kernel(q_ref, k_ref, v_ref, kc_ref, vc_ref, cos_ref, sin_ref, qg_ref, kg_ref, sink_ref,
                    o_ref, *, nb):
    lane = lax.broadcasted_iota(jnp.int32, (1, HEAD_DIM), 1)
    k_seq = _rope(_rms(k_ref[...], kg_ref[...]), cos_ref[...], sin_ref[...], lane).astype(BF16)
    v_seq = v_ref[...].astype(BF16)
    k_ctx = kc_ref[...].astype(BF16)
    v_ctx = vc_ref[...].astype(BF16)
    n_ctx = k_ctx.shape[0]
    sink_col = _sink_column(sink_ref, BLOCK)

    def band_mask(has_prev, has_next):
        n_keys = n_ctx + (1 + has_prev + has_next) * BLOCK
        shape = (GQA_GROUP * BLOCK, n_keys)
        i = lax.broadcasted_iota(jnp.int32, shape, 0) % BLOCK
        c = lax.broadcasted_iota(jnp.int32, shape, 1)
        ok = None
        if has_prev:
            ok = (c < n_ctx) | (c >= n_ctx + BLOCK) | (c - n_ctx >= i)
        if has_next:
            start = n_keys - BLOCK
            nxt = (c < start) | (c - start <= i)
            ok = nxt if ok is None else ok & nxt
        return ok

    masks = {}
    problems = []
    for n in range(nb):
        lo, hi = max(n - 1, 0), min(n + 1, nb - 1)
        shape_key = (lo < n, hi > n)
        if shape_key not in masks:
            masks[shape_key] = band_mask(*shape_key)
        rows = slice(n * BLOCK, (n + 1) * BLOCK)
        cos, sin = cos_ref[rows, :], sin_ref[rows, :]
        q_all = jnp.concatenate(
            [_rope(_rms(q_ref[rows, g * HEAD_DIM:(g + 1) * HEAD_DIM], qg_ref[...]), cos, sin, lane)
             for g in range(GQA_GROUP)], axis=0).astype(BF16)
        band = slice(lo * BLOCK, (hi + 1) * BLOCK)
        k_all = jnp.concatenate([k_ctx, k_seq[band]], axis=0)
        v_all = jnp.concatenate([v_ctx, v_seq[band]], axis=0)
        problems.append((q_all, k_all, v_all, masks[shape_key], sink_col))
    for n, out in enumerate(_softmax_pv(problems)):
        rows = slice(n * BLOCK, (n + 1) * BLOCK)
        for g in range(GQA_GROUP):
            o_ref[rows, g * HEAD_DIM:(g + 1) * HEAD_DIM] = out[g * BLOCK:(g + 1) * BLOCK].astype(o_ref.dtype)


def _window_attention(proj, L, cache_k, cache_v, layer, p):
    m = proj.shape[0]
    b = m // L
    nb = L // BLOCK
    n_ctx = cache_k.shape[2]
    qw = GQA_GROUP * HEAD_DIM
    cos_t, sin_t = _rope_tables(L)
    seq = lambda col0: pl.BlockSpec((L, HEAD_DIM), lambda i, h: (i, col0 // HEAD_DIM + h))
    cache = pl.BlockSpec((None, None, n_ctx, HEAD_DIM), lambda i, h: (i, layer, 0, h))
    table = pl.BlockSpec((L, HEAD_DIM), lambda i, h: (0, 0))
    gain = pl.BlockSpec((1, HEAD_DIM), lambda i, h: (0, 0))
    return pl.pallas_call(
        functools.partial(_lat_att_kernel, nb=nb),
        grid=(b, N_KV_HEADS),
        in_specs=[
            pl.BlockSpec((L, qw), lambda i, h: (i, COL_Q // qw + h)),
            seq(COL_K), seq(COL_V), cache, cache, table, table, gain, gain,
            pl.BlockSpec((None, PAD_ROWS, V7X_LANES), lambda i, h: (h, 0, 0)),
        ],
        out_specs=pl.BlockSpec((L, qw), lambda i, h: (i, h)),
        out_shape=jax.ShapeDtypeStruct((m, ATT_W), BF16),
        compiler_params=_params(("arbitrary", "arbitrary")),
        name="window_attention",
    )(proj, proj, proj, cache_k, cache_v, cos_t, sin_t,
      p['q_norm_g'].reshape(1, -1), p['k_norm_g'].reshape(1, -1), _sink_rows(p['sink']))


def _mixer_branches(proj, L, p, tables, cache, layer, cast, depth=None, kv_prev=None):
    hy, w_cast = _hyena(proj, L, _filter_spectra(L, p, tables), tables, p, cast)
    gm = _gmlp(proj, p)
    if cache is None:
        att, k_new, v_new = _context_attention(proj, L, p, layer, depth, kv_prev)
    else:
        att = _window_attention(proj, L, cache[0], cache[1], layer, p)
        k_new = v_new = None
    return (hy, att, gm), w_cast, k_new, v_new


_LAYER_PARAMS = ('norm1_g', 'norm2_g', 'conv_w', 'conv_b', 'flt_w1', 'flt_b1', 'flt_f1', 'flt_w2',
                 'flt_b2', 'flt_f2', 'flt_w3', 'hy_skip', 'q_norm_g', 'k_norm_g', 'sink',
                 'gm_ln_g', 'gm_ln_b', 'gm_ws', 'gm_bs', 'b_gate')


def kernel(x_prompt, x_sample, cache_k, cache_v, c, c_ctx, w_mod, b_mod, norm1_g, norm2_g,
           w_in, conv_w, conv_b, flt_w1, flt_b1, flt_f1, flt_w2, flt_b2, flt_f2, flt_w3,
           hy_skip, q_norm_g, k_norm_g, sink, gm_ln_g, gm_ln_b, gm_ws, gm_bs,
           w_p_hy, w_p_at, w_p_gm, w_gate, b_gate, w_out, w_up, w_down):
    args = dict(locals())
    batch, seq, d = x_prompt.shape
    dec_batch, dec_seq, _ = x_sample.shape
    depth = w_mod.shape[0]
    assert 1 + dec_batch <= PAD_ROWS

    c_rows = jnp.concatenate(
        [c_ctx[None, :], c, jnp.zeros((PAD_ROWS - 1 - dec_batch, d), F32)], axis=0)
    mods = _modulation(c_rows, w_mod, b_mod).reshape(depth, PAD_ROWS, 1, 6 * d)

    tables = {L: _dft_tables(L) for L in (seq, dec_seq)}
    cache = (cache_k.reshape(dec_batch, depth, -1, KV_W), cache_v.reshape(dec_batch, depth, -1, KV_W))

    y_p = x_prompt.reshape(batch * seq, d)
    y_s = x_sample.reshape(dec_batch * dec_seq, d)
    new_kv = None
    ctx_seg = (0, batch * seq)
    lat_seg = (1, dec_seq)
    w_in_b = (w_in[:1].astype(BF16), 0)
    for l in range(depth):
        p = {name: args[name][l] for name in _LAYER_PARAMS}
        m_l = mods[l]
        h_c = _normmod(y_p, p['norm1_g'], m_l, *ctx_seg, sh_chunk=0, sc_chunk=1)
        h_s = _normmod(y_s, p['norm1_g'], m_l, *lat_seg, sh_chunk=0, sc_chunk=1)
        proj_c, w_gate_b = _matmul(h_c, w_in_b, _mm_kernel, F32, "in_proj", cast=(w_gate, l))
        proj_s, w_p_at_b = _matmul(h_s, w_in_b, _mm_kernel, F32, "in_proj", cast=(w_p_at, l))
        gates_c, w_up_b = _matmul(h_c, (w_gate_b, 0), _mm_sigmoid_kernel, BF16, "gate_proj",
                                  bias=p['b_gate'], cast=(w_up, l))
        gates_s, w_out_b = _matmul(h_s, (w_gate_b, 0), _mm_sigmoid_kernel, BF16, "gate_proj",
                                   bias=p['b_gate'], cast=(w_out, l))
        br_c, w_p_hy_b, *new_kv = _mixer_branches(proj_c, seq, p, tables[seq], None, l, (w_p_hy, l),
                                                  depth, new_kv)
        br_s, w_p_gm_b, _, _ = _mixer_branches(proj_s, dec_seq, p, tables[dec_seq], cache, l, (w_p_gm, l))
        w_branch = ((w_p_hy_b, 0), (w_p_at_b, 0), (w_p_gm_b, 0))
        merged_c = _merge(*br_c, *w_branch, gates_c)
        merged_s = _merge(*br_s, *w_branch, gates_s)
        y_p, = _matmul_residual(merged_c, (w_out_b, 0), y_p, m_l, *ctx_seg, 2, "out_proj")
        y_s, = _matmul_residual(merged_s, (w_out_b, 0), y_s, m_l, *lat_seg, 2, "out_proj")
        h_c = _normmod(y_p, p['norm2_g'], m_l, *ctx_seg, sh_chunk=3, sc_chunk=4)
        h_s = _normmod(y_s, p['norm2_g'], m_l, *lat_seg, sh_chunk=3, sc_chunk=4)
        act_c, w_down_b = _matmul(h_c, (w_up_b, 0), _mm_relu2_kernel, BF16, "mlp_up", cast=(w_down, l))
        act_s, = _matmul(h_s, (w_up_b, 0), _mm_relu2_kernel, BF16, "mlp_up")
        if l + 1 < depth:
            y_p, w_in_next = _matmul_residual(act_c, (w_down_b, 0), y_p, m_l, *ctx_seg, 5, "mlp_down",
                                              cast=(w_in, l + 1))
            w_in_b = (w_in_next, 0)
        else:
            y_p, = _matmul_residual(act_c, (w_down_b, 0), y_p, m_l, *ctx_seg, 5, "mlp_down")
        y_s, = _matmul_residual(act_s, (w_down_b, 0), y_s, m_l, *lat_seg, 5, "mlp_down")
    return (y_p.reshape(batch, seq, d), y_s.reshape(dec_batch, dec_seq, d),
            new_kv[0], new_kv[1])
```

```python
import functools
import math
from typing import Callable, NamedTuple

import jax
import jax.numpy as jnp
import numpy as np
from jax import lax
from jax.experimental import pallas as pl
from jax.experimental.pallas import tpu as pltpu

F32 = jnp.float32
BF16 = jnp.bfloat16

D_MODEL = 4096
HEAD_DIM = 128
N_Q_HEADS = 16
N_KV_HEADS = 4
GQA_GROUP = 4
ATT_W = N_Q_HEADS * HEAD_DIM
KV_W = N_KV_HEADS * HEAD_DIM
GRID_W = 64
WINDOW = 128
BLOCK = 128
ROPE_BASE = 10000.0
ROT_FREQS = 32
D_H = 1024
HY_ORDER = 2
POS_BANDS = 16
POS_DIM = 1 + 2 * POS_BANDS
FILT_HID = 64
DECAY_MIN = math.log(1e-2) / 1.5
DECAY_MAX = math.log(1e-2) / 0.3
D_G = 1024
CHUNK = 128
N_GM_GROUPS = 8
IN_COLS = 3 * D_H + ATT_W + 2 * KV_W + 2 * D_G
assert WINDOW == BLOCK
EPS = 1e-6
NEG = -1e30
ATT_SCALE = HEAD_DIM ** -0.5

COL_Q = 3 * D_H
COL_K = COL_Q + ATT_W
COL_V = COL_K + KV_W
COL_GM = COL_V + KV_W

V7X_LANES = 128
V7X_VMEM_BYTES = 64 * 1024 * 1024
VMEM_LIMIT = V7X_VMEM_BYTES - 4 * 1024 * 1024

TM = 1024
TN = 1024
TN_KSPLIT = 512
TN_MERGE = 1024
TK = 4096
NORM_ROWS = 512
HY_TC = 256
HY_TILE_ELEMS = 256 * 1024
HY_SEQS = 2
ATT_ROWS = 1024
ATT_GROUP = 8
GM_ROWS = 512
MOD_TN = 512
PAD_ROWS = 8
BF16_SUBLANES = 16


def _params(sem):
    return pltpu.CompilerParams(dimension_semantics=sem, vmem_limit_bytes=VMEM_LIMIT)


def _split(a):
    hi = a.astype(BF16)
    lo = (a - hi.astype(F32)).astype(BF16)
    return hi, lo


def _dot(a, b):
    return jnp.dot(a, b, preferred_element_type=F32)


def _dot3(a_hi, a_lo, b_hi, b_lo):
    return _dot(a_hi, b_hi) + _dot(a_lo, b_hi) + _dot(a_hi, b_lo)


def _mod_kernel(c_ref, w_ref, b_ref, o_ref):
    c = c_ref[...]
    x = c * jax.nn.sigmoid(c)
    o_ref[...] = _dot(x.astype(BF16), w_ref[...].astype(BF16)) + b_ref[...]


def _modulation(c_rows, w_mod, b_mod, layer, n):
    depth, d, n_all = w_mod.shape
    return pl.pallas_call(
        _mod_kernel,
        grid=(n // MOD_TN,),
        in_specs=[
            pl.BlockSpec((PAD_ROWS, d), lambda j: (0, 0)),
            pl.BlockSpec((None, d, MOD_TN), lambda j: (layer, 0, j)),
            pl.BlockSpec((None, 1, MOD_TN), lambda j: (layer, 0, j)),
        ],
        out_specs=pl.BlockSpec((PAD_ROWS, MOD_TN), lambda j: (0, j)),
        out_shape=jax.ShapeDtypeStruct((PAD_ROWS, n), F32),
        compiler_params=_params(("arbitrary",)),
        name="modulation",
    )(c_rows, w_mod, b_mod.reshape(depth, 1, n_all))


def _normmod_kernel(x_ref, g_ref, sc_ref, sh_ref, o_ref):
    x = x_ref[...]
    y = x * lax.rsqrt(jnp.mean(x * x, axis=-1, keepdims=True) + EPS) * g_ref[...]
    o_ref[...] = (y * (1.0 + sc_ref[...]) + sh_ref[...]).astype(o_ref.dtype)


def _normmod(x, g, mods, seg0, rows_per_seg, sh_chunk, sc_chunk):
    m, d = x.shape
    tiles_per_seg = rows_per_seg // NORM_ROWS
    seg = lambda i: seg0 + i // tiles_per_seg
    return pl.pallas_call(
        _normmod_kernel,
        grid=(m // NORM_ROWS,),
        in_specs=[
            pl.BlockSpec((NORM_ROWS, d), lambda i: (i, 0)),
            pl.BlockSpec((1, d), lambda i: (0, 0)),
            pl.BlockSpec((None, 1, d), lambda i: (seg(i), 0, sc_chunk)),
            pl.BlockSpec((None, 1, d), lambda i: (seg(i), 0, sh_chunk)),
        ],
        out_specs=pl.BlockSpec((NORM_ROWS, d), lambda i: (i, 0)),
        out_shape=jax.ShapeDtypeStruct((m, d), BF16),
        compiler_params=_params(("arbitrary",)),
        name="normmod",
    )(x, g.reshape(1, d), mods, mods)


def _mm_kernel(x_ref, w_ref, o_ref):
    o_ref[...] = _dot(x_ref[...], w_ref[...]).astype(o_ref.dtype)


def _mm_sigmoid_kernel(x_ref, w_ref, b_ref, o_ref):
    t = _dot(x_ref[...], w_ref[...]) + b_ref[...]
    o_ref[...] = (0.5 * jnp.tanh(0.5 * t) + 0.5).astype(o_ref.dtype)


def _mm_relu2_kernel(x_ref, w_ref, o_ref):
    a = jnp.maximum(_dot(x_ref[...], w_ref[...]), 0.0)
    o_ref[...] = (a * a).astype(o_ref.dtype)


class _SideJob(NamedTuple):
    in_specs: list
    out_specs: list
    out_shape: list
    args: tuple
    run: Callable


def _step_index(grid, ids):
    step = ids[0]
    for extent, idx in zip(grid[1:], ids[1:]):
        step = step * extent + idx
    return step


def _cast_job(grid, src, layer):
    _, k2, n2 = src.shape
    steps = math.prod(grid)
    slabs = min(1 << (steps.bit_length() - 1), k2 // BF16_SUBLANES)
    rows = k2 // slabs
    slab = lambda *ids: jnp.minimum(_step_index(grid, ids), slabs - 1)

    def run(ins, outs):
        outs[0][...] = ins[0][...].astype(BF16)

    return _SideJob([pl.BlockSpec((None, rows, n2), lambda *ids: (layer, slab(*ids), 0))],
                    [pl.BlockSpec((None, rows, n2), lambda *ids: (0, slab(*ids), 0))],
                    [jax.ShapeDtypeStruct((1, k2, n2), BF16)], (src,), run)


def _mod_job(grid, c_rows, w_mod, b_mod, layer, col0=0):
    depth, d, n_all = w_mod.shape
    n = n_all - col0
    tn = n // math.prod(grid)
    assert tn * math.prod(grid) == n and tn % V7X_LANES == 0 and col0 % tn == 0
    tile = lambda *ids: _step_index(grid, ids)
    src = lambda *ids: col0 // tn + tile(*ids)
    return _SideJob(
        [pl.BlockSpec((PAD_ROWS, d), lambda *ids: (0, 0)),
         pl.BlockSpec((None, d, tn), lambda *ids: (layer, 0, src(*ids))),
         pl.BlockSpec((None, 1, tn), lambda *ids: (layer, 0, src(*ids)))],
        [pl.BlockSpec((PAD_ROWS, tn), lambda *ids: (0, tile(*ids)))],
        [jax.ShapeDtypeStruct((PAD_ROWS, n), F32)],
        (c_rows, w_mod, b_mod.reshape(depth, 1, n_all)),
        lambda ins, outs: _mod_kernel(*ins, *outs))


def _both_jobs(grid, first, second):
    a, b = first(grid), second(grid)
    na, nao = len(a.in_specs), len(a.out_specs)

    def run(ins, outs):
        a.run(ins[:na], outs[:nao])
        b.run(ins[na:], outs[nao:])

    return _SideJob(a.in_specs + b.in_specs, a.out_specs + b.out_specs, a.out_shape + b.out_shape,
                    a.args + b.args, run)


def _call(body, grid, in_specs, out_specs, out_shape, args, name, scratch_shapes=(), cast=None):
    sem = ("arbitrary",) * len(grid)
    if cast is None:
        out = pl.pallas_call(body, grid=grid, in_specs=in_specs, out_specs=out_specs, out_shape=out_shape,
                             scratch_shapes=scratch_shapes, compiler_params=_params(sem), name=name)(*args)
        return tuple(out)
    job = cast(grid) if callable(cast) else _cast_job(grid, *cast)
    a = len(in_specs)
    b = a + len(job.in_specs)
    c = b + len(out_specs)
    e = c + len(job.out_specs)

    def kernel(*refs):
        body(*refs[:a], *refs[b:c], *refs[e:])
        job.run(refs[a:b], refs[c:e])

    out = pl.pallas_call(
        kernel, grid=grid,
        in_specs=[*in_specs, *job.in_specs],
        out_specs=[*out_specs, *job.out_specs],
        out_shape=[*out_shape, *job.out_shape],
        scratch_shapes=scratch_shapes, compiler_params=_params(sem), name=name)(*args, *job.args)
    return tuple(out)


def _matmul(x, w, kernel, out_dtype, name, bias=None, cast=None):
    w_arr, layer = w
    m, k = x.shape
    n = w_arr.shape[2]
    in_specs = [
        pl.BlockSpec((TM, k), lambda i, j: (i, 0)),
        pl.BlockSpec((None, k, TN), lambda i, j: (layer, 0, j)),
    ]
    args = [x, w_arr]
    if bias is not None:
        in_specs.append(pl.BlockSpec((1, TN), lambda i, j: (0, j)))
        args.append(bias.reshape(1, n))
    return _call(kernel, (m // TM, n // TN), in_specs,
                 [pl.BlockSpec((TM, TN), lambda i, j: (i, j))],
                 [jax.ShapeDtypeStruct((m, n), out_dtype)], args, name, cast=cast)


def _mm_residual_kernel(a_ref, w_ref, x_ref, g_ref, o_ref):
    o_ref[...] = x_ref[...] + g_ref[...] * _dot(a_ref[...], w_ref[...])


def _mm_residual_ksplit_kernel(a_ref, w_ref, x_ref, g_ref, o_ref, acc_ref, *, nk):
    kk = pl.program_id(1)
    j = pl.program_id(2)

    @pl.when(kk == 0)
    def _():
        acc_ref[j] = _dot(a_ref[...], w_ref[...])

    @pl.when((kk > 0) & (kk < nk - 1))
    def _():
        acc_ref[j] += _dot(a_ref[...], w_ref[...])

    @pl.when(kk == nk - 1)
    def _():
        o_ref[...] = x_ref[...] + g_ref[...] * (acc_ref[j] + _dot(a_ref[...], w_ref[...]))


def _matmul_residual(a, w, x, mods, seg0, rows_per_seg, g_chunk, name, cast=None):
    w_arr, layer = w
    m, k = a.shape
    n = w_arr.shape[2]
    tiles_per_seg = rows_per_seg // TM
    seg = lambda i: seg0 + i // tiles_per_seg
    if k <= TK:
        nb = n // TN
        return _call(
            _mm_residual_kernel, (m // TM, nb),
            [
                pl.BlockSpec((TM, k), lambda i, j: (i, 0)),
                pl.BlockSpec((None, k, TN), lambda i, j: (layer, 0, j)),
                pl.BlockSpec((TM, TN), lambda i, j: (i, j)),
                pl.BlockSpec((None, 1, TN), lambda i, j: (seg(i), 0, g_chunk * nb + j)),
            ],
            [pl.BlockSpec((TM, TN), lambda i, j: (i, j))],
            [jax.ShapeDtypeStruct((m, n), F32)], (a, w_arr, x, mods), name, cast=cast)
    nk = k // TK
    tn = TN_KSPLIT
    nb = n // tn
    out_col = lambda kk, j: jnp.where(kk == nk - 1, j, 0)
    return _call(
        functools.partial(_mm_residual_ksplit_kernel, nk=nk), (m // TM, nk, nb),
        [
            pl.BlockSpec((TM, TK), lambda i, kk, j: (i, kk)),
            pl.BlockSpec((None, TK, tn), lambda i, kk, j: (layer, kk, j)),
            pl.BlockSpec((TM, tn), lambda i, kk, j: (i, out_col(kk, j))),
            pl.BlockSpec((None, 1, tn), lambda i, kk, j: (seg(i), 0, g_chunk * nb + out_col(kk, j))),
        ],
        [pl.BlockSpec((TM, tn), lambda i, kk, j: (i, out_col(kk, j)))],
        [jax.ShapeDtypeStruct((m, n), F32)], (a, w_arr, x, mods), name,
        scratch_shapes=[pltpu.VMEM((nb, TM, tn), F32)], cast=cast)


def _merge_kernel(hy_ref, at_ref, gm_ref, whv_ref, wat_ref, wgm_ref, ga_ref, gb_ref, gc_ref, o_ref):
    merged = (ga_ref[...] * _dot(hy_ref[...], whv_ref[...])
              + gb_ref[...] * _dot(at_ref[...], wat_ref[...])
              + gc_ref[...] * _dot(gm_ref[...], wgm_ref[...]))
    o_ref[...] = merged.astype(o_ref.dtype)


def _merge(hy, at, gm, w_hy, w_at, w_gm, gates):
    m = hy.shape[0]
    n = w_hy[0].shape[2]
    tn = TN_MERGE
    nb = n // tn
    row = lambda width: pl.BlockSpec((TM, width), lambda i, j: (i, 0))
    col = lambda w: pl.BlockSpec((None, w[0].shape[1], tn), lambda i, j: (w[1], 0, j))
    gate = lambda b: pl.BlockSpec((TM, tn), lambda i, j: (i, b * nb + j))
    return _call(
        _merge_kernel, (m // TM, nb),
        [row(hy.shape[1]), row(at.shape[1]), row(gm.shape[1]), col(w_hy), col(w_at), col(w_gm),
         gate(0), gate(1), gate(2)],
        [pl.BlockSpec((TM, tn), lambda i, j: (i, j))],
        [jax.ShapeDtypeStruct((m, n), BF16)],
        (hy, at, gm, w_hy[0], w_at[0], w_gm[0], gates, gates, gates), "merge")[0]


def _dft_tables(L):
    k = np.arange(L)
    ang = ((k[:, None] * k[None, :]) % (2 * L)) * (np.pi / L)
    cos, sin = np.cos(ang), np.sin(ang)
    alt = np.where(k % 2 == 0, 1.0, -1.0)
    fwd = np.concatenate([cos, -sin], axis=0)
    inv = np.concatenate([cos, -sin], axis=1)
    fwd[L, :] = alt
    inv[:, L] = alt
    const = lambda a: jnp.asarray(a.astype(np.float32).astype(BF16))
    return {'cos': const(cos), 'sin': const(sin), 'fwd': const(fwd), 'inv': const(inv)}


def _filter_features(L):
    t = np.linspace(0.0, 1.0, L)
    w = 2.0 * np.pi * np.arange(L) / L
    bands = np.linspace(1e-4, POS_BANDS - 1, POS_BANDS)
    z = np.concatenate([t[:, None], np.cos(w[:, None] * bands), -np.sin(w[:, None] * bands)], axis=-1)
    z = np.pad(z, ((0, 0), (0, V7X_LANES - POS_DIM)))
    return jnp.asarray(z.astype(np.float32)), jnp.asarray(t[:, None].astype(np.float32))


def _filter_kernel(z_ref, t_ref, w1_ref, b1_ref, f1_ref, w2_ref, b2_ref, f2_ref,
                   w3f_ref, w3b_ref, dl_ref, cos_ref, sin_ref, o_ref, ahi_ref, alo_ref, *, L):
    @pl.when(pl.program_id(0) == 0)
    def _():
        z_hi, z_lo = _split(z_ref[...])
        a = jnp.sin(f1_ref[...] * (_dot3(z_hi, z_lo, *_split(w1_ref[...])) + b1_ref[...]))
        a = jnp.sin(f2_ref[...] * (_dot3(*_split(a), *_split(w2_ref[...])) + b2_ref[...]))
        ahi_ref[...], alo_ref[...] = _split(a)

    a_hi, a_lo = ahi_ref[...], alo_ref[...]
    decay = jnp.exp(-t_ref[...] * dl_ref[...])
    row = lax.broadcasted_iota(jnp.int32, (L, 1), 0)
    alt = jnp.where(row % 2 == 0, 1.0, -1.0).astype(F32)
    for o in range(HY_ORDER):
        fwd = _dot3(a_hi, a_lo, *_split(w3f_ref[o])) * decay
        bwd = _dot3(a_hi, a_lo, *_split(w3b_ref[o])) * decay
        s = fwd + jnp.where(row == 0, 0.0, bwd)
        d = bwd - fwd
        hr = _dot(cos_ref[...], s.astype(BF16)) * (1.0 / L)
        hi = _dot(sin_ref[...], d.astype(BF16)) * (1.0 / L)
        nyq = jnp.sum(alt * s, axis=0, keepdims=True) * (0.5 / L)
        o_ref[o, 0] = jnp.where(row == 0, 0.5 * hr, hr)
        o_ref[o, 1] = hi
        o_ref[o, 2] = jnp.where(row == 0, nyq, hr)


def _filter_spectra(L, p, tables):
    z, t = _filter_features(L)
    w1 = jnp.pad(p['flt_w1'], ((0, V7X_LANES - POS_DIM), (0, 0)))
    w3 = p['flt_w3'].reshape(FILT_HID, HY_ORDER, 2, D_H)
    w3f = jnp.transpose(w3[:, :, 0], (1, 0, 2))
    w3b = jnp.transpose(w3[:, :, 1], (1, 0, 2))
    deltas = jnp.asarray(np.abs(np.linspace(DECAY_MIN, DECAY_MAX, D_H)).astype(np.float32).reshape(1, D_H))
    full = lambda shape: pl.BlockSpec(shape, lambda c: (0,) * len(shape))
    return pl.pallas_call(
        functools.partial(_filter_kernel, L=L),
        grid=(D_H // HY_TC,),
        in_specs=[
            full((L, V7X_LANES)), full((L, 1)),
            full((V7X_LANES, FILT_HID)), full((1, FILT_HID)), full((1, FILT_HID)),
            full((FILT_HID, FILT_HID)), full((1, FILT_HID)), full((1, FILT_HID)),
            pl.BlockSpec((HY_ORDER, FILT_HID, HY_TC), lambda c: (0, 0, c)),
            pl.BlockSpec((HY_ORDER, FILT_HID, HY_TC), lambda c: (0, 0, c)),
            pl.BlockSpec((1, HY_TC), lambda c: (0, c)),
            full((L, L)), full((L, L)),
        ],
        out_specs=pl.BlockSpec((HY_ORDER, 3, L, HY_TC), lambda c: (0, 0, 0, c)),
        out_shape=jax.ShapeDtypeStruct((HY_ORDER, 3, L, D_H), F32),
        scratch_shapes=[pltpu.VMEM((L, FILT_HID), BF16), pltpu.VMEM((L, FILT_HID), BF16)],
        compiler_params=_params(("arbitrary",)),
        name=f"hyena_filter_{L}",
    )(z, t, w1, p['flt_b1'].reshape(1, -1), p['flt_f1'].reshape(1, -1),
      p['flt_w2'], p['flt_b2'].reshape(1, -1), p['flt_f2'].reshape(1, -1),
      w3f, w3b, deltas, tables['cos'], tables['sin'])


def _hyena_kernel(v_ref, x1_ref, x2_ref, wv_ref, wx1_ref, wx2_ref, bv_ref, bx1_ref, bx2_ref,
                  tab_ref, skip_ref, fwd_ref, inv_ref, o_ref, *, L):
    row = lax.broadcasted_iota(jnp.int32, (L, 1), 0)

    def short_conv(u_ref, rows, w_ref, b_ref):
        u = u_ref[rows, :]
        prev = jnp.where(row == 0, 0.0, pltpu.roll(u, 1, 0))
        nxt = jnp.where(row == L - 1, 0.0, pltpu.roll(u, L - 1, 0))
        return prev * w_ref[0:1, :] + u * w_ref[1:2, :] + nxt * w_ref[2:3, :] + b_ref[...]

    seqs = [slice(s * L, (s + 1) * L) for s in range(v_ref.shape[0] // L)]
    zs = [short_conv(v_ref, rows, wv_ref, bv_ref) for rows in seqs]
    gates = [(short_conv(x1_ref, rows, wx1_ref, bx1_ref), short_conv(x2_ref, rows, wx2_ref, bx2_ref))
             for rows in seqs]
    for o in range(HY_ORDER):
        p, q, r = tab_ref[o, 0], tab_ref[o, 1], tab_ref[o, 2]
        specs = [_dot(fwd_ref[...], z.astype(BF16)) for z in zs]
        prods = [(spec[:L] * p - spec[L:] * q, spec[:L] * q + spec[L:] * r) for spec in specs]
        ys = [_dot(inv_ref[...], jnp.concatenate([ya, yb], axis=0).astype(BF16)) for ya, yb in prods]
        zs = [g[o] * (y + skip_ref[o:o + 1, :] * z) for g, y, z in zip(gates, ys, zs)]
    for rows, z in zip(seqs, zs):
        o_ref[rows, :] = z.astype(o_ref.dtype)


def _hyena(proj, L, spectra, tables, p, cast):
    m = proj.shape[0]
    tc = min(D_H, HY_TILE_ELEMS // L)
    nc = D_H // tc
    rows = (HY_SEQS if tc < D_H else 1) * L
    u = lambda part: pl.BlockSpec((rows, tc), lambda c, b: (b, part * nc + c))
    cw = lambda part: pl.BlockSpec((3, tc), lambda c, b: (0, part * nc + c))
    cb = lambda part: pl.BlockSpec((1, tc), lambda c, b: (0, part * nc + c))
    conv_b = p['conv_b'].reshape(1, -1)
    return _call(
        functools.partial(_hyena_kernel, L=L), (nc, m // rows),
        [u(0), u(1), u(2), cw(0), cw(1), cw(2), cb(0), cb(1), cb(2),
         pl.BlockSpec((HY_ORDER, 3, L, tc), lambda c, b: (0, 0, 0, c)),
         pl.BlockSpec((HY_ORDER, tc), lambda c, b: (0, c)),
         pl.BlockSpec((2 * L, L), lambda c, b: (0, 0)),
         pl.BlockSpec((L, 2 * L), lambda c, b: (0, 0))],
        [pl.BlockSpec((rows, tc), lambda c, b: (b, c))],
        [jax.ShapeDtypeStruct((m, D_H), BF16)],
        (proj, proj, proj, p['conv_w'], p['conv_w'], p['conv_w'], conv_b, conv_b, conv_b,
         spectra, p['hy_skip'], tables['fwd'], tables['inv']),
        f"hyena_{L}", cast=cast)


def _gmlp_kernel(u_ref, v_ref, lg_ref, lb_ref, ws_ref, bs_ref, o_ref):
    v = v_ref[...]
    mu = jnp.mean(v, axis=-1, keepdims=True)
    vc = v - mu
    var = jnp.mean(vc * vc, axis=-1, keepdims=True)
    vn = (vc * lax.rsqrt(var + EPS) * lg_ref[...] + lb_ref[...]).astype(BF16)
    for n in range(GM_ROWS // CHUNK):
        rows = slice(n * CHUNK, (n + 1) * CHUNK)
        for g in range(N_GM_GROUPS):
            cols = slice(g * CHUNK, (g + 1) * CHUNK)
            mixed = _dot(ws_ref[g], vn[rows, cols]) + bs_ref[:, g:g + 1]
            o_ref[rows, cols] = (u_ref[rows, cols] * mixed).astype(o_ref.dtype)


def _gmlp(proj, p):
    m = proj.shape[0]
    cu = COL_GM // D_G
    return pl.pallas_call(
        _gmlp_kernel,
        grid=(m // GM_ROWS,),
        in_specs=[
            pl.BlockSpec((GM_ROWS, D_G), lambda i: (i, cu)),
            pl.BlockSpec((GM_ROWS, D_G), lambda i: (i, cu + 1)),
            pl.BlockSpec((1, D_G), lambda i: (0, 0)),
            pl.BlockSpec((1, D_G), lambda i: (0, 0)),
            pl.BlockSpec((N_GM_GROUPS, CHUNK, CHUNK), lambda i: (0, 0, 0)),
            pl.BlockSpec((CHUNK, N_GM_GROUPS), lambda i: (0, 0)),
        ],
        out_specs=pl.BlockSpec((GM_ROWS, D_G), lambda i: (i, 0)),
        out_shape=jax.ShapeDtypeStruct((m, D_G), BF16),
        compiler_params=_params(("arbitrary",)),
        name="gmlp",
    )(proj, proj, p['gm_ln_g'].reshape(1, -1), p['gm_ln_b'].reshape(1, -1),
      p['gm_ws'].astype(BF16), p['gm_bs'].T)


def _rms(x, g):
    return x * lax.rsqrt(jnp.mean(x * x, axis=-1, keepdims=True) + EPS) * g


def _rope(x, cos, sin_signed, lane):
    partner = jnp.where((lane % 64) < ROT_FREQS,
                        pltpu.roll(x, HEAD_DIM - ROT_FREQS, 1), pltpu.roll(x, ROT_FREQS, 1))
    return x * cos + partner * sin_signed


def _softmax_pv(problems):
    dims = (((1,), (1,)), ((), ()))
    outs = []
    for first in range(0, len(problems), ATT_GROUP):
        group = problems[first:first + ATT_GROUP]
        ss = [lax.dot_general(q, k, dims, preferred_element_type=F32) * ATT_SCALE for q, k, _, _, _ in group]
        ss = [s if pr[3] is None else jnp.where(pr[3], s, NEG) for s, pr in zip(ss, group)]
        ms = [jnp.maximum(jnp.max(s, axis=-1, keepdims=True), pr[4]) for s, pr in zip(ss, group)]
        es = [jnp.exp(s - m) for s, m in zip(ss, ms)]
        dens = [jnp.sum(e, axis=-1, keepdims=True) + jnp.exp(pr[4] - m) for e, m, pr in zip(es, ms, group)]
        outs += [_dot(e.astype(BF16), pr[2]) / den for e, den, pr in zip(es, dens, group)]
    return outs


def _sink_column(sink_ref, rows):
    return jnp.concatenate(
        [jnp.broadcast_to(sink_ref[g:g + 1, 0:1], (rows, 1)) for g in range(GQA_GROUP)], axis=0)


def _ctx_att_kernel(qa_ref, qb_ref, k_ref, v_ref, qg_ref, kg_ref, sink_ref, o_ref, ko_ref, vo_ref, *,
                    L, layer, zero_other_layers):
    if zero_other_layers:
        for d in range(ko_ref.shape[0]):
            if d != layer:
                ko_ref[d] = jnp.zeros(ko_ref.shape[1:], ko_ref.dtype)
                vo_ref[d] = jnp.zeros(vo_ref.shape[1:], vo_ref.dtype)
        ko_ref, vo_ref = ko_ref.at[layer], vo_ref.at[layer]
    qw = GQA_GROUP * HEAD_DIM
    q_halves = (qa_ref, qb_ref)
    chunks = range(0, GQA_GROUP * L, ATT_ROWS)
    problems = []
    for h in range(N_KV_HEADS):
        q_ref = q_halves[h // 2]
        q0 = (h % 2) * qw
        q_all = jnp.concatenate(
            [_rms(q_ref[:, q0 + g * HEAD_DIM:q0 + (g + 1) * HEAD_DIM], qg_ref[...]) for g in range(GQA_GROUP)],
            axis=0).astype(BF16)
        cols = slice(h * HEAD_DIM, (h + 1) * HEAD_DIM)
        k = _rms(k_ref[:, cols], kg_ref[...])
        v = v_ref[:, cols]
        ko_ref[:, h, :] = k
        vo_ref[:, h, :] = v
        k, v = k.astype(BF16), v.astype(BF16)
        sink = _sink_column(sink_ref.at[h], L)
        problems += [(q_all[r:r + ATT_ROWS], k, v, None, sink[r:r + ATT_ROWS]) for r in chunks]
    outs = _softmax_pv(problems)
    for h in range(N_KV_HEADS):
        out = jnp.concatenate(outs[h * len(chunks):(h + 1) * len(chunks)], axis=0)
        for g in range(GQA_GROUP):
            o_ref[:, h * qw + g * HEAD_DIM:h * qw + (g + 1) * HEAD_DIM] = out[g * L:(g + 1) * L].astype(o_ref.dtype)


def _sink_rows(sink):
    s = sink.reshape(N_KV_HEADS, GQA_GROUP, 1)
    s = jnp.pad(s, ((0, 0), (0, PAD_ROWS - GQA_GROUP), (0, 0)))
    return jnp.broadcast_to(s, (N_KV_HEADS, PAD_ROWS, V7X_LANES))


def _context_attention(proj, L, p, layer, depth, kv_prev):
    m = proj.shape[0]
    b = m // L
    half = ATT_W // 2
    kv_shape = jax.ShapeDtypeStruct((b, depth, L, N_KV_HEADS, HEAD_DIM), F32)
    if kv_prev is None:
        kv_spec = pl.BlockSpec((None, depth, L, N_KV_HEADS, HEAD_DIM), lambda i: (i, 0, 0, 0, 0))
    else:
        kv_spec = pl.BlockSpec((None, None, L, N_KV_HEADS, HEAD_DIM), lambda i: (i, layer, 0, 0, 0))
    in_specs = [
        pl.BlockSpec((L, half), lambda i: (i, COL_Q // half)),
        pl.BlockSpec((L, half), lambda i: (i, COL_Q // half + 1)),
        pl.BlockSpec((L, KV_W), lambda i: (i, COL_K // KV_W)),
        pl.BlockSpec((L, KV_W), lambda i: (i, COL_V // KV_W)),
        pl.BlockSpec((1, HEAD_DIM), lambda i: (0, 0)),
        pl.BlockSpec((1, HEAD_DIM), lambda i: (0, 0)),
        pl.BlockSpec((N_KV_HEADS, PAD_ROWS, V7X_LANES), lambda i: (0, 0, 0)),
    ]
    args = [proj, proj, proj, proj, p['q_norm_g'].reshape(1, -1), p['k_norm_g'].reshape(1, -1),
            _sink_rows(p['sink'])]
    kernel = functools.partial(_ctx_att_kernel, L=L, layer=layer, zero_other_layers=kv_prev is None)
    aliases = {}
    if kv_prev is not None:
        n_in = len(in_specs)
        in_specs += [pl.BlockSpec(memory_space=pl.ANY)] * 2
        args += list(kv_prev)
        aliases = {n_in: 1, n_in + 1: 2}
        inner = kernel
        kernel = lambda *refs: inner(*refs[:n_in], *refs[n_in + 2:])
    return pl.pallas_call(
        kernel,
        grid=(b,),
        in_specs=in_specs,
        out_specs=[pl.BlockSpec((L, ATT_W), lambda i: (i, 0)), kv_spec, kv_spec],
        out_shape=[jax.ShapeDtypeStruct((m, ATT_W), BF16), kv_shape, kv_shape],
        input_output_aliases=aliases,
        compiler_params=_params(("arbitrary",)),
        name="context_attention",
    )(*args)


def _rope_tables(L):
    rows = L // GRID_W
    row = np.repeat(np.arange(rows), GRID_W)
    col = np.tile(np.arange(GRID_W), rows)
    inv = ROPE_BASE ** (-np.arange(ROT_FREQS) / ROT_FREQS)
    pos = np.stack([row, col], axis=-1).astype(np.float64)
    ang = pos[:, :, None] * inv
    cos, sin = np.cos(ang), np.sin(ang)
    cos_t = np.stack([cos, cos], axis=2).reshape(L, HEAD_DIM)
    sin_t = np.stack([-sin, sin], axis=2).reshape(L, HEAD_DIM)
    return jnp.asarray(cos_t.astype(np.float32)), jnp.asarray(sin_t.astype(np.float32))


def _lat_att_kernel(q_ref, k_ref, v_ref, kc_ref, vc_ref, cos_ref, sin_ref, qg_ref, kg_ref, sink_ref,
                    o_ref, *, nb):
    lane = lax.broadcasted_iota(jnp.int32, (1, HEAD_DIM), 1)
    k_seq = _rope(_rms(k_ref[...], kg_ref[...]), cos_ref[...], sin_ref[...], lane).astype(BF16)
    v_seq = v_ref[...].astype(BF16)
    k_ctx = kc_ref[...].astype(BF16)
    v_ctx = vc_ref[...].astype(BF16)
    n_ctx = k_ctx.shape[0]
    sink_col = _sink_column(sink_ref, BLOCK)

    def band_mask(has_prev, has_next):
        n_keys = n_ctx + (1 + has_prev + has_next) * BLOCK
        shape = (GQA_GROUP * BLOCK, n_keys)
        i = lax.broadcasted_iota(jnp.int32, shape, 0) % BLOCK
        c = lax.broadcasted_iota(jnp.int32, shape, 1)
        ok = None
        if has_prev:
            ok = (c < n_ctx) | (c >= n_ctx + BLOCK) | (c - n_ctx >= i)
        if has_next:
            start = n_keys - BLOCK
            nxt = (c < start) | (c - start <= i)
            ok = nxt if ok is None else ok & nxt
        return ok

    masks = {}
    problems = []
    for n in range(nb):
        lo, hi = max(n - 1, 0), min(n + 1, nb - 1)
        shape_key = (lo < n, hi > n)
        if shape_key not in masks:
            masks[shape_key] = band_mask(*shape_key)
        rows = slice(n * BLOCK, (n + 1) * BLOCK)
        cos, sin = cos_ref[rows, :], sin_ref[rows, :]
        q_all = jnp.concatenate(
            [_rope(_rms(q_ref[rows, g * HEAD_DIM:(g + 1) * HEAD_DIM], qg_ref[...]), cos, sin, lane)
             for g in range(GQA_GROUP)], axis=0).astype(BF16)
        band = slice(lo * BLOCK, (hi + 1) * BLOCK)
        k_all = jnp.concatenate([k_ctx, k_seq[band]], axis=0)
        v_all = jnp.concatenate([v_ctx, v_seq[band]], axis=0)
        problems.append((q_all, k_all, v_all, masks[shape_key], sink_col))
    for n, out in enumerate(_softmax_pv(problems)):
        rows = slice(n * BLOCK, (n + 1) * BLOCK)
        for g in range(GQA_GROUP):
            o_ref[rows, g * HEAD_DIM:(g + 1) * HEAD_DIM] = out[g * BLOCK:(g + 1) * BLOCK].astype(o_ref.dtype)


def _window_attention(proj, L, cache_k, cache_v, layer, p):
    m = proj.shape[0]
    b = m // L
    nb = L // BLOCK
    n_ctx = cache_k.shape[2]
    qw = GQA_GROUP * HEAD_DIM
    cos_t, sin_t = _rope_tables(L)
    seq = lambda col0: pl.BlockSpec((L, HEAD_DIM), lambda i, h: (i, col0 // HEAD_DIM + h))
    cache = pl.BlockSpec((None, None, n_ctx, HEAD_DIM), lambda i, h: (i, layer, 0, h))
    table = pl.BlockSpec((L, HEAD_DIM), lambda i, h: (0, 0))
    gain = pl.BlockSpec((1, HEAD_DIM), lambda i, h: (0, 0))
    return pl.pallas_call(
        functools.partial(_lat_att_kernel, nb=nb),
        grid=(b, N_KV_HEADS),
        in_specs=[
            pl.BlockSpec((L, qw), lambda i, h: (i, COL_Q // qw + h)),
            seq(COL_K), seq(COL_V), cache, cache, table, table, gain, gain,
            pl.BlockSpec((None, PAD_ROWS, V7X_LANES), lambda i, h: (h, 0, 0)),
        ],
        out_specs=pl.BlockSpec((L, qw), lambda i, h: (i, h)),
        out_shape=jax.ShapeDtypeStruct((m, ATT_W), BF16),
        compiler_params=_params(("arbitrary", "arbitrary")),
        name="window_attention",
    )(proj, proj, proj, cache_k, cache_v, cos_t, sin_t,
      p['q_norm_g'].reshape(1, -1), p['k_norm_g'].reshape(1, -1), _sink_rows(p['sink']))


def _mixer_branches(proj, L, p, tables, cache, layer, cast, depth=None, kv_prev=None):
    hy, *w_cast = _hyena(proj, L, _filter_spectra(L, p, tables), tables, p, cast)
    gm = _gmlp(proj, p)
    if cache is None:
        att, k_new, v_new = _context_attention(proj, L, p, layer, depth, kv_prev)
    else:
        att = _window_attention(proj, L, cache[0], cache[1], layer, p)
        k_new = v_new = None
    return (hy, att, gm), w_cast, k_new, v_new


_LAYER_PARAMS = ('norm1_g', 'norm2_g', 'conv_w', 'conv_b', 'flt_w1', 'flt_b1', 'flt_f1', 'flt_w2',
                 'flt_b2', 'flt_f2', 'flt_w3', 'hy_skip', 'q_norm_g', 'k_norm_g', 'sink',
                 'gm_ln_g', 'gm_ln_b', 'gm_ws', 'gm_bs', 'b_gate')


def kernel(x_prompt, x_sample, cache_k, cache_v, c, c_ctx, w_mod, b_mod, norm1_g, norm2_g,
           w_in, conv_w, conv_b, flt_w1, flt_b1, flt_f1, flt_w2, flt_b2, flt_f2, flt_w3,
           hy_skip, q_norm_g, k_norm_g, sink, gm_ln_g, gm_ln_b, gm_ws, gm_bs,
           w_p_hy, w_p_at, w_p_gm, w_gate, b_gate, w_out, w_up, w_down):
    args = dict(locals())
    batch, seq, d = x_prompt.shape
    dec_batch, dec_seq, _ = x_sample.shape
    depth = w_mod.shape[0]
    assert d == D_MODEL and w_in.shape[2] == IN_COLS and 1 + dec_batch <= PAD_ROWS

    c_rows = jnp.concatenate(
        [c_ctx[None, :], c, jnp.zeros((PAD_ROWS - 1 - dec_batch, d), F32)], axis=0)
    head = 2 * d
    mods = _modulation(c_rows, w_mod, b_mod, 0, head)

    tables = {L: _dft_tables(L) for L in (seq, dec_seq)}
    cache = (cache_k.reshape(dec_batch, depth, -1, KV_W), cache_v.reshape(dec_batch, depth, -1, KV_W))

    y_p = x_prompt.reshape(batch * seq, d)
    y_s = x_sample.reshape(dec_batch * dec_seq, d)
    new_kv = None
    ctx_seg = (0, batch * seq)
    lat_seg = (1, dec_seq)
    w_in_b = (w_in[:1].astype(BF16), 0)
    for l in range(depth):
        p = {name: args[name][l] for name in _LAYER_PARAMS}
        m_l = mods.reshape(PAD_ROWS, 1, -1)
        h_c = _normmod(y_p, p['norm1_g'], m_l, *ctx_seg, sh_chunk=0, sc_chunk=1)
        h_s = _normmod(y_s, p['norm1_g'], m_l, *lat_seg, sh_chunk=0, sc_chunk=1)
        proj_c, w_gate_b = _matmul(h_c, w_in_b, _mm_kernel, F32, "in_proj", cast=(w_gate, l))
        proj_s, w_p_at_b = _matmul(h_s, w_in_b, _mm_kernel, F32, "in_proj", cast=(w_p_at, l))
        gates_c, w_up_b = _matmul(h_c, (w_gate_b, 0), _mm_sigmoid_kernel, BF16, "gate_proj",
                                  bias=p['b_gate'], cast=(w_up, l))
        gates_s, w_out_b = _matmul(h_s, (w_gate_b, 0), _mm_sigmoid_kernel, BF16, "gate_proj",
                                   bias=p['b_gate'], cast=(w_out, l))
        hy_job = functools.partial(_cast_job, src=w_p_hy, layer=l)
        if l == 0:
            tail = functools.partial(_mod_job, c_rows=c_rows, w_mod=w_mod, b_mod=b_mod, layer=0, col0=head)
            hy_job = functools.partial(_both_jobs, first=hy_job, second=tail)
        br_c, hy_side, *new_kv = _mixer_branches(proj_c, seq, p, tables[seq], None, l, hy_job, depth, new_kv)
        w_p_hy_b = hy_side[0]
        if l == 0:
            m_l = jnp.concatenate([mods, hy_side[1]], axis=1).reshape(PAD_ROWS, 1, -1)
        br_s, (w_p_gm_b,), _, _ = _mixer_branches(proj_s, dec_seq, p, tables[dec_seq], cache, l, (w_p_gm, l))
        w_branch = ((w_p_hy_b, 0), (w_p_at_b, 0), (w_p_gm_b, 0))
        merged_c = _merge(*br_c, *w_branch, gates_c)
        merged_s = _merge(*br_s, *w_branch, gates_s)
        y_p, = _matmul_residual(merged_c, (w_out_b, 0), y_p, m_l, *ctx_seg, 2, "out_proj")
        y_s, = _matmul_residual(merged_s, (w_out_b, 0), y_s, m_l, *lat_seg, 2, "out_proj")
        h_c = _normmod(y_p, p['norm2_g'], m_l, *ctx_seg, sh_chunk=3, sc_chunk=4)
        h_s = _normmod(y_s, p['norm2_g'], m_l, *lat_seg, sh_chunk=3, sc_chunk=4)
        act_c, w_down_b = _matmul(h_c, (w_up_b, 0), _mm_relu2_kernel, BF16, "mlp_up", cast=(w_down, l))
        if l + 1 < depth:
            next_mods = functools.partial(_mod_job, c_rows=c_rows, w_mod=w_mod, b_mod=b_mod, layer=l + 1)
            act_s, mods = _matmul(h_s, (w_up_b, 0), _mm_relu2_kernel, BF16, "mlp_up", cast=next_mods)
        else:
            act_s, = _matmul(h_s, (w_up_b, 0), _mm_relu2_kernel, BF16, "mlp_up")
        if l + 1 < depth:
            y_p, w_in_next = _matmul_residual(act_c, (w_down_b, 0), y_p, m_l, *ctx_seg, 5, "mlp_down",
                                              cast=(w_in, l + 1))
            w_in_b = (w_in_next, 0)
        else:
            y_p, = _matmul_residual(act_c, (w_down_b, 0), y_p, m_l, *ctx_seg, 5, "mlp_down")
        y_s, = _matmul_residual(act_s, (w_down_b, 0), y_s, m_l, *lat_seg, 5, "mlp_down")
    return (y_p.reshape(batch, seq, d), y_s.reshape(dec_batch, dec_seq, d),
            new_kv[0], new_kv[1])
```
